```python
import jax, jax.numpy as jnp
from jax import lax
import numpy as np

D_MODEL = 2048
BATCH = 4
SEQ = 2048
DEPTH = 2
DEC_BATCH = 128
DEC_SEQ = 1
PAST_LEN = 16384
PAGE_SIZE = 128

N_MIXERS = 2
CHUNK = 128
E_A = D_MODEL
H_A = 8
DH_A = E_A // H_A
E_B = D_MODEL
CONV_K = 3
D_FF = ((8 * D_MODEL // 3 + 255) // 256) * 256
D_PLE = 256
N_A = (DEPTH + 1) // 2
N_B = DEPTH // 2
EPS = 1e-6

kernel_name = "hybrid_gmlp_shortconv_macaron_decode_step"


def _rmsnorm(x, g):
    xf = x.astype(jnp.float32)
    y = xf * lax.rsqrt(jnp.mean(xf * xf, axis=-1, keepdims=True) + EPS)
    return (y * g.astype(jnp.float32)).astype(x.dtype)


def _layernorm(x, g, b):
    xf = x.astype(jnp.float32)
    mu = jnp.mean(xf, axis=-1, keepdims=True)
    var = jnp.mean(jnp.square(xf - mu), axis=-1, keepdims=True)
    y = (xf - mu) * lax.rsqrt(var + EPS)
    return (y * g.astype(jnp.float32) + b.astype(jnp.float32)).astype(x.dtype)


def _swiglu(x, w_gate, w_up, w_down):
    return (jax.nn.silu(x @ w_gate) * (x @ w_up)) @ w_down


def _chunk_gmlp_mixer(a, w_in, ln_g, ln_b, w_s, b_s, w_out):
    n, L, _ = a.shape
    c = min(L, CHUNK)
    z = jax.nn.gelu(a @ w_in)
    u, v = jnp.split(z, 2, axis=-1)
    v = _layernorm(v, ln_g, ln_b)
    mask = jnp.tril(jnp.ones((c, c), dtype=bool))
    w = w_s[:, :c, :c]
    w = jnp.where(mask[None], w, jnp.zeros_like(w)).astype(v.dtype)
    vg = v.reshape(n, L // c, c, H_A, DH_A)
    s = jnp.einsum('hts,nkshd->nkthd', w, vg) + b_s[:, :c].T.astype(v.dtype)[None, None, :, :, None]
    y = u * s.reshape(n, L, E_A)
    return y @ w_out, v


def _short_conv_mixer(a, conv_state, w_in, w_conv, w_out):
    L = a.shape[1]
    bg, cg, xh = jnp.split(a @ w_in, 3, axis=-1)
    ci = cg * xh
    xp = jnp.concatenate([conv_state.astype(ci.dtype), ci], axis=1)
    co = w_conv[0] * xp[:, 0:L]
    for k in range(1, CONV_K):
        co = co + w_conv[k] * xp[:, k:k + L]
    return (bg * co) @ w_out, xp[:, L:]


def _trunk(x, p, conv_state, W):
    h = x
    conv_new, v_new = [], []
    for i in range(DEPTH):
        h = h + 0.5 * _swiglu(_rmsnorm(h, W['ffn1_norm'][i]), W['ffn1_w_gate'][i], W['ffn1_w_up'][i], W['ffn1_w_down'][i])
        a = _rmsnorm(h, W['mix_norm'][i])
        j = i // N_MIXERS
        if i % N_MIXERS == 0:
            m, v = _chunk_gmlp_mixer(a, W['a_w_in'][j], W['a_ln_g'][j], W['a_ln_b'][j], W['a_w_s'][j], W['a_b_s'][j], W['a_w_out'][j])
            v_new.append(v)
        else:
            st = conv_state[j] if conv_state is not None else jnp.zeros((x.shape[0], CONV_K - 1, E_B), x.dtype)
            m, st_new = _short_conv_mixer(a, st, W['c_w_in'][j], W['c_w_conv'][j], W['c_w_out'][j])
            conv_new.append(st_new)
        h = h + m
        h = h + 0.5 * _swiglu(_rmsnorm(h, W['ffn2_norm'][i]), W['ffn2_w_gate'][i], W['ffn2_w_up'][i], W['ffn2_w_down'][i])
        gate = jax.nn.sigmoid(_rmsnorm(h, W['ple_norm'][i]) @ W['ple_w_gate'][i])
        h = h + gate * (p[i] @ W['ple_w_proj'][i])
    y = _rmsnorm(h, W['final_norm'])
    return y, jnp.stack(conv_new), jnp.stack(v_new)


def setup_inputs(seed: int = 0) -> dict:
    key = jax.random.key(seed)
    ks = iter(jax.random.split(key, 40))

    def nrm(shape, scale):
        return jax.random.normal(next(ks), shape, jnp.float32) * scale

    def gain(shape):
        return 1.0 + 0.05 * jax.random.normal(next(ks), shape, jnp.float32)

    return {
        "x_prompt": nrm((BATCH, SEQ, D_MODEL), 1.0),
        "x_sample": nrm((DEC_BATCH, DEC_SEQ, D_MODEL), 1.0),
        "state_conv": nrm((N_B, DEC_BATCH, CONV_K - 1, E_B), 1.0),
        "p_prompt": nrm((DEPTH, BATCH, SEQ, D_PLE), 1.0),
        "p_sample": nrm((DEPTH, DEC_BATCH, DEC_SEQ, D_PLE), 1.0),
        "ffn1_norm": gain((DEPTH, D_MODEL)),
        "ffn1_w_gate": nrm((DEPTH, D_MODEL, D_FF), D_MODEL ** -0.5),
        "ffn1_w_up": nrm((DEPTH, D_MODEL, D_FF), D_MODEL ** -0.5),
        "ffn1_w_down": nrm((DEPTH, D_FF, D_MODEL), D_FF ** -0.5),
        "mix_norm": gain((DEPTH, D_MODEL)),
        "a_w_in": nrm((N_A, D_MODEL, 2 * E_A), D_MODEL ** -0.5),
        "a_ln_g": gain((N_A, E_A)),
        "a_ln_b": nrm((N_A, E_A), 0.02),
        "a_w_s": nrm((N_A, H_A, CHUNK, CHUNK), CHUNK ** -0.5),
        "a_b_s": 1.0 + nrm((N_A, H_A, CHUNK), 0.1),
        "a_w_out": nrm((N_A, E_A, D_MODEL), E_A ** -0.5),
        "c_w_in": nrm((N_B, D_MODEL, 3 * E_B), D_MODEL ** -0.5),
        "c_w_conv": nrm((N_B, CONV_K, E_B), CONV_K ** -0.5),
        "c_w_out": nrm((N_B, E_B, D_MODEL), E_B ** -0.5),
        "ffn2_norm": gain((DEPTH, D_MODEL)),
        "ffn2_w_gate": nrm((DEPTH, D_MODEL, D_FF), D_MODEL ** -0.5),
        "ffn2_w_up": nrm((DEPTH, D_MODEL, D_FF), D_MODEL ** -0.5),
        "ffn2_w_down": nrm((DEPTH, D_FF, D_MODEL), D_FF ** -0.5),
        "ple_norm": gain((DEPTH, D_MODEL)),
        "ple_w_gate": nrm((DEPTH, D_MODEL, D_MODEL), D_MODEL ** -0.5),
        "ple_w_proj": nrm((DEPTH, D_PLE, D_MODEL), D_PLE ** -0.5),
        "final_norm": gain((D_MODEL,)),
    }


def reference(x_prompt, x_sample, state_conv, p_prompt, p_sample,
              ffn1_norm, ffn1_w_gate, ffn1_w_up, ffn1_w_down,
              mix_norm, a_w_in, a_ln_g, a_ln_b, a_w_s, a_b_s, a_w_out,
              c_w_in, c_w_conv, c_w_out,
              ffn2_norm, ffn2_w_gate, ffn2_w_up, ffn2_w_down,
              ple_norm, ple_w_gate, ple_w_proj, final_norm):
    W = {
        'ffn1_norm': ffn1_norm, 'ffn1_w_gate': ffn1_w_gate, 'ffn1_w_up': ffn1_w_up, 'ffn1_w_down': ffn1_w_down,
        'mix_norm': mix_norm, 'a_w_in': a_w_in, 'a_ln_g': a_ln_g, 'a_ln_b': a_ln_b,
        'a_w_s': a_w_s, 'a_b_s': a_b_s, 'a_w_out': a_w_out,
        'c_w_in': c_w_in, 'c_w_conv': c_w_conv, 'c_w_out': c_w_out,
        'ffn2_norm': ffn2_norm, 'ffn2_w_gate': ffn2_w_gate, 'ffn2_w_up': ffn2_w_up, 'ffn2_w_down': ffn2_w_down,
        'ple_norm': ple_norm, 'ple_w_gate': ple_w_gate, 'ple_w_proj': ple_w_proj, 'final_norm': final_norm,
    }
    y_prompt, new_conv_prompt, _ = _trunk(x_prompt, p_prompt, None, W)
    y_sample, new_conv_sample, new_chunk_v_sample = _trunk(x_sample, p_sample, state_conv, W)
    return (y_prompt, y_sample, new_conv_prompt, new_conv_sample, new_chunk_v_sample)
```

```python
import functools

import jax
import jax.numpy as jnp
from jax import lax
from jax.experimental import pallas as pl
from jax.experimental.pallas import tpu as pltpu

F32 = jnp.float32
BF16 = jnp.bfloat16

D_MODEL = 2048
D_FF = 5632
E_MIX = 2048
N_HEAD = 8
D_HEAD = 256
CHUNK = 128
D_PLE = 256
SEQ = 2048
N_PROMPT = 4 * SEQ
N_SAMPLE = 128
N_TOK = N_PROMPT + N_SAMPLE
DEPTH = 2
EPS = 1e-6

TM = 640
N_TILE = N_TOK // TM
SAMPLE_ROW0 = TM - N_SAMPLE
TF = 512
TN = 512
VMEM_LIMIT = 56 * 1024 * 1024

_ARB2 = pltpu.CompilerParams(dimension_semantics=("arbitrary", "arbitrary"),
                             vmem_limit_bytes=VMEM_LIMIT)
_ARB1 = pltpu.CompilerParams(dimension_semantics=("arbitrary",),
                             vmem_limit_bytes=VMEM_LIMIT)


def _rms_bf16(x, g):
    ms = jnp.mean(x * x, axis=-1, keepdims=True)
    return (x * lax.rsqrt(ms + EPS) * g).astype(BF16)


def _dot(a, b):
    return jnp.dot(a, b, preferred_element_type=F32)


def _gelu_tanh(x):
    c = 0.7978845608028654
    return x * (0.5 * (1.0 + jnp.tanh(c * (x + 0.044715 * (x * x * x)))))


def _ffn_kernel(h_ref, g_ref, wg_ref, wu_ref, wd_ref, o_ref, xn_ref):
    j = pl.program_id(1)

    @pl.when(j == 0)
    def _():
        x = h_ref[...]
        xn_ref[...] = _rms_bf16(x, g_ref[...])
        o_ref[...] = x

    xn = xn_ref[...]
    gt = _dot(xn, wg_ref[...])
    up = _dot(xn, wu_ref[...])
    hd = (gt * jax.nn.sigmoid(gt)) * up * 0.5
    o_ref[...] += _dot(hd.astype(BF16), wd_ref[...])


def _ffn(h, gain, w_gate, w_up, w_down, layer):
    return pl.pallas_call(
        _ffn_kernel,
        grid=(N_TILE, D_FF // TF),
        in_specs=[
            pl.BlockSpec((TM, D_MODEL), lambda i, j: (i, 0)),
            pl.BlockSpec((None, 1, D_MODEL), lambda i, j: (layer, 0, 0)),
            pl.BlockSpec((None, D_MODEL, TF), lambda i, j: (layer, 0, j)),
            pl.BlockSpec((None, D_MODEL, TF), lambda i, j: (layer, 0, j)),
            pl.BlockSpec((None, TF, D_MODEL), lambda i, j: (layer, j, 0)),
        ],
        out_specs=pl.BlockSpec((TM, D_MODEL), lambda i, j: (i, 0)),
        out_shape=jax.ShapeDtypeStruct((N_TOK, D_MODEL), F32),
        scratch_shapes=[pltpu.VMEM((TM, D_MODEL), BF16)],
        compiler_params=_ARB2,
        name="ffn",
    )(h, gain, w_gate, w_up, w_down)


_GMLP_PROJ_STEPS = N_HEAD
_GMLP_HEADS_PER_MIX = 2
_GMLP_STEPS = _GMLP_PROJ_STEPS + N_HEAD // _GMLP_HEADS_PER_MIX


def _gmlp_kernel(h_ref, g_ref, wu_ref, wv_ref, lng_ref, lnb_ref, ws_ref, bs_ref, wo_ref,
                 o_ref, vo_ref,
                 xn_ref, u_ref, v_ref, sum_ref, mu_ref, rstd_ref, y_ref):
    i = pl.program_id(0)
    j = pl.program_id(1)

    @pl.when(j == 0)
    def _():
        x = h_ref[...]
        xn_ref[...] = _rms_bf16(x, g_ref[...])
        o_ref[...] = x
        sum_ref[...] = jnp.zeros_like(sum_ref)

    @pl.when(j < _GMLP_PROJ_STEPS)
    def _():
        xn = xn_ref[...]
        u_ref[j] = _gelu_tanh(_dot(xn, wu_ref[...]))
        v = _gelu_tanh(_dot(xn, wv_ref[...]))
        v_ref[j] = v
        sum_ref[...] += jnp.sum(v, axis=-1, keepdims=True)

    @pl.when(j == _GMLP_PROJ_STEPS)
    def _():
        mu = sum_ref[...] * (1.0 / E_MIX)
        var = jnp.zeros_like(mu)
        for k in range(N_HEAD):
            d = v_ref[k] - mu
            var = var + jnp.sum(d * d, axis=-1, keepdims=True)
        mu_ref[...] = mu
        rstd_ref[...] = lax.rsqrt(var * (1.0 / E_MIX) + EPS)

    @pl.when(j >= _GMLP_PROJ_STEPS)
    def _():
        is_last_tile = i == N_TILE - 1
        row = lax.broadcasted_iota(jnp.int32, (CHUNK, CHUNK), 0)
        col = lax.broadcasted_iota(jnp.int32, (CHUNK, CHUNK), 1)
        mu = mu_ref[...]
        rstd = rstd_ref[...]
        for hh in range(_GMLP_HEADS_PER_MIX):
            head = (j - _GMLP_PROJ_STEPS) * _GMLP_HEADS_PER_MIX + hh
            lanes = slice(hh * D_HEAD, (hh + 1) * D_HEAD)
            vn = (v_ref[head] - mu) * rstd * lng_ref[head] + lnb_ref[head]
            vo_ref[:, lanes] = vn[SAMPLE_ROW0:, :]
            w = ws_ref[head]
            b = bs_ref[head]
            w_causal = jnp.where(col <= row, w, 0.0)
            w_sample = jnp.where(col == row, w[0:1, 0:1], 0.0)
            b_sample = jnp.broadcast_to(b[0:1, :], (CHUNK, 1))
            w_last = jnp.where(is_last_tile, w_sample, w_causal).astype(BF16)
            b_last = jnp.where(is_last_tile, b_sample, b)
            w_causal = w_causal.astype(BF16)
            n_chunk = TM // CHUNK
            for c in range(n_chunk):
                rows = slice(c * CHUNK, (c + 1) * CHUNK)
                wm, bm = (w_last, b_last) if c == n_chunk - 1 else (w_causal, b)
                s = _dot(wm, vn[rows, :].astype(BF16)) + bm
                y_ref[rows, lanes] = (u_ref[head, rows, :] * s).astype(BF16)
        o_ref[...] += _dot(y_ref[...], wo_ref[...])


def _gmlp(h, gain, w_in, ln_g, ln_b, w_s, b_s, w_out, layer, jm):
    proj_last = _GMLP_PROJ_STEPS - 1
    mix_w = _GMLP_HEADS_PER_MIX * D_HEAD

    def mix_step(j):
        return jnp.maximum(j - _GMLP_PROJ_STEPS, 0)

    return pl.pallas_call(
        _gmlp_kernel,
        grid=(N_TILE, _GMLP_STEPS),
        in_specs=[
            pl.BlockSpec((TM, D_MODEL), lambda i, j: (i, 0)),
            pl.BlockSpec((None, 1, D_MODEL), lambda i, j: (layer, 0, 0)),
            pl.BlockSpec((None, D_MODEL, D_HEAD), lambda i, j: (jm, 0, jnp.minimum(j, proj_last))),
            pl.BlockSpec((None, D_MODEL, D_HEAD),
                         lambda i, j: (jm, 0, N_HEAD + jnp.minimum(j, proj_last))),
            pl.BlockSpec((None, N_HEAD, 1, D_HEAD), lambda i, j: (jm, 0, 0, 0)),
            pl.BlockSpec((None, N_HEAD, 1, D_HEAD), lambda i, j: (jm, 0, 0, 0)),
            pl.BlockSpec((None, N_HEAD, CHUNK, CHUNK), lambda i, j: (jm, 0, 0, 0)),
            pl.BlockSpec((None, N_HEAD, CHUNK, 1), lambda i, j: (jm, 0, 0, 0)),
            pl.BlockSpec((None, mix_w, D_MODEL), lambda i, j: (jm, mix_step(j), 0)),
        ],
        out_specs=[
            pl.BlockSpec((TM, D_MODEL), lambda i, j: (i, 0)),
            pl.BlockSpec((N_SAMPLE, mix_w), lambda i, j: (0, mix_step(j))),
        ],
        out_shape=[
            jax.ShapeDtypeStruct((N_TOK, D_MODEL), F32),
            jax.ShapeDtypeStruct((N_SAMPLE, E_MIX), F32),
        ],
        scratch_shapes=[
            pltpu.VMEM((TM, D_MODEL), BF16),
            pltpu.VMEM((N_HEAD, TM, D_HEAD), F32),
            pltpu.VMEM((N_HEAD, TM, D_HEAD), F32),
            pltpu.VMEM((TM, 1), F32),
            pltpu.VMEM((TM, 1), F32),
            pltpu.VMEM((TM, 1), F32),
            pltpu.VMEM((TM, mix_w), BF16),
        ],
        compiler_params=_ARB2,
        name="gmlp",
    )(h, gain, w_in, w_in, ln_g, ln_b, w_s, b_s, w_out)


_HALO = 8


def _conv_kernel(h_ref, g_ref, wb_ref, wc_ref, wx_ref, wk_ref, st0_ref, st1_ref, wo_ref,
                 o_ref, ci_ref,
                 xn_ref, buf_ref, carry_ref, y_ref):
    i = pl.program_id(0)
    j = pl.program_id(1)

    @pl.when(j == 0)
    def _():
        x = h_ref[...]
        xn_ref[...] = _rms_bf16(x, g_ref[...])
        o_ref[...] = x

    @pl.when(i == 0)
    def _():
        carry_ref[j] = jnp.zeros((_HALO, TN), F32)

    xn = xn_ref[...]
    bg = _dot(xn, wb_ref[...])
    ci = _dot(xn, wc_ref[...]) * _dot(xn, wx_ref[...])
    ci_ref[...] = ci

    buf_ref[0:_HALO, :] = carry_ref[j]
    buf_ref[_HALO:, :] = ci
    carry_ref[j] = ci[TM - _HALO:, :]
    pos = (lax.broadcasted_iota(jnp.int32, (TM, 1), 0) + i * TM) & (SEQ - 1)
    prev1 = jnp.where(pos >= 1, buf_ref[_HALO - 1:_HALO - 1 + TM, :], 0.0)
    prev2 = jnp.where(pos >= 2, buf_ref[_HALO - 2:_HALO - 2 + TM, :], 0.0)
    w0 = wk_ref[0:1, :]
    w1 = wk_ref[1:2, :]
    w2 = wk_ref[2:3, :]
    co = w0 * prev2 + w1 * prev1 + w2 * ci
    y_ref[...] = (bg * co).astype(BF16)

    @pl.when(i == N_TILE - 1)
    def _():
        co_s = w0 * st0_ref[...] + w1 * st1_ref[...] + w2 * ci[SAMPLE_ROW0:, :]
        y_ref[SAMPLE_ROW0:, :] = (bg[SAMPLE_ROW0:, :] * co_s).astype(BF16)

    o_ref[...] += _dot(y_ref[...], wo_ref[...])


def _conv(h, gain, w_in, w_conv, st0, st1, w_out, layer, jm):
    nb = E_MIX // TN
    return pl.pallas_call(
        _conv_kernel,
        grid=(N_TILE, nb),
        in_specs=[
            pl.BlockSpec((TM, D_MODEL), lambda i, j: (i, 0)),
            pl.BlockSpec((None, 1, D_MODEL), lambda i, j: (layer, 0, 0)),
            pl.BlockSpec((None, D_MODEL, TN), lambda i, j: (jm, 0, j)),
            pl.BlockSpec((None, D_MODEL, TN), lambda i, j: (jm, 0, nb + j)),
            pl.BlockSpec((None, D_MODEL, TN), lambda i, j: (jm, 0, 2 * nb + j)),
            pl.BlockSpec((None, 3, TN), lambda i, j: (jm, 0, j)),
            pl.BlockSpec((N_SAMPLE, TN), lambda i, j: (0, j)),
            pl.BlockSpec((N_SAMPLE, TN), lambda i, j: (0, j)),
            pl.BlockSpec((None, TN, D_MODEL), lambda i, j: (jm, j, 0)),
        ],
        out_specs=[
            pl.BlockSpec((TM, D_MODEL), lambda i, j: (i, 0)),
            pl.BlockSpec((TM, TN), lambda i, j: (i, j)),
        ],
        out_shape=[
            jax.ShapeDtypeStruct((N_TOK, D_MODEL), F32),
            jax.ShapeDtypeStruct((N_TOK, E_MIX), F32),
        ],
        scratch_shapes=[
            pltpu.VMEM((TM, D_MODEL), BF16),
            pltpu.VMEM((TM + _HALO, TN), F32),
            pltpu.VMEM((nb, _HALO, TN), F32),
            pltpu.VMEM((TM, TN), BF16),
        ],
        compiler_params=_ARB2,
        name="sconv",
    )(h, gain, w_in, w_in, w_in, w_conv, st0, st1, w_out)


def _ple_kernel(h_ref, g_ref, wg_ref, p_ref, wp_ref, gf_ref, o_ref, *, final_norm):
    xn = _rms_bf16(h_ref[...], g_ref[...])
    pb = p_ref[...].astype(BF16)
    for c in range(D_MODEL // TN):
        cols = slice(c * TN, (c + 1) * TN)
        gate = jax.nn.sigmoid(_dot(xn, wg_ref[:, cols]))
        o_ref[:, cols] = h_ref[:, cols] + gate * _dot(pb, wp_ref[:, cols])
    if final_norm:
        hn = o_ref[...]
        ms = jnp.mean(hn * hn, axis=-1, keepdims=True)
        o_ref[...] = hn * lax.rsqrt(ms + EPS) * gf_ref[...]


def _ple(h, gain, w_gate, p, w_proj, g_final, layer, final_norm):
    return pl.pallas_call(
        functools.partial(_ple_kernel, final_norm=final_norm),
        grid=(N_TILE,),
        in_specs=[
            pl.BlockSpec((TM, D_MODEL), lambda i: (i, 0)),
            pl.BlockSpec((None, 1, D_MODEL), lambda i: (layer, 0, 0)),
            pl.BlockSpec((None, D_MODEL, D_MODEL), lambda i: (layer, 0, 0)),
            pl.BlockSpec((None, TM, D_PLE), lambda i: (layer, i, 0)),
            pl.BlockSpec((None, D_PLE, D_MODEL), lambda i: (layer, 0, 0)),
            pl.BlockSpec((1, D_MODEL), lambda i: (0, 0)),
        ],
        out_specs=pl.BlockSpec((TM, D_MODEL), lambda i: (i, 0)),
        out_shape=jax.ShapeDtypeStruct((N_TOK, D_MODEL), F32),
        compiler_params=_ARB1,
        name="ple",
    )(h, gain, w_gate, p, w_proj, g_final)


def kernel(x_prompt, x_sample, state_conv, p_prompt, p_sample, ffn1_norm, ffn1_w_gate, ffn1_w_up, ffn1_w_down, mix_norm, a_w_in, a_ln_g, a_ln_b, a_w_s, a_b_s, a_w_out, c_w_in, c_w_conv, c_w_out, ffn2_norm, ffn2_w_gate, ffn2_w_up, ffn2_w_down, ple_norm, ple_w_gate, ple_w_proj, final_norm):
    bf = lambda w: w.astype(BF16)
    gain3 = lambda g: g.reshape(g.shape[0], 1, D_MODEL)

    h = jnp.concatenate([x_prompt.reshape(N_PROMPT, D_MODEL),
                         x_sample.reshape(N_SAMPLE, D_MODEL)], axis=0)
    p = jnp.concatenate([p_prompt.reshape(DEPTH, N_PROMPT, D_PLE),
                         p_sample.reshape(DEPTH, N_SAMPLE, D_PLE)], axis=1)

    f1 = (gain3(ffn1_norm), bf(ffn1_w_gate), bf(ffn1_w_up), bf(ffn1_w_down))
    f2 = (gain3(ffn2_norm), bf(ffn2_w_gate), bf(ffn2_w_up), bf(ffn2_w_down))
    mix_gain = gain3(mix_norm)
    ple_gain = gain3(ple_norm)
    a_w_in_b, a_w_out_b = bf(a_w_in), bf(a_w_out)
    c_w_in_b, c_w_out_b = bf(c_w_in), bf(c_w_out)
    ple_w_gate_b, ple_w_proj_b = bf(ple_w_gate), bf(ple_w_proj)
    ln_g = a_ln_g.reshape(-1, N_HEAD, 1, D_HEAD)
    ln_b = a_ln_b.reshape(-1, N_HEAD, 1, D_HEAD)
    b_s = a_b_s.reshape(-1, N_HEAD, CHUNK, 1)
    g_final = final_norm.reshape(1, D_MODEL)

    conv_new_prompt, conv_new_sample, v_new = [], [], []
    for layer in range(DEPTH):
        h = _ffn(h, *f1, layer)
        jm = layer // 2
        if layer % 2 == 0:
            h, v_s = _gmlp(h, mix_gain, a_w_in_b, ln_g, ln_b, a_w_s, b_s, a_w_out_b, layer, jm)
            v_new.append(v_s.reshape(N_SAMPLE, 1, E_MIX))
        else:
            st0 = state_conv[jm, :, 0, :]
            st1 = state_conv[jm, :, 1, :]
            h, ci = _conv(h, mix_gain, c_w_in_b, c_w_conv, st0, st1, c_w_out_b, layer, jm)
            conv_new_prompt.append(ci[:N_PROMPT].reshape(4, SEQ, E_MIX)[:, SEQ - 2:, :])
            conv_new_sample.append(jnp.stack([st1, ci[N_PROMPT:]], axis=1))
        h = _ffn(h, *f2, layer)
        h = _ple(h, ple_gain, ple_w_gate_b, p, ple_w_proj_b, g_final, layer,
                 final_norm=(layer == DEPTH - 1))

    y_prompt = h[:N_PROMPT].reshape(4, SEQ, D_MODEL)
    y_sample = h[N_PROMPT:].reshape(N_SAMPLE, 1, D_MODEL)
    return (y_prompt, y_sample, jnp.stack(conv_new_prompt), jnp.stack(conv_new_sample),
            jnp.stack(v_new))
```

```python
import functools

import jax
import jax.numpy as jnp
from jax import lax
from jax.experimental import pallas as pl
from jax.experimental.pallas import tpu as pltpu

F32 = jnp.float32
BF16 = jnp.bfloat16

D_MODEL = 2048
D_FF = 5632
E_MIX = 2048
N_HEAD = 8
D_HEAD = 256
CHUNK = 128
D_PLE = 256
SEQ = 2048
N_PROMPT = 4 * SEQ
N_SAMPLE = 128
N_TOK = N_PROMPT + N_SAMPLE
DEPTH = 2
EPS = 1e-6

TM = 640
N_TILE = N_TOK // TM
SAMPLE_ROW0 = TM - N_SAMPLE
TM_FFN = 832
N_TILE_FFN = N_TOK // TM_FFN
SAMPLE_ROW0_FFN = TM_FFN - N_SAMPLE
TF = 256
TN = 512
VMEM_LIMIT = 56 * 1024 * 1024

_ARB2 = pltpu.CompilerParams(dimension_semantics=("arbitrary", "arbitrary"),
                             vmem_limit_bytes=VMEM_LIMIT)
_ARB1 = pltpu.CompilerParams(dimension_semantics=("arbitrary",),
                             vmem_limit_bytes=VMEM_LIMIT)


def _rms_bf16(x, g):
    ms = jnp.mean(x * x, axis=-1, keepdims=True)
    return (x * lax.rsqrt(ms + EPS) * g).astype(BF16)


def _dot(a, b):
    return jnp.dot(a, b, preferred_element_type=F32)


def _gelu_tanh(x):
    c = 0.7978845608028654
    return x * (0.5 * (1.0 + jnp.tanh(c * (x + 0.044715 * (x * x * x)))))


def _ffn_start(x, g_ref, o_ref, xn_ref):
    o_ref[...] = x
    xn_ref[...] = _rms_bf16(x, g_ref[...])


def _ffn_step(wg_ref, wu_ref, wd_ref, o_ref, xn_ref):
    xn = xn_ref[...]
    gt = _dot(xn, wg_ref[...].astype(BF16))
    up = _dot(xn, wu_ref[...].astype(BF16))
    hd = (gt * jax.nn.sigmoid(gt)) * up * 0.5
    o_ref[...] += _dot(hd.astype(BF16), wd_ref[...].astype(BF16))


def _ffn_kernel(h_ref, g_ref, wg_ref, wu_ref, wd_ref, o_ref, xn_ref):
    @pl.when(pl.program_id(1) == 0)
    def _():
        _ffn_start(h_ref[...], g_ref, o_ref, xn_ref)

    _ffn_step(wg_ref, wu_ref, wd_ref, o_ref, xn_ref)


def _ffn_split_kernel(xp_ref, xs_ref, g_ref, wg_ref, wu_ref, wd_ref, o_ref, xn_ref):
    i = pl.program_id(0)
    first_step = pl.program_id(1) == 0

    @pl.when(first_step & (i < N_TILE_FFN - 1))
    def _():
        _ffn_start(xp_ref[...], g_ref, o_ref, xn_ref)

    @pl.when(first_step & (i == N_TILE_FFN - 1))
    def _():
        x = jnp.concatenate([xp_ref[0:SAMPLE_ROW0_FFN, :], xs_ref[...]], axis=0)
        _ffn_start(x, g_ref, o_ref, xn_ref)

    _ffn_step(wg_ref, wu_ref, wd_ref, o_ref, xn_ref)


def _ffn(h, gain, w_gate, w_up, w_down, layer):
    split = isinstance(h, tuple)
    row_specs = [pl.BlockSpec((TM_FFN, D_MODEL), lambda i, j: (i, 0))]
    if split:
        row_specs.append(pl.BlockSpec((N_SAMPLE, D_MODEL), lambda i, j: (0, 0)))
    return pl.pallas_call(
        _ffn_split_kernel if split else _ffn_kernel,
        grid=(N_TILE_FFN, D_FF // TF),
        in_specs=row_specs + [
            pl.BlockSpec((None, 1, D_MODEL), lambda i, j: (layer, 0, 0)),
            pl.BlockSpec((None, D_MODEL, TF), lambda i, j: (layer, 0, j)),
            pl.BlockSpec((None, D_MODEL, TF), lambda i, j: (layer, 0, j)),
            pl.BlockSpec((None, TF, D_MODEL), lambda i, j: (layer, j, 0)),
        ],
        out_specs=pl.BlockSpec((TM_FFN, D_MODEL), lambda i, j: (i, 0)),
        out_shape=jax.ShapeDtypeStruct((N_TOK, D_MODEL), F32),
        scratch_shapes=[pltpu.VMEM((TM_FFN, D_MODEL), BF16)],
        compiler_params=_ARB2,
        name="ffn",
    )(*(h if split else (h,)), gain, w_gate, w_up, w_down)


_GMLP_PROJ_STEPS = N_HEAD
_GMLP_HEADS_PER_MIX = 2
_GMLP_STEPS = _GMLP_PROJ_STEPS + N_HEAD // _GMLP_HEADS_PER_MIX


def _gmlp_kernel(h_ref, g_ref, wu_ref, wv_ref, lng_ref, lnb_ref, ws_ref, bs_ref, wo_ref,
                 o_ref, vo_ref,
                 xn_ref, u_ref, v_ref, sum_ref, mu_ref, rstd_ref, y_ref):
    i = pl.program_id(0)
    j = pl.program_id(1)

    @pl.when(j == 0)
    def _():
        x = h_ref[...]
        xn_ref[...] = _rms_bf16(x, g_ref[...])
        o_ref[...] = x
        sum_ref[...] = jnp.zeros_like(sum_ref)

    @pl.when(j < _GMLP_PROJ_STEPS)
    def _():
        xn = xn_ref[...]
        u_ref[j] = _gelu_tanh(_dot(xn, wu_ref[...]))
        v = _gelu_tanh(_dot(xn, wv_ref[...]))
        v_ref[j] = v
        sum_ref[...] += jnp.sum(v, axis=-1, keepdims=True)

    @pl.when(j == _GMLP_PROJ_STEPS)
    def _():
        mu = sum_ref[...] * (1.0 / E_MIX)
        var = jnp.zeros_like(mu)
        for k in range(N_HEAD):
            d = v_ref[k] - mu
            var = var + jnp.sum(d * d, axis=-1, keepdims=True)
        mu_ref[...] = mu
        rstd_ref[...] = lax.rsqrt(var * (1.0 / E_MIX) + EPS)

    @pl.when(j >= _GMLP_PROJ_STEPS)
    def _():
        is_last_tile = i == N_TILE - 1
        row = lax.broadcasted_iota(jnp.int32, (CHUNK, CHUNK), 0)
        col = lax.broadcasted_iota(jnp.int32, (CHUNK, CHUNK), 1)
        mu = mu_ref[...]
        rstd = rstd_ref[...]
        for hh in range(_GMLP_HEADS_PER_MIX):
            head = (j - _GMLP_PROJ_STEPS) * _GMLP_HEADS_PER_MIX + hh
            lanes = slice(hh * D_HEAD, (hh + 1) * D_HEAD)
            vn = (v_ref[head] - mu) * rstd * lng_ref[head] + lnb_ref[head]

            @pl.when(is_last_tile)
            def _():
                vo_ref[head] = vn[SAMPLE_ROW0:, :]

            w = ws_ref[head]
            b = bs_ref[head]
            w_causal = jnp.where(col <= row, w, 0.0)
            w_sample = jnp.where(col == row, w[0:1, 0:1], 0.0)
            b_sample = jnp.broadcast_to(b[0:1, :], (CHUNK, 1))
            w_last = jnp.where(is_last_tile, w_sample, w_causal).astype(BF16)
            b_last = jnp.where(is_last_tile, b_sample, b)
            w_causal = w_causal.astype(BF16)
            n_chunk = TM // CHUNK
            for c in range(n_chunk):
                rows = slice(c * CHUNK, (c + 1) * CHUNK)
                wm, bm = (w_last, b_last) if c == n_chunk - 1 else (w_causal, b)
                s = _dot(wm, vn[rows, :].astype(BF16)) + bm
                y_ref[rows, lanes] = (u_ref[head, rows, :] * s).astype(BF16)
        o_ref[...] += _dot(y_ref[...], wo_ref[...])


def _gmlp(h, gain, w_in, ln_g, ln_b, w_s, b_s, w_out, layer, jm):
    proj_last = _GMLP_PROJ_STEPS - 1
    mix_w = _GMLP_HEADS_PER_MIX * D_HEAD

    def mix_step(j):
        return jnp.maximum(j - _GMLP_PROJ_STEPS, 0)

    return pl.pallas_call(
        _gmlp_kernel,
        grid=(N_TILE, _GMLP_STEPS),
        in_specs=[
            pl.BlockSpec((TM, D_MODEL), lambda i, j: (i, 0)),
            pl.BlockSpec((None, 1, D_MODEL), lambda i, j: (layer, 0, 0)),
            pl.BlockSpec((None, D_MODEL, D_HEAD), lambda i, j: (jm, 0, jnp.minimum(j, proj_last))),
            pl.BlockSpec((None, D_MODEL, D_HEAD),
                         lambda i, j: (jm, 0, N_HEAD + jnp.minimum(j, proj_last))),
            pl.BlockSpec((None, N_HEAD, 1, D_HEAD), lambda i, j: (jm, 0, 0, 0)),
            pl.BlockSpec((None, N_HEAD, 1, D_HEAD), lambda i, j: (jm, 0, 0, 0)),
            pl.BlockSpec((None, N_HEAD, CHUNK, CHUNK), lambda i, j: (jm, 0, 0, 0)),
            pl.BlockSpec((None, N_HEAD, CHUNK, 1), lambda i, j: (jm, 0, 0, 0)),
            pl.BlockSpec((None, mix_w, D_MODEL), lambda i, j: (jm, mix_step(j), 0)),
        ],
        out_specs=[
            pl.BlockSpec((TM, D_MODEL), lambda i, j: (i, 0)),
            pl.BlockSpec((N_HEAD, N_SAMPLE, D_HEAD), lambda i, j: (0, 0, 0)),
        ],
        out_shape=[
            jax.ShapeDtypeStruct((N_TOK, D_MODEL), F32),
            jax.ShapeDtypeStruct((N_HEAD, N_SAMPLE, D_HEAD), F32),
        ],
        scratch_shapes=[
            pltpu.VMEM((TM, D_MODEL), BF16),
            pltpu.VMEM((N_HEAD, TM, D_HEAD), F32),
            pltpu.VMEM((N_HEAD, TM, D_HEAD), F32),
            pltpu.VMEM((TM, 1), F32),
            pltpu.VMEM((TM, 1), F32),
            pltpu.VMEM((TM, 1), F32),
            pltpu.VMEM((TM, mix_w), BF16),
        ],
        compiler_params=_ARB2,
        name="gmlp",
    )(h, gain, w_in, w_in, ln_g, ln_b, w_s, b_s, w_out)


_HALO = 8


_SEQ_TAIL = [divmod((s + 1) * SEQ - _HALO, TM) for s in range(N_PROMPT // SEQ)]


def _conv_kernel(h_ref, g_ref, wb_ref, wc_ref, wx_ref, wk_ref, st0_ref, st1_ref, wo_ref,
                 o_ref, cs_ref, cp_ref,
                 xn_ref, buf_ref, carry_ref, y_ref):
    i = pl.program_id(0)
    j = pl.program_id(1)

    @pl.when(j == 0)
    def _():
        x = h_ref[...]
        xn_ref[...] = _rms_bf16(x, g_ref[...])
        o_ref[...] = x

    @pl.when(i == 0)
    def _():
        carry_ref[j] = jnp.zeros((_HALO, TN), F32)

    xn = xn_ref[...]
    bg = _dot(xn, wb_ref[...])
    ci = _dot(xn, wc_ref[...]) * _dot(xn, wx_ref[...])

    for s, (tile, row0) in enumerate(_SEQ_TAIL):
        @pl.when(i == tile)
        def _():
            cp_ref[s, j] = ci[row0:row0 + _HALO, :]

    buf_ref[0:_HALO, :] = carry_ref[j]
    buf_ref[_HALO:, :] = ci
    carry_ref[j] = ci[TM - _HALO:, :]
    pos = (lax.broadcasted_iota(jnp.int32, (TM, 1), 0) + i * TM) & (SEQ - 1)
    prev1 = jnp.where(pos >= 1, buf_ref[_HALO - 1:_HALO - 1 + TM, :], 0.0)
    prev2 = jnp.where(pos >= 2, buf_ref[_HALO - 2:_HALO - 2 + TM, :], 0.0)
    w0 = wk_ref[0:1, :]
    w1 = wk_ref[1:2, :]
    w2 = wk_ref[2:3, :]
    co = w0 * prev2 + w1 * prev1 + w2 * ci
    y_ref[...] = (bg * co).astype(BF16)

    @pl.when(i == N_TILE - 1)
    def _():
        co_s = w0 * st0_ref[...] + w1 * st1_ref[...] + w2 * ci[SAMPLE_ROW0:, :]
        y_ref[SAMPLE_ROW0:, :] = (bg[SAMPLE_ROW0:, :] * co_s).astype(BF16)
        cs_ref[j] = ci[SAMPLE_ROW0:, :]

    o_ref[...] += _dot(y_ref[...], wo_ref[...])


def _conv(h, gain, w_in, w_conv, st0, st1, w_out, layer, jm):
    nb = E_MIX // TN
    n_seq = N_PROMPT // SEQ
    return pl.pallas_call(
        _conv_kernel,
        grid=(N_TILE, nb),
        in_specs=[
            pl.BlockSpec((TM, D_MODEL), lambda i, j: (i, 0)),
            pl.BlockSpec((None, 1, D_MODEL), lambda i, j: (layer, 0, 0)),
            pl.BlockSpec((None, D_MODEL, TN), lambda i, j: (jm, 0, j)),
            pl.BlockSpec((None, D_MODEL, TN), lambda i, j: (jm, 0, nb + j)),
            pl.BlockSpec((None, D_MODEL, TN), lambda i, j: (jm, 0, 2 * nb + j)),
            pl.BlockSpec((None, 3, TN), lambda i, j: (jm, 0, j)),
            pl.BlockSpec((N_SAMPLE, TN), lambda i, j: (0, j)),
            pl.BlockSpec((N_SAMPLE, TN), lambda i, j: (0, j)),
            pl.BlockSpec((None, TN, D_MODEL), lambda i, j: (jm, j, 0)),
        ],
        out_specs=[
            pl.BlockSpec((TM, D_MODEL), lambda i, j: (i, 0)),
            pl.BlockSpec((nb, N_SAMPLE, TN), lambda i, j: (0, 0, 0)),
            pl.BlockSpec((n_seq, nb, _HALO, TN), lambda i, j: (0, 0, 0, 0)),
        ],
        out_shape=[
            jax.ShapeDtypeStruct((N_TOK, D_MODEL), F32),
            jax.ShapeDtypeStruct((nb, N_SAMPLE, TN), F32),
            jax.ShapeDtypeStruct((n_seq, nb, _HALO, TN), F32),
        ],
        scratch_shapes=[
            pltpu.VMEM((TM, D_MODEL), BF16),
            pltpu.VMEM((TM + _HALO, TN), F32),
            pltpu.VMEM((nb, _HALO, TN), F32),
            pltpu.VMEM((TM, TN), BF16),
        ],
        compiler_params=_ARB2,
        name="sconv",
    )(h, gain, w_in, w_in, w_in, w_conv, st0, st1, w_out)


def _ple_kernel(h_ref, g_ref, wg_ref, pp_ref, ps_ref, wp_ref, gf_ref, o_ref, *rest, final_norm):
    if final_norm:
        os_ref, pb_ref = rest
    else:
        (pb_ref,) = rest
    i = pl.program_id(0)

    @pl.when(i < N_TILE - 1)
    def _():
        pb_ref[...] = pp_ref[...].astype(BF16)

    @pl.when(i == N_TILE - 1)
    def _():
        pb_ref[0:SAMPLE_ROW0, :] = pp_ref[0:SAMPLE_ROW0, :].astype(BF16)
        pb_ref[SAMPLE_ROW0:, :] = ps_ref[...].astype(BF16)

    xn = _rms_bf16(h_ref[...], g_ref[...])
    pb = pb_ref[...]
    for c in range(D_MODEL // TN):
        cols = slice(c * TN, (c + 1) * TN)
        gate = jax.nn.sigmoid(_dot(xn, wg_ref[:, cols]))
        o_ref[:, cols] = h_ref[:, cols] + gate * _dot(pb, wp_ref[:, cols])
    if final_norm:
        hn = o_ref[...]
        ms = jnp.mean(hn * hn, axis=-1, keepdims=True)
        o_ref[...] = hn * lax.rsqrt(ms + EPS) * gf_ref[...]

        @pl.when(i == N_TILE - 1)
        def _():
            os_ref[...] = o_ref[SAMPLE_ROW0:, :]


def _ple(h, gain, w_gate, p_prompt, p_sample, w_proj, g_final, layer, final_norm):
    row_block = pl.BlockSpec((TM, D_MODEL), lambda i: (i, 0))
    if final_norm:
        out_specs = [row_block, pl.BlockSpec((N_SAMPLE, D_MODEL), lambda i: (0, 0))]
        out_shape = [jax.ShapeDtypeStruct((N_PROMPT, D_MODEL), F32),
                     jax.ShapeDtypeStruct((N_SAMPLE, D_MODEL), F32)]
    else:
        out_specs = row_block
        out_shape = jax.ShapeDtypeStruct((N_TOK, D_MODEL), F32)
    return pl.pallas_call(
        functools.partial(_ple_kernel, final_norm=final_norm),
        grid=(N_TILE,),
        in_specs=[
            row_block,
            pl.BlockSpec((None, 1, D_MODEL), lambda i: (layer, 0, 0)),
            pl.BlockSpec((None, D_MODEL, D_MODEL), lambda i: (layer, 0, 0)),
            pl.BlockSpec((None, TM, D_PLE), lambda i: (layer, i, 0)),
            pl.BlockSpec((None, N_SAMPLE, D_PLE), lambda i: (layer, 0, 0)),
            pl.BlockSpec((None, D_PLE, D_MODEL), lambda i: (layer, 0, 0)),
            pl.BlockSpec((1, D_MODEL), lambda i: (0, 0)),
        ],
        out_specs=out_specs,
        out_shape=out_shape,
        scratch_shapes=[pltpu.VMEM((TM, D_PLE), BF16)],
        compiler_params=_ARB1,
        name="ple",
    )(h, gain, w_gate, p_prompt, p_sample, w_proj, g_final)


def kernel(x_prompt, x_sample, state_conv, p_prompt, p_sample, ffn1_norm, ffn1_w_gate, ffn1_w_up, ffn1_w_down, mix_norm, a_w_in, a_ln_g, a_ln_b, a_w_s, a_b_s, a_w_out, c_w_in, c_w_conv, c_w_out, ffn2_norm, ffn2_w_gate, ffn2_w_up, ffn2_w_down, ple_norm, ple_w_gate, ple_w_proj, final_norm):
    bf = lambda w: w.astype(BF16)
    gain3 = lambda g: g.reshape(g.shape[0], 1, D_MODEL)

    h = (x_prompt.reshape(N_PROMPT, D_MODEL), x_sample.reshape(N_SAMPLE, D_MODEL))
    p_prompt = p_prompt.reshape(DEPTH, N_PROMPT, D_PLE)
    p_sample = p_sample.reshape(DEPTH, N_SAMPLE, D_PLE)

    f1 = (gain3(ffn1_norm), ffn1_w_gate, ffn1_w_up, ffn1_w_down)
    f2 = (gain3(ffn2_norm), ffn2_w_gate, ffn2_w_up, ffn2_w_down)
    mix_gain = gain3(mix_norm)
    ple_gain = gain3(ple_norm)
    a_w_in_b, a_w_out_b = bf(a_w_in), bf(a_w_out)
    c_w_in_b, c_w_out_b = bf(c_w_in), bf(c_w_out)
    ple_w_gate_b, ple_w_proj_b = bf(ple_w_gate), bf(ple_w_proj)
    ln_g = a_ln_g.reshape(-1, N_HEAD, 1, D_HEAD)
    ln_b = a_ln_b.reshape(-1, N_HEAD, 1, D_HEAD)
    b_s = a_b_s.reshape(-1, N_HEAD, CHUNK, 1)
    g_final = final_norm.reshape(1, D_MODEL)

    conv_new_prompt, conv_new_sample, v_new = [], [], []
    for layer in range(DEPTH):
        h = _ffn(h, *f1, layer)
        jm = layer // 2
        if layer % 2 == 0:
            h, v_s = _gmlp(h, mix_gain, a_w_in_b, ln_g, ln_b, a_w_s, b_s, a_w_out_b, layer, jm)
            v_new.append(v_s.transpose(1, 0, 2).reshape(N_SAMPLE, 1, E_MIX))
        else:
            st0 = state_conv[jm, :, 0, :]
            st1 = state_conv[jm, :, 1, :]
            h, ci_s, ci_p = _conv(h, mix_gain, c_w_in_b, c_w_conv, st0, st1, c_w_out_b, layer, jm)
            ci_s = ci_s.transpose(1, 0, 2).reshape(N_SAMPLE, E_MIX)
            ci_p = ci_p.transpose(0, 2, 1, 3).reshape(N_PROMPT // SEQ, _HALO, E_MIX)
            conv_new_prompt.append(ci_p[:, _HALO - 2:, :])
            conv_new_sample.append(jnp.stack([st1, ci_s], axis=1))
        h = _ffn(h, *f2, layer)
        h = _ple(h, ple_gain, ple_w_gate_b, p_prompt, p_sample, ple_w_proj_b, g_final, layer,
                 final_norm=(layer == DEPTH - 1))

    y_prompt, y_sample = h
    return (y_prompt.reshape(N_PROMPT // SEQ, SEQ, D_MODEL), y_sample.reshape(N_SAMPLE, 1, D_MODEL),
            jnp.stack(conv_new_prompt), jnp.stack(conv_new_sample), jnp.stack(v_new))
```

```python
import functools

import jax
import jax.numpy as jnp
from jax import lax
from jax.experimental import pallas as pl
from jax.experimental.pallas import tpu as pltpu

F32 = jnp.float32
BF16 = jnp.bfloat16

D_MODEL = 2048
D_FF = 5632
E_MIX = 2048
N_HEAD = 8
D_HEAD = 256
CHUNK = 128
D_PLE = 256
SEQ = 2048
N_PROMPT = 4 * SEQ
N_SAMPLE = 128
N_TOK = N_PROMPT + N_SAMPLE
DEPTH = 2
EPS = 1e-6

TM = 640
N_TILE = N_TOK // TM
SAMPLE_ROW0 = TM - N_SAMPLE
TM_FFN = 1040
N_TILE_FFN = N_TOK // TM_FFN
SAMPLE_ROW0_FFN = TM_FFN - N_SAMPLE
TF = 256
TN = 512
VMEM_LIMIT = 62 * 1024 * 1024

_ARB2 = pltpu.CompilerParams(dimension_semantics=("arbitrary", "arbitrary"),
                             vmem_limit_bytes=VMEM_LIMIT)
_ARB1 = pltpu.CompilerParams(dimension_semantics=("arbitrary",),
                             vmem_limit_bytes=VMEM_LIMIT)


def _rms_bf16(x, g):
    ms = jnp.mean(x * x, axis=-1, keepdims=True)
    return (x * lax.rsqrt(ms + EPS) * g).astype(BF16)


def _dot(a, b):
    return jnp.dot(a, b, preferred_element_type=F32)


def _gelu_tanh(x):
    c = 0.7978845608028654
    return x * (0.5 * (1.0 + jnp.tanh(c * (x + 0.044715 * (x * x * x)))))


def _ffn_start(x, g_ref, o_ref, xn_ref):
    o_ref[...] = x
    xn_ref[...] = _rms_bf16(x, g_ref[...])


def _ffn_step(wg_ref, wu_ref, wd_ref, o_ref, xn_ref):
    xn = xn_ref[...]
    gt = _dot(xn, wg_ref[...].astype(BF16))
    up = _dot(xn, wu_ref[...].astype(BF16))
    hd = (gt * jax.nn.sigmoid(gt)) * up * 0.5
    o_ref[...] += _dot(hd.astype(BF16), wd_ref[...].astype(BF16))


def _ffn_kernel(h_ref, g_ref, wg_ref, wu_ref, wd_ref, o_ref, xn_ref):
    @pl.when(pl.program_id(1) == 0)
    def _():
        _ffn_start(h_ref[...], g_ref, o_ref, xn_ref)

    _ffn_step(wg_ref, wu_ref, wd_ref, o_ref, xn_ref)


def _ffn_split_kernel(xp_ref, xs_ref, g_ref, wg_ref, wu_ref, wd_ref, o_ref, xn_ref):
    i = pl.program_id(0)
    first_step = pl.program_id(1) == 0

    @pl.when(first_step & (i < N_TILE_FFN - 1))
    def _():
        _ffn_start(xp_ref[...], g_ref, o_ref, xn_ref)

    @pl.when(first_step & (i == N_TILE_FFN - 1))
    def _():
        x = jnp.concatenate([xp_ref[0:SAMPLE_ROW0_FFN, :], xs_ref[...]], axis=0)
        _ffn_start(x, g_ref, o_ref, xn_ref)

    _ffn_step(wg_ref, wu_ref, wd_ref, o_ref, xn_ref)


def _ffn(h, gain, w_gate, w_up, w_down, layer):
    split = isinstance(h, tuple)
    row_specs = [pl.BlockSpec((TM_FFN, D_MODEL), lambda i, j: (i, 0))]
    if split:
        row_specs.append(pl.BlockSpec((N_SAMPLE, D_MODEL), lambda i, j: (0, 0)))
    return pl.pallas_call(
        _ffn_split_kernel if split else _ffn_kernel,
        grid=(N_TILE_FFN, D_FF // TF),
        in_specs=row_specs + [
            pl.BlockSpec((None, 1, D_MODEL), lambda i, j: (layer, 0, 0)),
            pl.BlockSpec((None, D_MODEL, TF), lambda i, j: (layer, 0, j)),
            pl.BlockSpec((None, D_MODEL, TF), lambda i, j: (layer, 0, j)),
            pl.BlockSpec((None, TF, D_MODEL), lambda i, j: (layer, j, 0)),
        ],
        out_specs=pl.BlockSpec((TM_FFN, D_MODEL), lambda i, j: (i, 0)),
        out_shape=jax.ShapeDtypeStruct((N_TOK, D_MODEL), F32),
        scratch_shapes=[pltpu.VMEM((TM_FFN, D_MODEL), BF16)],
        compiler_params=_ARB2,
        name="ffn",
    )(*(h if split else (h,)), gain, w_gate, w_up, w_down)


_GMLP_PROJ_STEPS = N_HEAD
_GMLP_HEADS_PER_MIX = 2
_GMLP_STEPS = _GMLP_PROJ_STEPS + N_HEAD // _GMLP_HEADS_PER_MIX


def _gmlp_kernel(h_ref, g_ref, wu_ref, wv_ref, lng_ref, lnb_ref, ws_ref, bs_ref, wo_ref,
                 o_ref, vo_ref,
                 xn_ref, u_ref, v_ref, sum_ref, mu_ref, rstd_ref, y_ref):
    i = pl.program_id(0)
    j = pl.program_id(1)

    @pl.when(j == 0)
    def _():
        x = h_ref[...]
        xn_ref[...] = _rms_bf16(x, g_ref[...])
        o_ref[...] = x
        sum_ref[...] = jnp.zeros_like(sum_ref)

    @pl.when(j < _GMLP_PROJ_STEPS)
    def _():
        xn = xn_ref[...]
        u_ref[j] = _gelu_tanh(_dot(xn, wu_ref[...]))
        v = _gelu_tanh(_dot(xn, wv_ref[...]))
        v_ref[j] = v
        sum_ref[...] += jnp.sum(v, axis=-1, keepdims=True)

    @pl.when(j == _GMLP_PROJ_STEPS)
    def _():
        mu = sum_ref[...] * (1.0 / E_MIX)
        var = jnp.zeros_like(mu)
        for k in range(N_HEAD):
            d = v_ref[k] - mu
            var = var + jnp.sum(d * d, axis=-1, keepdims=True)
        mu_ref[...] = mu
        rstd_ref[...] = lax.rsqrt(var * (1.0 / E_MIX) + EPS)

    @pl.when(j >= _GMLP_PROJ_STEPS)
    def _():
        is_last_tile = i == N_TILE - 1
        row = lax.broadcasted_iota(jnp.int32, (CHUNK, CHUNK), 0)
        col = lax.broadcasted_iota(jnp.int32, (CHUNK, CHUNK), 1)
        mu = mu_ref[...]
        rstd = rstd_ref[...]
        for hh in range(_GMLP_HEADS_PER_MIX):
            head = (j - _GMLP_PROJ_STEPS) * _GMLP_HEADS_PER_MIX + hh
            lanes = slice(hh * D_HEAD, (hh + 1) * D_HEAD)
            vn = (v_ref[head] - mu) * rstd * lng_ref[head] + lnb_ref[head]

            @pl.when(is_last_tile)
            def _():
                vo_ref[head] = vn[SAMPLE_ROW0:, :]

            w = ws_ref[head]
            b = bs_ref[head]
            w_causal = jnp.where(col <= row, w, 0.0)
            w_sample = jnp.where(col == row, w[0:1, 0:1], 0.0)
            b_sample = jnp.broadcast_to(b[0:1, :], (CHUNK, 1))
            w_last = jnp.where(is_last_tile, w_sample, w_causal).astype(BF16)
            b_last = jnp.where(is_last_tile, b_sample, b)
            w_causal = w_causal.astype(BF16)
            n_chunk = TM // CHUNK
            for c in range(n_chunk):
                rows = slice(c * CHUNK, (c + 1) * CHUNK)
                wm, bm = (w_last, b_last) if c == n_chunk - 1 else (w_causal, b)
                s = _dot(wm, vn[rows, :].astype(BF16)) + bm
                y_ref[rows, lanes] = (u_ref[head, rows, :] * s).astype(BF16)
        o_ref[...] += _dot(y_ref[...], wo_ref[...])


def _gmlp(h, gain, w_in, ln_g, ln_b, w_s, b_s, w_out, layer, jm):
    proj_last = _GMLP_PROJ_STEPS - 1
    mix_w = _GMLP_HEADS_PER_MIX * D_HEAD

    def mix_step(j):
        return jnp.maximum(j - _GMLP_PROJ_STEPS, 0)

    return pl.pallas_call(
        _gmlp_kernel,
        grid=(N_TILE, _GMLP_STEPS),
        in_specs=[
            pl.BlockSpec((TM, D_MODEL), lambda i, j: (i, 0)),
            pl.BlockSpec((None, 1, D_MODEL), lambda i, j: (layer, 0, 0)),
            pl.BlockSpec((None, D_MODEL, D_HEAD), lambda i, j: (jm, 0, jnp.minimum(j, proj_last))),
            pl.BlockSpec((None, D_MODEL, D_HEAD),
                         lambda i, j: (jm, 0, N_HEAD + jnp.minimum(j, proj_last))),
            pl.BlockSpec((None, N_HEAD, 1, D_HEAD), lambda i, j: (jm, 0, 0, 0)),
            pl.BlockSpec((None, N_HEAD, 1, D_HEAD), lambda i, j: (jm, 0, 0, 0)),
            pl.BlockSpec((None, N_HEAD, CHUNK, CHUNK), lambda i, j: (jm, 0, 0, 0)),
            pl.BlockSpec((None, N_HEAD, CHUNK, 1), lambda i, j: (jm, 0, 0, 0)),
            pl.BlockSpec((None, mix_w, D_MODEL), lambda i, j: (jm, mix_step(j), 0)),
        ],
        out_specs=[
            pl.BlockSpec((TM, D_MODEL), lambda i, j: (i, 0)),
            pl.BlockSpec((N_HEAD, N_SAMPLE, D_HEAD), lambda i, j: (0, 0, 0)),
        ],
        out_shape=[
            jax.ShapeDtypeStruct((N_TOK, D_MODEL), F32),
            jax.ShapeDtypeStruct((N_HEAD, N_SAMPLE, D_HEAD), F32),
        ],
        scratch_shapes=[
            pltpu.VMEM((TM, D_MODEL), BF16),
            pltpu.VMEM((N_HEAD, TM, D_HEAD), F32),
            pltpu.VMEM((N_HEAD, TM, D_HEAD), F32),
            pltpu.VMEM((TM, 1), F32),
            pltpu.VMEM((TM, 1), F32),
            pltpu.VMEM((TM, 1), F32),
            pltpu.VMEM((TM, mix_w), BF16),
        ],
        compiler_params=_ARB2,
        name="gmlp",
    )(h, gain, w_in, w_in, ln_g, ln_b, w_s, b_s, w_out)


_HALO = 8


_SEQ_TAIL = [divmod((s + 1) * SEQ - _HALO, TM) for s in range(N_PROMPT // SEQ)]


def _conv_kernel(h_ref, g_ref, wb_ref, wc_ref, wx_ref, wk_ref, st0_ref, st1_ref, wo_ref,
                 o_ref, cs_ref, cp_ref,
                 xn_ref, buf_ref, carry_ref, y_ref):
    i = pl.program_id(0)
    j = pl.program_id(1)

    @pl.when(j == 0)
    def _():
        x = h_ref[...]
        xn_ref[...] = _rms_bf16(x, g_ref[...])
        o_ref[...] = x

    @pl.when(i == 0)
    def _():
        carry_ref[j] = jnp.zeros((_HALO, TN), F32)

    xn = xn_ref[...]
    bg = _dot(xn, wb_ref[...])
    ci = _dot(xn, wc_ref[...]) * _dot(xn, wx_ref[...])

    for s, (tile, row0) in enumerate(_SEQ_TAIL):
        @pl.when(i == tile)
        def _():
            cp_ref[s, j] = ci[row0:row0 + _HALO, :]

    buf_ref[0:_HALO, :] = carry_ref[j]
    buf_ref[_HALO:, :] = ci
    carry_ref[j] = ci[TM - _HALO:, :]
    pos = (lax.broadcasted_iota(jnp.int32, (TM, 1), 0) + i * TM) & (SEQ - 1)
    prev1 = jnp.where(pos >= 1, buf_ref[_HALO - 1:_HALO - 1 + TM, :], 0.0)
    prev2 = jnp.where(pos >= 2, buf_ref[_HALO - 2:_HALO - 2 + TM, :], 0.0)
    w0 = wk_ref[0:1, :]
    w1 = wk_ref[1:2, :]
    w2 = wk_ref[2:3, :]
    co = w0 * prev2 + w1 * prev1 + w2 * ci
    y_ref[...] = (bg * co).astype(BF16)

    @pl.when(i == N_TILE - 1)
    def _():
        co_s = w0 * st0_ref[...] + w1 * st1_ref[...] + w2 * ci[SAMPLE_ROW0:, :]
        y_ref[SAMPLE_ROW0:, :] = (bg[SAMPLE_ROW0:, :] * co_s).astype(BF16)
        cs_ref[j] = ci[SAMPLE_ROW0:, :]

    o_ref[...] += _dot(y_ref[...], wo_ref[...])


def _conv(h, gain, w_in, w_conv, st0, st1, w_out, layer, jm):
    nb = E_MIX // TN
    n_seq = N_PROMPT // SEQ
    return pl.pallas_call(
        _conv_kernel,
        grid=(N_TILE, nb),
        in_specs=[
            pl.BlockSpec((TM, D_MODEL), lambda i, j: (i, 0)),
            pl.BlockSpec((None, 1, D_MODEL), lambda i, j: (layer, 0, 0)),
            pl.BlockSpec((None, D_MODEL, TN), lambda i, j: (jm, 0, j)),
            pl.BlockSpec((None, D_MODEL, TN), lambda i, j: (jm, 0, nb + j)),
            pl.BlockSpec((None, D_MODEL, TN), lambda i, j: (jm, 0, 2 * nb + j)),
            pl.BlockSpec((None, 3, TN), lambda i, j: (jm, 0, j)),
            pl.BlockSpec((N_SAMPLE, TN), lambda i, j: (0, j)),
            pl.BlockSpec((N_SAMPLE, TN), lambda i, j: (0, j)),
            pl.BlockSpec((None, TN, D_MODEL), lambda i, j: (jm, j, 0)),
        ],
        out_specs=[
            pl.BlockSpec((TM, D_MODEL), lambda i, j: (i, 0)),
            pl.BlockSpec((nb, N_SAMPLE, TN), lambda i, j: (0, 0, 0)),
            pl.BlockSpec((n_seq, nb, _HALO, TN), lambda i, j: (0, 0, 0, 0)),
        ],
        out_shape=[
            jax.ShapeDtypeStruct((N_TOK, D_MODEL), F32),
            jax.ShapeDtypeStruct((nb, N_SAMPLE, TN), F32),
            jax.ShapeDtypeStruct((n_seq, nb, _HALO, TN), F32),
        ],
        scratch_shapes=[
            pltpu.VMEM((TM, D_MODEL), BF16),
            pltpu.VMEM((TM + _HALO, TN), F32),
            pltpu.VMEM((nb, _HALO, TN), F32),
            pltpu.VMEM((TM, TN), BF16),
        ],
        compiler_params=_ARB2,
        name="sconv",
    )(h, gain, w_in, w_in, w_in, w_conv, st0, st1, w_out)


def _ple_kernel(h_ref, g_ref, wg_ref, pp_ref, ps_ref, wp_ref, gf_ref, o_ref, *rest, final_norm):
    if final_norm:
        os_ref, pb_ref = rest
    else:
        (pb_ref,) = rest
    i = pl.program_id(0)

    @pl.when(i < N_TILE - 1)
    def _():
        pb_ref[...] = pp_ref[...].astype(BF16)

    @pl.when(i == N_TILE - 1)
    def _():
        pb_ref[0:SAMPLE_ROW0, :] = pp_ref[0:SAMPLE_ROW0, :].astype(BF16)
        pb_ref[SAMPLE_ROW0:, :] = ps_ref[...].astype(BF16)

    xn = _rms_bf16(h_ref[...], g_ref[...])
    pb = pb_ref[...]
    for c in range(D_MODEL // TN):
        cols = slice(c * TN, (c + 1) * TN)
        gate = jax.nn.sigmoid(_dot(xn, wg_ref[:, cols]))
        o_ref[:, cols] = h_ref[:, cols] + gate * _dot(pb, wp_ref[:, cols])
    if final_norm:
        hn = o_ref[...]
        ms = jnp.mean(hn * hn, axis=-1, keepdims=True)
        o_ref[...] = hn * lax.rsqrt(ms + EPS) * gf_ref[...]

        @pl.when(i == N_TILE - 1)
        def _():
            os_ref[...] = o_ref[SAMPLE_ROW0:, :]


def _ple(h, gain, w_gate, p_prompt, p_sample, w_proj, g_final, layer, final_norm):
    row_block = pl.BlockSpec((TM, D_MODEL), lambda i: (i, 0))
    if final_norm:
        out_specs = [row_block, pl.BlockSpec((N_SAMPLE, D_MODEL), lambda i: (0, 0))]
        out_shape = [jax.ShapeDtypeStruct((N_PROMPT, D_MODEL), F32),
                     jax.ShapeDtypeStruct((N_SAMPLE, D_MODEL), F32)]
    else:
        out_specs = row_block
        out_shape = jax.ShapeDtypeStruct((N_TOK, D_MODEL), F32)
    return pl.pallas_call(
        functools.partial(_ple_kernel, final_norm=final_norm),
        grid=(N_TILE,),
        in_specs=[
            row_block,
            pl.BlockSpec((None, 1, D_MODEL), lambda i: (layer, 0, 0)),
            pl.BlockSpec((None, D_MODEL, D_MODEL), lambda i: (layer, 0, 0)),
            pl.BlockSpec((None, TM, D_PLE), lambda i: (layer, i, 0)),
            pl.BlockSpec((None, N_SAMPLE, D_PLE), lambda i: (layer, 0, 0)),
            pl.BlockSpec((None, D_PLE, D_MODEL), lambda i: (layer, 0, 0)),
            pl.BlockSpec((1, D_MODEL), lambda i: (0, 0)),
        ],
        out_specs=out_specs,
        out_shape=out_shape,
        scratch_shapes=[pltpu.VMEM((TM, D_PLE), BF16)],
        compiler_params=_ARB1,
        name="ple",
    )(h, gain, w_gate, p_prompt, p_sample, w_proj, g_final)


def kernel(x_prompt, x_sample, state_conv, p_prompt, p_sample, ffn1_norm, ffn1_w_gate, ffn1_w_up, ffn1_w_down, mix_norm, a_w_in, a_ln_g, a_ln_b, a_w_s, a_b_s, a_w_out, c_w_in, c_w_conv, c_w_out, ffn2_norm, ffn2_w_gate, ffn2_w_up, ffn2_w_down, ple_norm, ple_w_gate, ple_w_proj, final_norm):
    bf = lambda w: w.astype(BF16)
    gain3 = lambda g: g.reshape(g.shape[0], 1, D_MODEL)

    h = (x_prompt.reshape(N_PROMPT, D_MODEL), x_sample.reshape(N_SAMPLE, D_MODEL))
    p_prompt = p_prompt.reshape(DEPTH, N_PROMPT, D_PLE)
    p_sample = p_sample.reshape(DEPTH, N_SAMPLE, D_PLE)

    f1 = (gain3(ffn1_norm), ffn1_w_gate, ffn1_w_up, ffn1_w_down)
    f2 = (gain3(ffn2_norm), ffn2_w_gate, ffn2_w_up, ffn2_w_down)
    mix_gain = gain3(mix_norm)
    ple_gain = gain3(ple_norm)
    a_w_in_b, a_w_out_b = bf(a_w_in), bf(a_w_out)
    c_w_in_b, c_w_out_b = bf(c_w_in), bf(c_w_out)
    ple_w_gate_b, ple_w_proj_b = bf(ple_w_gate), bf(ple_w_proj)
    ln_g = a_ln_g.reshape(-1, N_HEAD, 1, D_HEAD)
    ln_b = a_ln_b.reshape(-1, N_HEAD, 1, D_HEAD)
    b_s = a_b_s.reshape(-1, N_HEAD, CHUNK, 1)
    g_final = final_norm.reshape(1, D_MODEL)

    conv_new_prompt, conv_new_sample, v_new = [], [], []
    for layer in range(DEPTH):
        h = _ffn(h, *f1, layer)
        jm = layer // 2
        if layer % 2 == 0:
            h, v_s = _gmlp(h, mix_gain, a_w_in_b, ln_g, ln_b, a_w_s, b_s, a_w_out_b, layer, jm)
            v_new.append(v_s.transpose(1, 0, 2).reshape(N_SAMPLE, 1, E_MIX))
        else:
            st0 = state_conv[jm, :, 0, :]
            st1 = state_conv[jm, :, 1, :]
            h, ci_s, ci_p = _conv(h, mix_gain, c_w_in_b, c_w_conv, st0, st1, c_w_out_b, layer, jm)
            ci_s = ci_s.transpose(1, 0, 2).reshape(N_SAMPLE, E_MIX)
            ci_p = ci_p.transpose(0, 2, 1, 3).reshape(N_PROMPT // SEQ, _HALO, E_MIX)
            conv_new_prompt.append(ci_p[:, _HALO - 2:, :])
            conv_new_sample.append(jnp.stack([st1, ci_s], axis=1))
        h = _ffn(h, *f2, layer)
        h = _ple(h, ple_gain, ple_w_gate_b, p_prompt, p_sample, ple_w_proj_b, g_final, layer,
                 final_norm=(layer == DEPTH - 1))

    y_prompt, y_sample = h
    return (y_prompt.reshape(N_PROMPT // SEQ, SEQ, D_MODEL), y_sample.reshape(N_SAMPLE, 1, D_MODEL),
            jnp.stack(conv_new_prompt), jnp.stack(conv_new_sample), jnp.stack(v_new))
```

```python
import functools

import jax
import jax.numpy as jnp
from jax import lax
from jax.experimental import pallas as pl
from jax.experimental.pallas import tpu as pltpu

F32 = jnp.float32
BF16 = jnp.bfloat16

D_MODEL = 2048
D_FF = 5632
E_MIX = 2048
N_HEAD = 8
D_HEAD = 256
CHUNK = 128
D_PLE = 256
SEQ = 2048
N_PROMPT = 4 * SEQ
N_SAMPLE = 128
N_TOK = N_PROMPT + N_SAMPLE
DEPTH = 2
EPS = 1e-6

TM = 640
N_TILE = N_TOK // TM
SAMPLE_ROW0 = TM - N_SAMPLE
TM_FFN = 1040
N_TILE_FFN = N_TOK // TM_FFN
SAMPLE_ROW0_FFN = TM_FFN - N_SAMPLE
TF = 256
TN = 512
VMEM_LIMIT = 62 * 1024 * 1024

_ARB2 = pltpu.CompilerParams(dimension_semantics=("arbitrary", "arbitrary"),
                             vmem_limit_bytes=VMEM_LIMIT)
_ARB1 = pltpu.CompilerParams(dimension_semantics=("arbitrary",),
                             vmem_limit_bytes=VMEM_LIMIT)


def _rms_bf16(x, g):
    ms = jnp.mean(x * x, axis=-1, keepdims=True)
    return (x * lax.rsqrt(ms + EPS) * g).astype(BF16)


def _dot(a, b):
    return jnp.dot(a, b, preferred_element_type=F32)


def _gelu_tanh(x):
    c = 0.7978845608028654
    return x * (0.5 * (1.0 + jnp.tanh(c * (x + 0.044715 * (x * x * x)))))


N_FSTEP = D_FF // TF
N_STEP_FFN = N_TILE_FFN * N_FSTEP
W_RING = 3
W_AHEAD = W_RING - 1


def _ffn_weight_copies(layer, step, w_hbm, w_buf, sem):
    wg_hbm, wu_hbm, wd_hbm = w_hbm
    wg_buf, wu_buf, wd_buf = w_buf
    slot = step % W_RING
    cols = pl.ds(pl.multiple_of((step % N_FSTEP) * TF, TF), TF)
    return (
        pltpu.make_async_copy(wg_hbm.at[layer, :, cols], wg_buf.at[slot], sem.at[0, slot]),
        pltpu.make_async_copy(wu_hbm.at[layer, :, cols], wu_buf.at[slot], sem.at[1, slot]),
        pltpu.make_async_copy(wd_hbm.at[layer, cols, :], wd_buf.at[slot], sem.at[2, slot]),
    )


def _ffn_start(x, g_ref, o_ref, xn_ref):
    o_ref[...] = x
    xn_ref[...] = _rms_bf16(x, g_ref[...])


def _ffn_step(layer, w_hbm, w_buf, sem, o_ref, xn_ref):
    step = pl.program_id(0) * N_FSTEP + pl.program_id(1)
    copies = functools.partial(_ffn_weight_copies, layer, w_hbm=w_hbm, w_buf=w_buf, sem=sem)

    @pl.when(step == 0)
    def _():
        for ahead in range(W_AHEAD):
            for c in copies(ahead):
                c.start()

    @pl.when(step + W_AHEAD < N_STEP_FFN)
    def _():
        for c in copies(step + W_AHEAD):
            c.start()

    for c in copies(step):
        c.wait()

    slot = step % W_RING
    wg_buf, wu_buf, wd_buf = w_buf
    xn = xn_ref[...]
    gt = _dot(xn, wg_buf[slot].astype(BF16))
    up = _dot(xn, wu_buf[slot].astype(BF16))
    hd = (gt * jax.nn.sigmoid(gt)) * up * 0.5
    o_ref[...] += _dot(hd.astype(BF16), wd_buf[slot].astype(BF16))


def _ffn_kernel(h_ref, g_ref, wg_hbm, wu_hbm, wd_hbm, o_ref,
                xn_ref, wg_buf, wu_buf, wd_buf, sem, *, layer):
    @pl.when(pl.program_id(1) == 0)
    def _():
        _ffn_start(h_ref[...], g_ref, o_ref, xn_ref)

    _ffn_step(layer, (wg_hbm, wu_hbm, wd_hbm), (wg_buf, wu_buf, wd_buf), sem, o_ref, xn_ref)


def _ffn_split_kernel(xp_ref, xs_ref, g_ref, wg_hbm, wu_hbm, wd_hbm, o_ref,
                      xn_ref, wg_buf, wu_buf, wd_buf, sem, *, layer):
    i = pl.program_id(0)
    first_step = pl.program_id(1) == 0

    @pl.when(first_step & (i < N_TILE_FFN - 1))
    def _():
        _ffn_start(xp_ref[...], g_ref, o_ref, xn_ref)

    @pl.when(first_step & (i == N_TILE_FFN - 1))
    def _():
        x = jnp.concatenate([xp_ref[0:SAMPLE_ROW0_FFN, :], xs_ref[...]], axis=0)
        _ffn_start(x, g_ref, o_ref, xn_ref)

    _ffn_step(layer, (wg_hbm, wu_hbm, wd_hbm), (wg_buf, wu_buf, wd_buf), sem, o_ref, xn_ref)


def _ffn(h, gain, w_gate, w_up, w_down, layer):
    split = isinstance(h, tuple)
    row_specs = [pl.BlockSpec((TM_FFN, D_MODEL), lambda i, j: (i, 0))]
    if split:
        row_specs.append(pl.BlockSpec((N_SAMPLE, D_MODEL), lambda i, j: (0, 0)))
    hbm = pl.BlockSpec(memory_space=pl.ANY)
    return pl.pallas_call(
        functools.partial(_ffn_split_kernel if split else _ffn_kernel, layer=layer),
        grid=(N_TILE_FFN, N_FSTEP),
        in_specs=row_specs + [
            pl.BlockSpec((None, 1, D_MODEL), lambda i, j: (layer, 0, 0)), hbm, hbm, hbm],
        out_specs=pl.BlockSpec((TM_FFN, D_MODEL), lambda i, j: (i, 0)),
        out_shape=jax.ShapeDtypeStruct((N_TOK, D_MODEL), F32),
        scratch_shapes=[
            pltpu.VMEM((TM_FFN, D_MODEL), BF16),
            pltpu.VMEM((W_RING, D_MODEL, TF), F32),
            pltpu.VMEM((W_RING, D_MODEL, TF), F32),
            pltpu.VMEM((W_RING, TF, D_MODEL), F32),
            pltpu.SemaphoreType.DMA((3, W_RING)),
        ],
        compiler_params=_ARB2,
        name="ffn",
    )(*(h if split else (h,)), gain, w_gate, w_up, w_down)


_GMLP_PROJ_STEPS = N_HEAD
_GMLP_HEADS_PER_MIX = 2
_GMLP_STEPS = _GMLP_PROJ_STEPS + N_HEAD // _GMLP_HEADS_PER_MIX


def _gmlp_kernel(h_ref, g_ref, wu_ref, wv_ref, lng_ref, lnb_ref, ws_ref, bs_ref, wo_ref,
                 o_ref, vo_ref,
                 xn_ref, u_ref, v_ref, sum_ref, mu_ref, rstd_ref, y_ref):
    i = pl.program_id(0)
    j = pl.program_id(1)

    @pl.when(j == 0)
    def _():
        x = h_ref[...]
        xn_ref[...] = _rms_bf16(x, g_ref[...])
        o_ref[...] = x
        sum_ref[...] = jnp.zeros_like(sum_ref)

    @pl.when(j < _GMLP_PROJ_STEPS)
    def _():
        xn = xn_ref[...]
        u_ref[j] = _gelu_tanh(_dot(xn, wu_ref[...]))
        v = _gelu_tanh(_dot(xn, wv_ref[...]))
        v_ref[j] = v
        sum_ref[...] += jnp.sum(v, axis=-1, keepdims=True)

    @pl.when(j == _GMLP_PROJ_STEPS)
    def _():
        mu = sum_ref[...] * (1.0 / E_MIX)
        var = jnp.zeros_like(mu)
        for k in range(N_HEAD):
            d = v_ref[k] - mu
            var = var + jnp.sum(d * d, axis=-1, keepdims=True)
        mu_ref[...] = mu
        rstd_ref[...] = lax.rsqrt(var * (1.0 / E_MIX) + EPS)

    @pl.when(j >= _GMLP_PROJ_STEPS)
    def _():
        is_last_tile = i == N_TILE - 1
        row = lax.broadcasted_iota(jnp.int32, (CHUNK, CHUNK), 0)
        col = lax.broadcasted_iota(jnp.int32, (CHUNK, CHUNK), 1)
        mu = mu_ref[...]
        rstd = rstd_ref[...]
        for hh in range(_GMLP_HEADS_PER_MIX):
            head = (j - _GMLP_PROJ_STEPS) * _GMLP_HEADS_PER_MIX + hh
            lanes = slice(hh * D_HEAD, (hh + 1) * D_HEAD)
            vn = (v_ref[head] - mu) * rstd * lng_ref[head] + lnb_ref[head]

            @pl.when(is_last_tile)
            def _():
                vo_ref[head] = vn[SAMPLE_ROW0:, :]

            w = ws_ref[head]
            b = bs_ref[head]
            w_causal = jnp.where(col <= row, w, 0.0)
            w_sample = jnp.where(col == row, w[0:1, 0:1], 0.0)
            b_sample = jnp.broadcast_to(b[0:1, :], (CHUNK, 1))
            w_last = jnp.where(is_last_tile, w_sample, w_causal).astype(BF16)
            b_last = jnp.where(is_last_tile, b_sample, b)
            w_causal = w_causal.astype(BF16)
            n_chunk = TM // CHUNK
            for c in range(n_chunk):
                rows = slice(c * CHUNK, (c + 1) * CHUNK)
                wm, bm = (w_last, b_last) if c == n_chunk - 1 else (w_causal, b)
                s = _dot(wm, vn[rows, :].astype(BF16)) + bm
                y_ref[rows, lanes] = (u_ref[head, rows, :] * s).astype(BF16)
        o_ref[...] += _dot(y_ref[...], wo_ref[...])


def _gmlp(h, gain, w_in, ln_g, ln_b, w_s, b_s, w_out, layer, jm):
    proj_last = _GMLP_PROJ_STEPS - 1
    mix_w = _GMLP_HEADS_PER_MIX * D_HEAD

    def mix_step(j):
        return jnp.maximum(j - _GMLP_PROJ_STEPS, 0)

    return pl.pallas_call(
        _gmlp_kernel,
        grid=(N_TILE, _GMLP_STEPS),
        in_specs=[
            pl.BlockSpec((TM, D_MODEL), lambda i, j: (i, 0)),
            pl.BlockSpec((None, 1, D_MODEL), lambda i, j: (layer, 0, 0)),
            pl.BlockSpec((None, D_MODEL, D_HEAD), lambda i, j: (jm, 0, jnp.minimum(j, proj_last))),
            pl.BlockSpec((None, D_MODEL, D_HEAD),
                         lambda i, j: (jm, 0, N_HEAD + jnp.minimum(j, proj_last))),
            pl.BlockSpec((None, N_HEAD, 1, D_HEAD), lambda i, j: (jm, 0, 0, 0)),
            pl.BlockSpec((None, N_HEAD, 1, D_HEAD), lambda i, j: (jm, 0, 0, 0)),
            pl.BlockSpec((None, N_HEAD, CHUNK, CHUNK), lambda i, j: (jm, 0, 0, 0)),
            pl.BlockSpec((None, N_HEAD, CHUNK, 1), lambda i, j: (jm, 0, 0, 0)),
            pl.BlockSpec((None, mix_w, D_MODEL), lambda i, j: (jm, mix_step(j), 0)),
        ],
        out_specs=[
            pl.BlockSpec((TM, D_MODEL), lambda i, j: (i, 0)),
            pl.BlockSpec((N_HEAD, N_SAMPLE, D_HEAD), lambda i, j: (0, 0, 0)),
        ],
        out_shape=[
            jax.ShapeDtypeStruct((N_TOK, D_MODEL), F32),
            jax.ShapeDtypeStruct((N_HEAD, N_SAMPLE, D_HEAD), F32),
        ],
        scratch_shapes=[
            pltpu.VMEM((TM, D_MODEL), BF16),
            pltpu.VMEM((N_HEAD, TM, D_HEAD), F32),
            pltpu.VMEM((N_HEAD, TM, D_HEAD), F32),
            pltpu.VMEM((TM, 1), F32),
            pltpu.VMEM((TM, 1), F32),
            pltpu.VMEM((TM, 1), F32),
            pltpu.VMEM((TM, mix_w), BF16),
        ],
        compiler_params=_ARB2,
        name="gmlp",
    )(h, gain, w_in, w_in, ln_g, ln_b, w_s, b_s, w_out)


_HALO = 8


_SEQ_TAIL = [divmod((s + 1) * SEQ - _HALO, TM) for s in range(N_PROMPT // SEQ)]


def _conv_kernel(h_ref, g_ref, wb_ref, wc_ref, wx_ref, wk_ref, st0_ref, st1_ref, wo_ref,
                 o_ref, cs_ref, cp_ref,
                 xn_ref, buf_ref, carry_ref, y_ref):
    i = pl.program_id(0)
    j = pl.program_id(1)

    @pl.when(j == 0)
    def _():
        x = h_ref[...]
        xn_ref[...] = _rms_bf16(x, g_ref[...])
        o_ref[...] = x

    @pl.when(i == 0)
    def _():
        carry_ref[j] = jnp.zeros((_HALO, TN), F32)

    xn = xn_ref[...]
    bg = _dot(xn, wb_ref[...])
    ci = _dot(xn, wc_ref[...]) * _dot(xn, wx_ref[...])

    for s, (tile, row0) in enumerate(_SEQ_TAIL):
        @pl.when(i == tile)
        def _():
            cp_ref[s, j] = ci[row0:row0 + _HALO, :]

    buf_ref[0:_HALO, :] = carry_ref[j]
    buf_ref[_HALO:, :] = ci
    carry_ref[j] = ci[TM - _HALO:, :]
    pos = (lax.broadcasted_iota(jnp.int32, (TM, 1), 0) + i * TM) & (SEQ - 1)
    prev1 = jnp.where(pos >= 1, buf_ref[_HALO - 1:_HALO - 1 + TM, :], 0.0)
    prev2 = jnp.where(pos >= 2, buf_ref[_HALO - 2:_HALO - 2 + TM, :], 0.0)
    w0 = wk_ref[0:1, :]
    w1 = wk_ref[1:2, :]
    w2 = wk_ref[2:3, :]
    co = w0 * prev2 + w1 * prev1 + w2 * ci
    y_ref[...] = (bg * co).astype(BF16)

    @pl.when(i == N_TILE - 1)
    def _():
        co_s = w0 * st0_ref[...] + w1 * st1_ref[...] + w2 * ci[SAMPLE_ROW0:, :]
        y_ref[SAMPLE_ROW0:, :] = (bg[SAMPLE_ROW0:, :] * co_s).astype(BF16)
        cs_ref[j] = ci[SAMPLE_ROW0:, :]

    o_ref[...] += _dot(y_ref[...], wo_ref[...])


def _conv(h, gain, w_in, w_conv, st0, st1, w_out, layer, jm):
    nb = E_MIX // TN
    n_seq = N_PROMPT // SEQ
    return pl.pallas_call(
        _conv_kernel,
        grid=(N_TILE, nb),
        in_specs=[
            pl.BlockSpec((TM, D_MODEL), lambda i, j: (i, 0)),
            pl.BlockSpec((None, 1, D_MODEL), lambda i, j: (layer, 0, 0)),
            pl.BlockSpec((None, D_MODEL, TN), lambda i, j: (jm, 0, j)),
            pl.BlockSpec((None, D_MODEL, TN), lambda i, j: (jm, 0, nb + j)),
            pl.BlockSpec((None, D_MODEL, TN), lambda i, j: (jm, 0, 2 * nb + j)),
            pl.BlockSpec((None, 3, TN), lambda i, j: (jm, 0, j)),
            pl.BlockSpec((N_SAMPLE, TN), lambda i, j: (0, j)),
            pl.BlockSpec((N_SAMPLE, TN), lambda i, j: (0, j)),
            pl.BlockSpec((None, TN, D_MODEL), lambda i, j: (jm, j, 0)),
        ],
        out_specs=[
            pl.BlockSpec((TM, D_MODEL), lambda i, j: (i, 0)),
            pl.BlockSpec((nb, N_SAMPLE, TN), lambda i, j: (0, 0, 0)),
            pl.BlockSpec((n_seq, nb, _HALO, TN), lambda i, j: (0, 0, 0, 0)),
        ],
        out_shape=[
            jax.ShapeDtypeStruct((N_TOK, D_MODEL), F32),
            jax.ShapeDtypeStruct((nb, N_SAMPLE, TN), F32),
            jax.ShapeDtypeStruct((n_seq, nb, _HALO, TN), F32),
        ],
        scratch_shapes=[
            pltpu.VMEM((TM, D_MODEL), BF16),
            pltpu.VMEM((TM + _HALO, TN), F32),
            pltpu.VMEM((nb, _HALO, TN), F32),
            pltpu.VMEM((TM, TN), BF16),
        ],
        compiler_params=_ARB2,
        name="sconv",
    )(h, gain, w_in, w_in, w_in, w_conv, st0, st1, w_out)


def _ple_kernel(h_ref, g_ref, wg_ref, pp_ref, ps_ref, wp_ref, gf_ref, o_ref, *rest, final_norm):
    if final_norm:
        os_ref, pb_ref = rest
    else:
        (pb_ref,) = rest
    i = pl.program_id(0)

    @pl.when(i < N_TILE - 1)
    def _():
        pb_ref[...] = pp_ref[...].astype(BF16)

    @pl.when(i == N_TILE - 1)
    def _():
        pb_ref[0:SAMPLE_ROW0, :] = pp_ref[0:SAMPLE_ROW0, :].astype(BF16)
        pb_ref[SAMPLE_ROW0:, :] = ps_ref[...].astype(BF16)

    xn = _rms_bf16(h_ref[...], g_ref[...])
    pb = pb_ref[...]
    for c in range(D_MODEL // TN):
        cols = slice(c * TN, (c + 1) * TN)
        gate = jax.nn.sigmoid(_dot(xn, wg_ref[:, cols]))
        o_ref[:, cols] = h_ref[:, cols] + gate * _dot(pb, wp_ref[:, cols])
    if final_norm:
        hn = o_ref[...]
        ms = jnp.mean(hn * hn, axis=-1, keepdims=True)
        o_ref[...] = hn * lax.rsqrt(ms + EPS) * gf_ref[...]

        @pl.when(i == N_TILE - 1)
        def _():
            os_ref[...] = o_ref[SAMPLE_ROW0:, :]


def _ple(h, gain, w_gate, p_prompt, p_sample, w_proj, g_final, layer, final_norm):
    row_block = pl.BlockSpec((TM, D_MODEL), lambda i: (i, 0))
    if final_norm:
        out_specs = [row_block, pl.BlockSpec((N_SAMPLE, D_MODEL), lambda i: (0, 0))]
        out_shape = [jax.ShapeDtypeStruct((N_PROMPT, D_MODEL), F32),
                     jax.ShapeDtypeStruct((N_SAMPLE, D_MODEL), F32)]
    else:
        out_specs = row_block
        out_shape = jax.ShapeDtypeStruct((N_TOK, D_MODEL), F32)
    return pl.pallas_call(
        functools.partial(_ple_kernel, final_norm=final_norm),
        grid=(N_TILE,),
        in_specs=[
            row_block,
            pl.BlockSpec((None, 1, D_MODEL), lambda i: (layer, 0, 0)),
            pl.BlockSpec((None, D_MODEL, D_MODEL), lambda i: (layer, 0, 0)),
            pl.BlockSpec((None, TM, D_PLE), lambda i: (layer, i, 0)),
            pl.BlockSpec((None, N_SAMPLE, D_PLE), lambda i: (layer, 0, 0)),
            pl.BlockSpec((None, D_PLE, D_MODEL), lambda i: (layer, 0, 0)),
            pl.BlockSpec((1, D_MODEL), lambda i: (0, 0)),
        ],
        out_specs=out_specs,
        out_shape=out_shape,
        scratch_shapes=[pltpu.VMEM((TM, D_PLE), BF16)],
        compiler_params=_ARB1,
        name="ple",
    )(h, gain, w_gate, p_prompt, p_sample, w_proj, g_final)


def kernel(x_prompt, x_sample, state_conv, p_prompt, p_sample, ffn1_norm, ffn1_w_gate, ffn1_w_up, ffn1_w_down, mix_norm, a_w_in, a_ln_g, a_ln_b, a_w_s, a_b_s, a_w_out, c_w_in, c_w_conv, c_w_out, ffn2_norm, ffn2_w_gate, ffn2_w_up, ffn2_w_down, ple_norm, ple_w_gate, ple_w_proj, final_norm):
    bf = lambda w: w.astype(BF16)
    gain3 = lambda g: g.reshape(g.shape[0], 1, D_MODEL)

    h = (x_prompt.reshape(N_PROMPT, D_MODEL), x_sample.reshape(N_SAMPLE, D_MODEL))
    p_prompt = p_prompt.reshape(DEPTH, N_PROMPT, D_PLE)
    p_sample = p_sample.reshape(DEPTH, N_SAMPLE, D_PLE)

    f1 = (gain3(ffn1_norm), ffn1_w_gate, ffn1_w_up, ffn1_w_down)
    f2 = (gain3(ffn2_norm), ffn2_w_gate, ffn2_w_up, ffn2_w_down)
    mix_gain = gain3(mix_norm)
    ple_gain = gain3(ple_norm)
    a_w_in_b, a_w_out_b = bf(a_w_in), bf(a_w_out)
    c_w_in_b, c_w_out_b = bf(c_w_in), bf(c_w_out)
    ple_w_gate_b, ple_w_proj_b = bf(ple_w_gate), bf(ple_w_proj)
    ln_g = a_ln_g.reshape(-1, N_HEAD, 1, D_HEAD)
    ln_b = a_ln_b.reshape(-1, N_HEAD, 1, D_HEAD)
    b_s = a_b_s.reshape(-1, N_HEAD, CHUNK, 1)
    g_final = final_norm.reshape(1, D_MODEL)

    conv_new_prompt, conv_new_sample, v_new = [], [], []
    for layer in range(DEPTH):
        h = _ffn(h, *f1, layer)
        jm = layer // 2
        if layer % 2 == 0:
            h, v_s = _gmlp(h, mix_gain, a_w_in_b, ln_g, ln_b, a_w_s, b_s, a_w_out_b, layer, jm)
            v_new.append(v_s.transpose(1, 0, 2).reshape(N_SAMPLE, 1, E_MIX))
        else:
            st0 = state_conv[jm, :, 0, :]
            st1 = state_conv[jm, :, 1, :]
            h, ci_s, ci_p = _conv(h, mix_gain, c_w_in_b, c_w_conv, st0, st1, c_w_out_b, layer, jm)
            ci_s = ci_s.transpose(1, 0, 2).reshape(N_SAMPLE, E_MIX)
            ci_p = ci_p.transpose(0, 2, 1, 3).reshape(N_PROMPT // SEQ, _HALO, E_MIX)
            conv_new_prompt.append(ci_p[:, _HALO - 2:, :])
            conv_new_sample.append(jnp.stack([st1, ci_s], axis=1))
        h = _ffn(h, *f2, layer)
        h = _ple(h, ple_gain, ple_w_gate_b, p_prompt, p_sample, ple_w_proj_b, g_final, layer,
                 final_norm=(layer == DEPTH - 1))

    y_prompt, y_sample = h
    return (y_prompt.reshape(N_PROMPT // SEQ, SEQ, D_MODEL), y_sample.reshape(N_SAMPLE, 1, D_MODEL),
            jnp.stack(conv_new_prompt), jnp.stack(conv_new_sample), jnp.stack(v_new))
```

```python
import functools

import jax
import jax.numpy as jnp
from jax import lax
from jax.experimental import pallas as pl
from jax.experimental.pallas import tpu as pltpu

F32 = jnp.float32
BF16 = jnp.bfloat16

D_MODEL = 2048
D_FF = 5632
E_MIX = 2048
N_HEAD = 8
D_HEAD = 256
CHUNK = 128
D_PLE = 256
SEQ = 2048
N_PROMPT = 4 * SEQ
N_SAMPLE = 128
N_TOK = N_PROMPT + N_SAMPLE
DEPTH = 2
EPS = 1e-6

TM = 640
N_TILE = N_TOK // TM
SAMPLE_ROW0 = TM - N_SAMPLE
TM_FFN = 1040
N_TILE_FFN = N_TOK // TM_FFN
SAMPLE_ROW0_FFN = TM_FFN - N_SAMPLE
TF = 256
TN = 512
VMEM_LIMIT = 62 * 1024 * 1024

_ARB2 = pltpu.CompilerParams(dimension_semantics=("arbitrary", "arbitrary"),
                             vmem_limit_bytes=VMEM_LIMIT)
_ARB1 = pltpu.CompilerParams(dimension_semantics=("arbitrary",),
                             vmem_limit_bytes=VMEM_LIMIT)


def _rms_bf16(x, g):
    ms = jnp.mean(x * x, axis=-1, keepdims=True)
    return (x * lax.rsqrt(ms + EPS) * g).astype(BF16)


def _dot(a, b):
    return jnp.dot(a, b, preferred_element_type=F32)


def _gelu_tanh(x):
    c = 0.7978845608028654
    return x * (0.5 * (1.0 + jnp.tanh(c * (x + 0.044715 * (x * x * x)))))


N_FSTEP = D_FF // TF
N_STEP_FFN = N_TILE_FFN * N_FSTEP
W_RING = 3
W_AHEAD = W_RING - 1


def _ffn_weight_copies(layer, step, w_hbm, w_buf, sem):
    wg_hbm, wu_hbm, wd_hbm = w_hbm
    wg_buf, wu_buf, wd_buf = w_buf
    slot = step % W_RING
    cols = pl.ds(pl.multiple_of((step % N_FSTEP) * TF, TF), TF)
    return (
        pltpu.make_async_copy(wg_hbm.at[layer, :, cols], wg_buf.at[slot], sem.at[0, slot]),
        pltpu.make_async_copy(wu_hbm.at[layer, :, cols], wu_buf.at[slot], sem.at[1, slot]),
        pltpu.make_async_copy(wd_hbm.at[layer, cols, :], wd_buf.at[slot], sem.at[2, slot]),
    )


def _ffn_start(x, g_ref, o_ref, xn_ref):
    o_ref[...] = x
    xn_ref[...] = _rms_bf16(x, g_ref[...])


def _ffn_blocks(layer, w_hbm, w_buf, sem, o_ref, xn_ref):
    tile = pl.program_id(0)
    copies = functools.partial(_ffn_weight_copies, layer, w_hbm=w_hbm, w_buf=w_buf, sem=sem)
    wg_buf, wu_buf, wd_buf = w_buf

    @pl.when(tile == 0)
    def _():
        for ahead in range(W_AHEAD):
            for c in copies(ahead):
                c.start()

    def block(jb, carry):
        step = tile * N_FSTEP + jb

        @pl.when(step + W_AHEAD < N_STEP_FFN)
        def _():
            for c in copies(step + W_AHEAD):
                c.start()

        for c in copies(step):
            c.wait()

        slot = step % W_RING
        xn = xn_ref[...]
        gt = _dot(xn, wg_buf[slot].astype(BF16))
        up = _dot(xn, wu_buf[slot].astype(BF16))
        hd = (gt * jax.nn.sigmoid(gt)) * up * 0.5
        o_ref[...] += _dot(hd.astype(BF16), wd_buf[slot].astype(BF16))
        return carry

    lax.fori_loop(0, N_FSTEP, block, 0)


def _ffn_kernel(h_ref, g_ref, wg_hbm, wu_hbm, wd_hbm, o_ref,
                xn_ref, wg_buf, wu_buf, wd_buf, sem, *, layer):
    _ffn_start(h_ref[...], g_ref, o_ref, xn_ref)
    _ffn_blocks(layer, (wg_hbm, wu_hbm, wd_hbm), (wg_buf, wu_buf, wd_buf), sem, o_ref, xn_ref)


def _ffn_split_kernel(xp_ref, xs_ref, g_ref, wg_hbm, wu_hbm, wd_hbm, o_ref,
                      xn_ref, wg_buf, wu_buf, wd_buf, sem, *, layer):
    tile = pl.program_id(0)

    @pl.when(tile < N_TILE_FFN - 1)
    def _():
        _ffn_start(xp_ref[...], g_ref, o_ref, xn_ref)

    @pl.when(tile == N_TILE_FFN - 1)
    def _():
        x = jnp.concatenate([xp_ref[0:SAMPLE_ROW0_FFN, :], xs_ref[...]], axis=0)
        _ffn_start(x, g_ref, o_ref, xn_ref)

    _ffn_blocks(layer, (wg_hbm, wu_hbm, wd_hbm), (wg_buf, wu_buf, wd_buf), sem, o_ref, xn_ref)


def _ffn(h, gain, w_gate, w_up, w_down, layer):
    split = isinstance(h, tuple)
    row_specs = [pl.BlockSpec((TM_FFN, D_MODEL), lambda i: (i, 0))]
    if split:
        row_specs.append(pl.BlockSpec((N_SAMPLE, D_MODEL), lambda i: (0, 0)))
    hbm = pl.BlockSpec(memory_space=pl.ANY)
    return pl.pallas_call(
        functools.partial(_ffn_split_kernel if split else _ffn_kernel, layer=layer),
        grid=(N_TILE_FFN,),
        in_specs=row_specs + [
            pl.BlockSpec((None, 1, D_MODEL), lambda i: (layer, 0, 0)), hbm, hbm, hbm],
        out_specs=pl.BlockSpec((TM_FFN, D_MODEL), lambda i: (i, 0)),
        out_shape=jax.ShapeDtypeStruct((N_TOK, D_MODEL), F32),
        scratch_shapes=[
            pltpu.VMEM((TM_FFN, D_MODEL), BF16),
            pltpu.VMEM((W_RING, D_MODEL, TF), F32),
            pltpu.VMEM((W_RING, D_MODEL, TF), F32),
            pltpu.VMEM((W_RING, TF, D_MODEL), F32),
            pltpu.SemaphoreType.DMA((3, W_RING)),
        ],
        compiler_params=_ARB1,
        name="ffn",
    )(*(h if split else (h,)), gain, w_gate, w_up, w_down)


_GMLP_PROJ_STEPS = N_HEAD
_GMLP_HEADS_PER_MIX = 2
_GMLP_STEPS = _GMLP_PROJ_STEPS + N_HEAD // _GMLP_HEADS_PER_MIX


def _gmlp_kernel(h_ref, g_ref, wu_ref, wv_ref, lng_ref, lnb_ref, ws_ref, bs_ref, wo_ref,
                 o_ref, vo_ref,
                 xn_ref, u_ref, v_ref, sum_ref, mu_ref, rstd_ref, y_ref):
    i = pl.program_id(0)
    j = pl.program_id(1)

    @pl.when(j == 0)
    def _():
        x = h_ref[...]
        xn_ref[...] = _rms_bf16(x, g_ref[...])
        o_ref[...] = x
        sum_ref[...] = jnp.zeros_like(sum_ref)

    @pl.when(j < _GMLP_PROJ_STEPS)
    def _():
        xn = xn_ref[...]
        u_ref[j] = _gelu_tanh(_dot(xn, wu_ref[...]))
        v = _gelu_tanh(_dot(xn, wv_ref[...]))
        v_ref[j] = v
        sum_ref[...] += jnp.sum(v, axis=-1, keepdims=True)

    @pl.when(j == _GMLP_PROJ_STEPS)
    def _():
        mu = sum_ref[...] * (1.0 / E_MIX)
        var = jnp.zeros_like(mu)
        for k in range(N_HEAD):
            d = v_ref[k] - mu
            var = var + jnp.sum(d * d, axis=-1, keepdims=True)
        mu_ref[...] = mu
        rstd_ref[...] = lax.rsqrt(var * (1.0 / E_MIX) + EPS)

    @pl.when(j >= _GMLP_PROJ_STEPS)
    def _():
        is_last_tile = i == N_TILE - 1
        row = lax.broadcasted_iota(jnp.int32, (CHUNK, CHUNK), 0)
        col = lax.broadcasted_iota(jnp.int32, (CHUNK, CHUNK), 1)
        mu = mu_ref[...]
        rstd = rstd_ref[...]
        for hh in range(_GMLP_HEADS_PER_MIX):
            head = (j - _GMLP_PROJ_STEPS) * _GMLP_HEADS_PER_MIX + hh
            lanes = slice(hh * D_HEAD, (hh + 1) * D_HEAD)
            vn = (v_ref[head] - mu) * rstd * lng_ref[head] + lnb_ref[head]

            @pl.when(is_last_tile)
            def _():
                vo_ref[head] = vn[SAMPLE_ROW0:, :]

            w = ws_ref[head]
            b = bs_ref[head]
            w_causal = jnp.where(col <= row, w, 0.0)
            w_sample = jnp.where(col == row, w[0:1, 0:1], 0.0)
            b_sample = jnp.broadcast_to(b[0:1, :], (CHUNK, 1))
            w_last = jnp.where(is_last_tile, w_sample, w_causal).astype(BF16)
            b_last = jnp.where(is_last_tile, b_sample, b)
            w_causal = w_causal.astype(BF16)
            n_chunk = TM // CHUNK
            for c in range(n_chunk):
                rows = slice(c * CHUNK, (c + 1) * CHUNK)
                wm, bm = (w_last, b_last) if c == n_chunk - 1 else (w_causal, b)
                s = _dot(wm, vn[rows, :].astype(BF16)) + bm
                y_ref[rows, lanes] = (u_ref[head, rows, :] * s).astype(BF16)
        o_ref[...] += _dot(y_ref[...], wo_ref[...])


def _gmlp(h, gain, w_in, ln_g, ln_b, w_s, b_s, w_out, layer, jm):
    proj_last = _GMLP_PROJ_STEPS - 1
    mix_w = _GMLP_HEADS_PER_MIX * D_HEAD

    def mix_step(j):
        return jnp.maximum(j - _GMLP_PROJ_STEPS, 0)

    return pl.pallas_call(
        _gmlp_kernel,
        grid=(N_TILE, _GMLP_STEPS),
        in_specs=[
            pl.BlockSpec((TM, D_MODEL), lambda i, j: (i, 0)),
            pl.BlockSpec((None, 1, D_MODEL), lambda i, j: (layer, 0, 0)),
            pl.BlockSpec((None, D_MODEL, D_HEAD), lambda i, j: (jm, 0, jnp.minimum(j, proj_last))),
            pl.BlockSpec((None, D_MODEL, D_HEAD),
                         lambda i, j: (jm, 0, N_HEAD + jnp.minimum(j, proj_last))),
            pl.BlockSpec((None, N_HEAD, 1, D_HEAD), lambda i, j: (jm, 0, 0, 0)),
            pl.BlockSpec((None, N_HEAD, 1, D_HEAD), lambda i, j: (jm, 0, 0, 0)),
            pl.BlockSpec((None, N_HEAD, CHUNK, CHUNK), lambda i, j: (jm, 0, 0, 0)),
            pl.BlockSpec((None, N_HEAD, CHUNK, 1), lambda i, j: (jm, 0, 0, 0)),
            pl.BlockSpec((None, mix_w, D_MODEL), lambda i, j: (jm, mix_step(j), 0)),
        ],
        out_specs=[
            pl.BlockSpec((TM, D_MODEL), lambda i, j: (i, 0)),
            pl.BlockSpec((N_HEAD, N_SAMPLE, D_HEAD), lambda i, j: (0, 0, 0)),
        ],
        out_shape=[
            jax.ShapeDtypeStruct((N_TOK, D_MODEL), F32),
            jax.ShapeDtypeStruct((N_HEAD, N_SAMPLE, D_HEAD), F32),
        ],
        scratch_shapes=[
            pltpu.VMEM((TM, D_MODEL), BF16),
            pltpu.VMEM((N_HEAD, TM, D_HEAD), F32),
            pltpu.VMEM((N_HEAD, TM, D_HEAD), F32),
            pltpu.VMEM((TM, 1), F32),
            pltpu.VMEM((TM, 1), F32),
            pltpu.VMEM((TM, 1), F32),
            pltpu.VMEM((TM, mix_w), BF16),
        ],
        compiler_params=_ARB2,
        name="gmlp",
    )(h, gain, w_in, w_in, ln_g, ln_b, w_s, b_s, w_out)


_HALO = 8


_SEQ_TAIL = [divmod((s + 1) * SEQ - _HALO, TM) for s in range(N_PROMPT // SEQ)]


def _conv_kernel(h_ref, g_ref, wb_ref, wc_ref, wx_ref, wk_ref, st0_ref, st1_ref, wo_ref,
                 o_ref, cs_ref, cp_ref,
                 xn_ref, buf_ref, carry_ref, y_ref):
    i = pl.program_id(0)
    j = pl.program_id(1)

    @pl.when(j == 0)
    def _():
        x = h_ref[...]
        xn_ref[...] = _rms_bf16(x, g_ref[...])
        o_ref[...] = x

    @pl.when(i == 0)
    def _():
        carry_ref[j] = jnp.zeros((_HALO, TN), F32)

    xn = xn_ref[...]
    bg = _dot(xn, wb_ref[...])
    ci = _dot(xn, wc_ref[...]) * _dot(xn, wx_ref[...])

    for s, (tile, row0) in enumerate(_SEQ_TAIL):
        @pl.when(i == tile)
        def _():
            cp_ref[s, j] = ci[row0:row0 + _HALO, :]

    buf_ref[0:_HALO, :] = carry_ref[j]
    buf_ref[_HALO:, :] = ci
    carry_ref[j] = ci[TM - _HALO:, :]
    pos = (lax.broadcasted_iota(jnp.int32, (TM, 1), 0) + i * TM) & (SEQ - 1)
    prev1 = jnp.where(pos >= 1, buf_ref[_HALO - 1:_HALO - 1 + TM, :], 0.0)
    prev2 = jnp.where(pos >= 2, buf_ref[_HALO - 2:_HALO - 2 + TM, :], 0.0)
    w0 = wk_ref[0:1, :]
    w1 = wk_ref[1:2, :]
    w2 = wk_ref[2:3, :]
    co = w0 * prev2 + w1 * prev1 + w2 * ci
    y_ref[...] = (bg * co).astype(BF16)

    @pl.when(i == N_TILE - 1)
    def _():
        co_s = w0 * st0_ref[...] + w1 * st1_ref[...] + w2 * ci[SAMPLE_ROW0:, :]
        y_ref[SAMPLE_ROW0:, :] = (bg[SAMPLE_ROW0:, :] * co_s).astype(BF16)
        cs_ref[j] = ci[SAMPLE_ROW0:, :]

    o_ref[...] += _dot(y_ref[...], wo_ref[...])


def _conv(h, gain, w_in, w_conv, st0, st1, w_out, layer, jm):
    nb = E_MIX // TN
    n_seq = N_PROMPT // SEQ
    return pl.pallas_call(
        _conv_kernel,
        grid=(N_TILE, nb),
        in_specs=[
            pl.BlockSpec((TM, D_MODEL), lambda i, j: (i, 0)),
            pl.BlockSpec((None, 1, D_MODEL), lambda i, j: (layer, 0, 0)),
            pl.BlockSpec((None, D_MODEL, TN), lambda i, j: (jm, 0, j)),
            pl.BlockSpec((None, D_MODEL, TN), lambda i, j: (jm, 0, nb + j)),
            pl.BlockSpec((None, D_MODEL, TN), lambda i, j: (jm, 0, 2 * nb + j)),
            pl.BlockSpec((None, 3, TN), lambda i, j: (jm, 0, j)),
            pl.BlockSpec((N_SAMPLE, TN), lambda i, j: (0, j)),
            pl.BlockSpec((N_SAMPLE, TN), lambda i, j: (0, j)),
            pl.BlockSpec((None, TN, D_MODEL), lambda i, j: (jm, j, 0)),
        ],
        out_specs=[
            pl.BlockSpec((TM, D_MODEL), lambda i, j: (i, 0)),
            pl.BlockSpec((nb, N_SAMPLE, TN), lambda i, j: (0, 0, 0)),
            pl.BlockSpec((n_seq, nb, _HALO, TN), lambda i, j: (0, 0, 0, 0)),
        ],
        out_shape=[
            jax.ShapeDtypeStruct((N_TOK, D_MODEL), F32),
            jax.ShapeDtypeStruct((nb, N_SAMPLE, TN), F32),
            jax.ShapeDtypeStruct((n_seq, nb, _HALO, TN), F32),
        ],
        scratch_shapes=[
            pltpu.VMEM((TM, D_MODEL), BF16),
            pltpu.VMEM((TM + _HALO, TN), F32),
            pltpu.VMEM((nb, _HALO, TN), F32),
            pltpu.VMEM((TM, TN), BF16),
        ],
        compiler_params=_ARB2,
        name="sconv",
    )(h, gain, w_in, w_in, w_in, w_conv, st0, st1, w_out)


def _ple_kernel(h_ref, g_ref, wg_ref, pp_ref, ps_ref, wp_ref, gf_ref, o_ref, *rest, final_norm):
    if final_norm:
        os_ref, pb_ref = rest
    else:
        (pb_ref,) = rest
    i = pl.program_id(0)

    @pl.when(i < N_TILE - 1)
    def _():
        pb_ref[...] = pp_ref[...].astype(BF16)

    @pl.when(i == N_TILE - 1)
    def _():
        pb_ref[0:SAMPLE_ROW0, :] = pp_ref[0:SAMPLE_ROW0, :].astype(BF16)
        pb_ref[SAMPLE_ROW0:, :] = ps_ref[...].astype(BF16)

    xn = _rms_bf16(h_ref[...], g_ref[...])
    pb = pb_ref[...]
    for c in range(D_MODEL // TN):
        cols = slice(c * TN, (c + 1) * TN)
        gate = jax.nn.sigmoid(_dot(xn, wg_ref[:, cols]))
        o_ref[:, cols] = h_ref[:, cols] + gate * _dot(pb, wp_ref[:, cols])
    if final_norm:
        hn = o_ref[...]
        ms = jnp.mean(hn * hn, axis=-1, keepdims=True)
        o_ref[...] = hn * lax.rsqrt(ms + EPS) * gf_ref[...]

        @pl.when(i == N_TILE - 1)
        def _():
            os_ref[...] = o_ref[SAMPLE_ROW0:, :]


def _ple(h, gain, w_gate, p_prompt, p_sample, w_proj, g_final, layer, final_norm):
    row_block = pl.BlockSpec((TM, D_MODEL), lambda i: (i, 0))
    if final_norm:
        out_specs = [row_block, pl.BlockSpec((N_SAMPLE, D_MODEL), lambda i: (0, 0))]
        out_shape = [jax.ShapeDtypeStruct((N_PROMPT, D_MODEL), F32),
                     jax.ShapeDtypeStruct((N_SAMPLE, D_MODEL), F32)]
    else:
        out_specs = row_block
        out_shape = jax.ShapeDtypeStruct((N_TOK, D_MODEL), F32)
    return pl.pallas_call(
        functools.partial(_ple_kernel, final_norm=final_norm),
        grid=(N_TILE,),
        in_specs=[
            row_block,
            pl.BlockSpec((None, 1, D_MODEL), lambda i: (layer, 0, 0)),
            pl.BlockSpec((None, D_MODEL, D_MODEL), lambda i: (layer, 0, 0)),
            pl.BlockSpec((None, TM, D_PLE), lambda i: (layer, i, 0)),
            pl.BlockSpec((None, N_SAMPLE, D_PLE), lambda i: (layer, 0, 0)),
            pl.BlockSpec((None, D_PLE, D_MODEL), lambda i: (layer, 0, 0)),
            pl.BlockSpec((1, D_MODEL), lambda i: (0, 0)),
        ],
        out_specs=out_specs,
        out_shape=out_shape,
        scratch_shapes=[pltpu.VMEM((TM, D_PLE), BF16)],
        compiler_params=_ARB1,
        name="ple",
    )(h, gain, w_gate, p_prompt, p_sample, w_proj, g_final)


def kernel(x_prompt, x_sample, state_conv, p_prompt, p_sample, ffn1_norm, ffn1_w_gate, ffn1_w_up, ffn1_w_down, mix_norm, a_w_in, a_ln_g, a_ln_b, a_w_s, a_b_s, a_w_out, c_w_in, c_w_conv, c_w_out, ffn2_norm, ffn2_w_gate, ffn2_w_up, ffn2_w_down, ple_norm, ple_w_gate, ple_w_proj, final_norm):
    bf = lambda w: w.astype(BF16)
    gain3 = lambda g: g.reshape(g.shape[0], 1, D_MODEL)

    h = (x_prompt.reshape(N_PROMPT, D_MODEL), x_sample.reshape(N_SAMPLE, D_MODEL))
    p_prompt = p_prompt.reshape(DEPTH, N_PROMPT, D_PLE)
    p_sample = p_sample.reshape(DEPTH, N_SAMPLE, D_PLE)

    f1 = (gain3(ffn1_norm), ffn1_w_gate, ffn1_w_up, ffn1_w_down)
    f2 = (gain3(ffn2_norm), ffn2_w_gate, ffn2_w_up, ffn2_w_down)
    mix_gain = gain3(mix_norm)
    ple_gain = gain3(ple_norm)
    a_w_in_b, a_w_out_b = bf(a_w_in), bf(a_w_out)
    c_w_in_b, c_w_out_b = bf(c_w_in), bf(c_w_out)
    ple_w_gate_b, ple_w_proj_b = bf(ple_w_gate), bf(ple_w_proj)
    ln_g = a_ln_g.reshape(-1, N_HEAD, 1, D_HEAD)
    ln_b = a_ln_b.reshape(-1, N_HEAD, 1, D_HEAD)
    b_s = a_b_s.reshape(-1, N_HEAD, CHUNK, 1)
    g_final = final_norm.reshape(1, D_MODEL)

    conv_new_prompt, conv_new_sample, v_new = [], [], []
    for layer in range(DEPTH):
        h = _ffn(h, *f1, layer)
        jm = layer // 2
        if layer % 2 == 0:
            h, v_s = _gmlp(h, mix_gain, a_w_in_b, ln_g, ln_b, a_w_s, b_s, a_w_out_b, layer, jm)
            v_new.append(v_s.transpose(1, 0, 2).reshape(N_SAMPLE, 1, E_MIX))
        else:
            st0 = state_conv[jm, :, 0, :]
            st1 = state_conv[jm, :, 1, :]
            h, ci_s, ci_p = _conv(h, mix_gain, c_w_in_b, c_w_conv, st0, st1, c_w_out_b, layer, jm)
            ci_s = ci_s.transpose(1, 0, 2).reshape(N_SAMPLE, E_MIX)
            ci_p = ci_p.transpose(0, 2, 1, 3).reshape(N_PROMPT // SEQ, _HALO, E_MIX)
            conv_new_prompt.append(ci_p[:, _HALO - 2:, :])
            conv_new_sample.append(jnp.stack([st1, ci_s], axis=1))
        h = _ffn(h, *f2, layer)
        h = _ple(h, ple_gain, ple_w_gate_b, p_prompt, p_sample, ple_w_proj_b, g_final, layer,
                 final_norm=(layer == DEPTH - 1))

    y_prompt, y_sample = h
    return (y_prompt.reshape(N_PROMPT // SEQ, SEQ, D_MODEL), y_sample.reshape(N_SAMPLE, 1, D_MODEL),
            jnp.stack(conv_new_prompt), jnp.stack(conv_new_sample), jnp.stack(v_new))
```

```python
import functools

import jax
import jax.numpy as jnp
from jax import lax
from jax.experimental import pallas as pl
from jax.experimental.pallas import tpu as pltpu

F32 = jnp.float32
BF16 = jnp.bfloat16

D_MODEL = 2048
D_FF = 5632
E_MIX = 2048
N_HEAD = 8
D_HEAD = 256
CHUNK = 128
D_PLE = 256
SEQ = 2048
N_PROMPT = 4 * SEQ
N_SAMPLE = 128
N_TOK = N_PROMPT + N_SAMPLE
DEPTH = 2
EPS = 1e-6

TM = 640
N_TILE = N_TOK // TM
SAMPLE_ROW0 = TM - N_SAMPLE
TM_FFN = 1040
N_TILE_FFN = N_TOK // TM_FFN
SAMPLE_ROW0_FFN = TM_FFN - N_SAMPLE
TF = 256
TN = 512
VMEM_LIMIT = 62 * 1024 * 1024

_ARB2 = pltpu.CompilerParams(dimension_semantics=("arbitrary", "arbitrary"),
                             vmem_limit_bytes=VMEM_LIMIT)
_ARB1 = pltpu.CompilerParams(dimension_semantics=("arbitrary",),
                             vmem_limit_bytes=VMEM_LIMIT)


def _rms_bf16(x, g):
    ms = jnp.mean(x * x, axis=-1, keepdims=True)
    return (x * lax.rsqrt(ms + EPS) * g).astype(BF16)


def _dot(a, b):
    return jnp.dot(a, b, preferred_element_type=F32)


def _gelu_tanh(x):
    c = 0.7978845608028654
    return x * (0.5 * (1.0 + jnp.tanh(c * (x + 0.044715 * (x * x * x)))))


N_FSTEP = D_FF // TF
N_STEP_FFN = N_TILE_FFN * N_FSTEP
W_RING = 3
W_AHEAD = W_RING - 1


def _ffn_weight_copies(layer, step, w_hbm, w_buf, sem):
    wg_hbm, wu_hbm, wd_hbm = w_hbm
    wg_buf, wu_buf, wd_buf = w_buf
    slot = step % W_RING
    cols = pl.ds(pl.multiple_of((step % N_FSTEP) * TF, TF), TF)
    return (
        pltpu.make_async_copy(wg_hbm.at[layer, :, cols], wg_buf.at[slot], sem.at[0, slot]),
        pltpu.make_async_copy(wu_hbm.at[layer, :, cols], wu_buf.at[slot], sem.at[1, slot]),
        pltpu.make_async_copy(wd_hbm.at[layer, cols, :], wd_buf.at[slot], sem.at[2, slot]),
    )


def _ffn_start(x, g_ref, o_ref, xn_ref):
    o_ref[...] = x
    xn_ref[...] = _rms_bf16(x, g_ref[...])


def _ffn_blocks(layer, w_hbm, w_buf, sem, o_ref, xn_ref):
    tile = pl.program_id(0)
    copies = functools.partial(_ffn_weight_copies, layer, w_hbm=w_hbm, w_buf=w_buf, sem=sem)
    wg_buf, wu_buf, wd_buf = w_buf

    @pl.when(tile == 0)
    def _():
        for ahead in range(W_AHEAD):
            for c in copies(ahead):
                c.start()

    def prefetch(step):
        @pl.when(step < N_STEP_FFN)
        def _():
            for c in copies(step):
                c.start()

    def block(step):
        slot = step % W_RING
        xn = xn_ref[...]
        gt = _dot(xn, wg_buf[slot].astype(BF16))
        up = _dot(xn, wu_buf[slot].astype(BF16))
        hd = (gt * jax.nn.sigmoid(gt)) * up * 0.5
        o_ref[...] += _dot(hd.astype(BF16), wd_buf[slot].astype(BF16))

    def block_pair(jp, carry):
        step = tile * N_FSTEP + 2 * jp
        prefetch(step + W_AHEAD)
        for c in copies(step) + copies(step + 1):
            c.wait()
        block(step)
        prefetch(step + 1 + W_AHEAD)
        block(step + 1)
        return carry

    lax.fori_loop(0, N_FSTEP // 2, block_pair, 0)


def _ffn_kernel(h_ref, g_ref, wg_hbm, wu_hbm, wd_hbm, o_ref,
                xn_ref, wg_buf, wu_buf, wd_buf, sem, *, layer):
    _ffn_start(h_ref[...], g_ref, o_ref, xn_ref)
    _ffn_blocks(layer, (wg_hbm, wu_hbm, wd_hbm), (wg_buf, wu_buf, wd_buf), sem, o_ref, xn_ref)


def _ffn_split_kernel(xp_ref, xs_ref, g_ref, wg_hbm, wu_hbm, wd_hbm, o_ref,
                      xn_ref, wg_buf, wu_buf, wd_buf, sem, *, layer):
    tile = pl.program_id(0)

    @pl.when(tile < N_TILE_FFN - 1)
    def _():
        _ffn_start(xp_ref[...], g_ref, o_ref, xn_ref)

    @pl.when(tile == N_TILE_FFN - 1)
    def _():
        x = jnp.concatenate([xp_ref[0:SAMPLE_ROW0_FFN, :], xs_ref[...]], axis=0)
        _ffn_start(x, g_ref, o_ref, xn_ref)

    _ffn_blocks(layer, (wg_hbm, wu_hbm, wd_hbm), (wg_buf, wu_buf, wd_buf), sem, o_ref, xn_ref)


def _ffn(h, gain, w_gate, w_up, w_down, layer):
    split = isinstance(h, tuple)
    row_specs = [pl.BlockSpec((TM_FFN, D_MODEL), lambda i: (i, 0))]
    if split:
        row_specs.append(pl.BlockSpec((N_SAMPLE, D_MODEL), lambda i: (0, 0)))
    hbm = pl.BlockSpec(memory_space=pl.ANY)
    return pl.pallas_call(
        functools.partial(_ffn_split_kernel if split else _ffn_kernel, layer=layer),
        grid=(N_TILE_FFN,),
        in_specs=row_specs + [
            pl.BlockSpec((None, 1, D_MODEL), lambda i: (layer, 0, 0)), hbm, hbm, hbm],
        out_specs=pl.BlockSpec((TM_FFN, D_MODEL), lambda i: (i, 0)),
        out_shape=jax.ShapeDtypeStruct((N_TOK, D_MODEL), F32),
        scratch_shapes=[
            pltpu.VMEM((TM_FFN, D_MODEL), BF16),
            pltpu.VMEM((W_RING, D_MODEL, TF), F32),
            pltpu.VMEM((W_RING, D_MODEL, TF), F32),
            pltpu.VMEM((W_RING, TF, D_MODEL), F32),
            pltpu.SemaphoreType.DMA((3, W_RING)),
        ],
        compiler_params=_ARB1,
        name="ffn",
    )(*(h if split else (h,)), gain, w_gate, w_up, w_down)


_GMLP_PROJ_STEPS = N_HEAD
_GMLP_HEADS_PER_MIX = 2
_GMLP_STEPS = _GMLP_PROJ_STEPS + N_HEAD // _GMLP_HEADS_PER_MIX


def _gmlp_kernel(h_ref, g_ref, wu_ref, wv_ref, lng_ref, lnb_ref, ws_ref, bs_ref, wo_ref,
                 o_ref, vo_ref,
                 xn_ref, u_ref, v_ref, sum_ref, mu_ref, rstd_ref, y_ref):
    i = pl.program_id(0)
    j = pl.program_id(1)

    @pl.when(j == 0)
    def _():
        x = h_ref[...]
        xn_ref[...] = _rms_bf16(x, g_ref[...])
        o_ref[...] = x
        sum_ref[...] = jnp.zeros_like(sum_ref)

    @pl.when(j < _GMLP_PROJ_STEPS)
    def _():
        xn = xn_ref[...]
        u_ref[j] = _gelu_tanh(_dot(xn, wu_ref[...]))
        v = _gelu_tanh(_dot(xn, wv_ref[...]))
        v_ref[j] = v
        sum_ref[...] += jnp.sum(v, axis=-1, keepdims=True)

    @pl.when(j == _GMLP_PROJ_STEPS)
    def _():
        mu = sum_ref[...] * (1.0 / E_MIX)
        var = jnp.zeros_like(mu)
        for k in range(N_HEAD):
            d = v_ref[k] - mu
            var = var + jnp.sum(d * d, axis=-1, keepdims=True)
        mu_ref[...] = mu
        rstd_ref[...] = lax.rsqrt(var * (1.0 / E_MIX) + EPS)

    @pl.when(j >= _GMLP_PROJ_STEPS)
    def _():
        is_last_tile = i == N_TILE - 1
        row = lax.broadcasted_iota(jnp.int32, (CHUNK, CHUNK), 0)
        col = lax.broadcasted_iota(jnp.int32, (CHUNK, CHUNK), 1)
        mu = mu_ref[...]
        rstd = rstd_ref[...]
        for hh in range(_GMLP_HEADS_PER_MIX):
            head = (j - _GMLP_PROJ_STEPS) * _GMLP_HEADS_PER_MIX + hh
            lanes = slice(hh * D_HEAD, (hh + 1) * D_HEAD)
            vn = (v_ref[head] - mu) * rstd * lng_ref[head] + lnb_ref[head]

            @pl.when(is_last_tile)
            def _():
                vo_ref[head] = vn[SAMPLE_ROW0:, :]

            w = ws_ref[head]
            b = bs_ref[head]
            w_causal = jnp.where(col <= row, w, 0.0)
            w_sample = jnp.where(col == row, w[0:1, 0:1], 0.0)
            b_sample = jnp.broadcast_to(b[0:1, :], (CHUNK, 1))
            w_last = jnp.where(is_last_tile, w_sample, w_causal).astype(BF16)
            b_last = jnp.where(is_last_tile, b_sample, b)
            w_causal = w_causal.astype(BF16)
            n_chunk = TM // CHUNK
            for c in range(n_chunk):
                rows = slice(c * CHUNK, (c + 1) * CHUNK)
                wm, bm = (w_last, b_last) if c == n_chunk - 1 else (w_causal, b)
                s = _dot(wm, vn[rows, :].astype(BF16)) + bm
                y_ref[rows, lanes] = (u_ref[head, rows, :] * s).astype(BF16)
        o_ref[...] += _dot(y_ref[...], wo_ref[...])


def _gmlp(h, gain, w_in, ln_g, ln_b, w_s, b_s, w_out, layer, jm):
    proj_last = _GMLP_PROJ_STEPS - 1
    mix_w = _GMLP_HEADS_PER_MIX * D_HEAD

    def mix_step(j):
        return jnp.maximum(j - _GMLP_PROJ_STEPS, 0)

    return pl.pallas_call(
        _gmlp_kernel,
        grid=(N_TILE, _GMLP_STEPS),
        in_specs=[
            pl.BlockSpec((TM, D_MODEL), lambda i, j: (i, 0)),
            pl.BlockSpec((None, 1, D_MODEL), lambda i, j: (layer, 0, 0)),
            pl.BlockSpec((None, D_MODEL, D_HEAD), lambda i, j: (jm, 0, jnp.minimum(j, proj_last))),
            pl.BlockSpec((None, D_MODEL, D_HEAD),
                         lambda i, j: (jm, 0, N_HEAD + jnp.minimum(j, proj_last))),
            pl.BlockSpec((None, N_HEAD, 1, D_HEAD), lambda i, j: (jm, 0, 0, 0)),
            pl.BlockSpec((None, N_HEAD, 1, D_HEAD), lambda i, j: (jm, 0, 0, 0)),
            pl.BlockSpec((None, N_HEAD, CHUNK, CHUNK), lambda i, j: (jm, 0, 0, 0)),
            pl.BlockSpec((None, N_HEAD, CHUNK, 1), lambda i, j: (jm, 0, 0, 0)),
            pl.BlockSpec((None, mix_w, D_MODEL), lambda i, j: (jm, mix_step(j), 0)),
        ],
        out_specs=[
            pl.BlockSpec((TM, D_MODEL), lambda i, j: (i, 0)),
            pl.BlockSpec((N_HEAD, N_SAMPLE, D_HEAD), lambda i, j: (0, 0, 0)),
        ],
        out_shape=[
            jax.ShapeDtypeStruct((N_TOK, D_MODEL), F32),
            jax.ShapeDtypeStruct((N_HEAD, N_SAMPLE, D_HEAD), F32),
        ],
        scratch_shapes=[
            pltpu.VMEM((TM, D_MODEL), BF16),
            pltpu.VMEM((N_HEAD, TM, D_HEAD), F32),
            pltpu.VMEM((N_HEAD, TM, D_HEAD), F32),
            pltpu.VMEM((TM, 1), F32),
            pltpu.VMEM((TM, 1), F32),
            pltpu.VMEM((TM, 1), F32),
            pltpu.VMEM((TM, mix_w), BF16),
        ],
        compiler_params=_ARB2,
        name="gmlp",
    )(h, gain, w_in, w_in, ln_g, ln_b, w_s, b_s, w_out)


_HALO = 8


_SEQ_TAIL = [divmod((s + 1) * SEQ - _HALO, TM) for s in range(N_PROMPT // SEQ)]


def _conv_kernel(h_ref, g_ref, wb_ref, wc_ref, wx_ref, wk_ref, st0_ref, st1_ref, wo_ref,
                 o_ref, cs_ref, cp_ref,
                 xn_ref, buf_ref, carry_ref, y_ref):
    i = pl.program_id(0)
    j = pl.program_id(1)

    @pl.when(j == 0)
    def _():
        x = h_ref[...]
        xn_ref[...] = _rms_bf16(x, g_ref[...])
        o_ref[...] = x

    @pl.when(i == 0)
    def _():
        carry_ref[j] = jnp.zeros((_HALO, TN), F32)

    xn = xn_ref[...]
    bg = _dot(xn, wb_ref[...])
    ci = _dot(xn, wc_ref[...]) * _dot(xn, wx_ref[...])

    for s, (tile, row0) in enumerate(_SEQ_TAIL):
        @pl.when(i == tile)
        def _():
            cp_ref[s, j] = ci[row0:row0 + _HALO, :]

    buf_ref[0:_HALO, :] = carry_ref[j]
    buf_ref[_HALO:, :] = ci
    carry_ref[j] = ci[TM - _HALO:, :]
    pos = (lax.broadcasted_iota(jnp.int32, (TM, 1), 0) + i * TM) & (SEQ - 1)
    prev1 = jnp.where(pos >= 1, buf_ref[_HALO - 1:_HALO - 1 + TM, :], 0.0)
    prev2 = jnp.where(pos >= 2, buf_ref[_HALO - 2:_HALO - 2 + TM, :], 0.0)
    w0 = wk_ref[0:1, :]
    w1 = wk_ref[1:2, :]
    w2 = wk_ref[2:3, :]
    co = w0 * prev2 + w1 * prev1 + w2 * ci
    y_ref[...] = (bg * co).astype(BF16)

    @pl.when(i == N_TILE - 1)
    def _():
        co_s = w0 * st0_ref[...] + w1 * st1_ref[...] + w2 * ci[SAMPLE_ROW0:, :]
        y_ref[SAMPLE_ROW0:, :] = (bg[SAMPLE_ROW0:, :] * co_s).astype(BF16)
        cs_ref[j] = ci[SAMPLE_ROW0:, :]

    o_ref[...] += _dot(y_ref[...], wo_ref[...])


def _conv(h, gain, w_in, w_conv, st0, st1, w_out, layer, jm):
    nb = E_MIX // TN
    n_seq = N_PROMPT // SEQ
    return pl.pallas_call(
        _conv_kernel,
        grid=(N_TILE, nb),
        in_specs=[
            pl.BlockSpec((TM, D_MODEL), lambda i, j: (i, 0)),
            pl.BlockSpec((None, 1, D_MODEL), lambda i, j: (layer, 0, 0)),
            pl.BlockSpec((None, D_MODEL, TN), lambda i, j: (jm, 0, j)),
            pl.BlockSpec((None, D_MODEL, TN), lambda i, j: (jm, 0, nb + j)),
            pl.BlockSpec((None, D_MODEL, TN), lambda i, j: (jm, 0, 2 * nb + j)),
            pl.BlockSpec((None, 3, TN), lambda i, j: (jm, 0, j)),
            pl.BlockSpec((N_SAMPLE, TN), lambda i, j: (0, j)),
            pl.BlockSpec((N_SAMPLE, TN), lambda i, j: (0, j)),
            pl.BlockSpec((None, TN, D_MODEL), lambda i, j: (jm, j, 0)),
        ],
        out_specs=[
            pl.BlockSpec((TM, D_MODEL), lambda i, j: (i, 0)),
            pl.BlockSpec((nb, N_SAMPLE, TN), lambda i, j: (0, 0, 0)),
            pl.BlockSpec((n_seq, nb, _HALO, TN), lambda i, j: (0, 0, 0, 0)),
        ],
        out_shape=[
            jax.ShapeDtypeStruct((N_TOK, D_MODEL), F32),
            jax.ShapeDtypeStruct((nb, N_SAMPLE, TN), F32),
            jax.ShapeDtypeStruct((n_seq, nb, _HALO, TN), F32),
        ],
        scratch_shapes=[
            pltpu.VMEM((TM, D_MODEL), BF16),
            pltpu.VMEM((TM + _HALO, TN), F32),
            pltpu.VMEM((nb, _HALO, TN), F32),
            pltpu.VMEM((TM, TN), BF16),
        ],
        compiler_params=_ARB2,
        name="sconv",
    )(h, gain, w_in, w_in, w_in, w_conv, st0, st1, w_out)


def _ple_kernel(h_ref, g_ref, wg_ref, pp_ref, ps_ref, wp_ref, gf_ref, o_ref, *rest, final_norm):
    if final_norm:
        os_ref, pb_ref = rest
    else:
        (pb_ref,) = rest
    i = pl.program_id(0)

    @pl.when(i < N_TILE - 1)
    def _():
        pb_ref[...] = pp_ref[...].astype(BF16)

    @pl.when(i == N_TILE - 1)
    def _():
        pb_ref[0:SAMPLE_ROW0, :] = pp_ref[0:SAMPLE_ROW0, :].astype(BF16)
        pb_ref[SAMPLE_ROW0:, :] = ps_ref[...].astype(BF16)

    xn = _rms_bf16(h_ref[...], g_ref[...])
    pb = pb_ref[...]
    for c in range(D_MODEL // TN):
        cols = slice(c * TN, (c + 1) * TN)
        gate = jax.nn.sigmoid(_dot(xn, wg_ref[:, cols]))
        o_ref[:, cols] = h_ref[:, cols] + gate * _dot(pb, wp_ref[:, cols])
    if final_norm:
        hn = o_ref[...]
        ms = jnp.mean(hn * hn, axis=-1, keepdims=True)
        o_ref[...] = hn * lax.rsqrt(ms + EPS) * gf_ref[...]

        @pl.when(i == N_TILE - 1)
        def _():
            os_ref[...] = o_ref[SAMPLE_ROW0:, :]


def _ple(h, gain, w_gate, p_prompt, p_sample, w_proj, g_final, layer, final_norm):
    row_block = pl.BlockSpec((TM, D_MODEL), lambda i: (i, 0))
    if final_norm:
        out_specs = [row_block, pl.BlockSpec((N_SAMPLE, D_MODEL), lambda i: (0, 0))]
        out_shape = [jax.ShapeDtypeStruct((N_PROMPT, D_MODEL), F32),
                     jax.ShapeDtypeStruct((N_SAMPLE, D_MODEL), F32)]
    else:
        out_specs = row_block
        out_shape = jax.ShapeDtypeStruct((N_TOK, D_MODEL), F32)
    return pl.pallas_call(
        functools.partial(_ple_kernel, final_norm=final_norm),
        grid=(N_TILE,),
        in_specs=[
            row_block,
            pl.BlockSpec((None, 1, D_MODEL), lambda i: (layer, 0, 0)),
            pl.BlockSpec((None, D_MODEL, D_MODEL), lambda i: (layer, 0, 0)),
            pl.BlockSpec((None, TM, D_PLE), lambda i: (layer, i, 0)),
            pl.BlockSpec((None, N_SAMPLE, D_PLE), lambda i: (layer, 0, 0)),
            pl.BlockSpec((None, D_PLE, D_MODEL), lambda i: (layer, 0, 0)),
            pl.BlockSpec((1, D_MODEL), lambda i: (0, 0)),
        ],
        out_specs=out_specs,
        out_shape=out_shape,
        scratch_shapes=[pltpu.VMEM((TM, D_PLE), BF16)],
        compiler_params=_ARB1,
        name="ple",
    )(h, gain, w_gate, p_prompt, p_sample, w_proj, g_final)


def kernel(x_prompt, x_sample, state_conv, p_prompt, p_sample, ffn1_norm, ffn1_w_gate, ffn1_w_up, ffn1_w_down, mix_norm, a_w_in, a_ln_g, a_ln_b, a_w_s, a_b_s, a_w_out, c_w_in, c_w_conv, c_w_out, ffn2_norm, ffn2_w_gate, ffn2_w_up, ffn2_w_down, ple_norm, ple_w_gate, ple_w_proj, final_norm):
    bf = lambda w: w.astype(BF16)
    gain3 = lambda g: g.reshape(g.shape[0], 1, D_MODEL)

    h = (x_prompt.reshape(N_PROMPT, D_MODEL), x_sample.reshape(N_SAMPLE, D_MODEL))
    p_prompt = p_prompt.reshape(DEPTH, N_PROMPT, D_PLE)
    p_sample = p_sample.reshape(DEPTH, N_SAMPLE, D_PLE)

    f1 = (gain3(ffn1_norm), ffn1_w_gate, ffn1_w_up, ffn1_w_down)
    f2 = (gain3(ffn2_norm), ffn2_w_gate, ffn2_w_up, ffn2_w_down)
    mix_gain = gain3(mix_norm)
    ple_gain = gain3(ple_norm)
    a_w_in_b, a_w_out_b = bf(a_w_in), bf(a_w_out)
    c_w_in_b, c_w_out_b = bf(c_w_in), bf(c_w_out)
    ple_w_gate_b, ple_w_proj_b = bf(ple_w_gate), bf(ple_w_proj)
    ln_g = a_ln_g.reshape(-1, N_HEAD, 1, D_HEAD)
    ln_b = a_ln_b.reshape(-1, N_HEAD, 1, D_HEAD)
    b_s = a_b_s.reshape(-1, N_HEAD, CHUNK, 1)
    g_final = final_norm.reshape(1, D_MODEL)

    conv_new_prompt, conv_new_sample, v_new = [], [], []
    for layer in range(DEPTH):
        h = _ffn(h, *f1, layer)
        jm = layer // 2
        if layer % 2 == 0:
            h, v_s = _gmlp(h, mix_gain, a_w_in_b, ln_g, ln_b, a_w_s, b_s, a_w_out_b, layer, jm)
            v_new.append(v_s.transpose(1, 0, 2).reshape(N_SAMPLE, 1, E_MIX))
        else:
            st0 = state_conv[jm, :, 0, :]
            st1 = state_conv[jm, :, 1, :]
            h, ci_s, ci_p = _conv(h, mix_gain, c_w_in_b, c_w_conv, st0, st1, c_w_out_b, layer, jm)
            ci_s = ci_s.transpose(1, 0, 2).reshape(N_SAMPLE, E_MIX)
            ci_p = ci_p.transpose(0, 2, 1, 3).reshape(N_PROMPT // SEQ, _HALO, E_MIX)
            conv_new_prompt.append(ci_p[:, _HALO - 2:, :])
            conv_new_sample.append(jnp.stack([st1, ci_s], axis=1))
        h = _ffn(h, *f2, layer)
        h = _ple(h, ple_gain, ple_w_gate_b, p_prompt, p_sample, ple_w_proj_b, g_final, layer,
                 final_norm=(layer == DEPTH - 1))

    y_prompt, y_sample = h
    return (y_prompt.reshape(N_PROMPT // SEQ, SEQ, D_MODEL), y_sample.reshape(N_SAMPLE, 1, D_MODEL),
            jnp.stack(conv_new_prompt), jnp.stack(conv_new_sample), jnp.stack(v_new))
```

```python
import functools

import jax
import jax.numpy as jnp
from jax import lax
from jax.experimental import pallas as pl
from jax.experimental.pallas import tpu as pltpu

F32 = jnp.float32
BF16 = jnp.bfloat16

D_MODEL = 2048
D_FF = 5632
E_MIX = 2048
N_HEAD = 8
D_HEAD = 256
CHUNK = 128
D_PLE = 256
SEQ = 2048
N_PROMPT = 4 * SEQ
N_SAMPLE = 128
N_TOK = N_PROMPT + N_SAMPLE
DEPTH = 2
EPS = 1e-6

TM = 640
N_TILE = N_TOK // TM
SAMPLE_ROW0 = TM - N_SAMPLE
TM_FFN = 1040
N_TILE_FFN = N_TOK // TM_FFN
SAMPLE_ROW0_FFN = TM_FFN - N_SAMPLE
TF = 256
TN = 512
V7X_VMEM_BYTES = 64 * 1024 * 1024
VMEM_LIMIT = V7X_VMEM_BYTES - 2 * 1024 * 1024

_ARB1 = pltpu.CompilerParams(dimension_semantics=("arbitrary",),
                             vmem_limit_bytes=VMEM_LIMIT)


def _rms_bf16(x, g):
    ms = jnp.mean(x * x, axis=-1, keepdims=True)
    return (x * lax.rsqrt(ms + EPS) * g).astype(BF16)


def _dot(a, b):
    return jnp.dot(a, b, preferred_element_type=F32)


def _gelu_tanh(x):
    c = 0.7978845608028654
    return x * (0.5 * (1.0 + jnp.tanh(c * (x + 0.044715 * (x * x * x)))))


N_FSTEP = D_FF // TF
N_STEP_FFN = N_TILE_FFN * N_FSTEP
W_RING = 3
W_AHEAD = W_RING - 1


def _ffn_weight_copies(layer, step, w_hbm, w_buf, sem):
    wg_hbm, wu_hbm, wd_hbm = w_hbm
    wg_buf, wu_buf, wd_buf = w_buf
    slot = step % W_RING
    cols = pl.ds(pl.multiple_of((step % N_FSTEP) * TF, TF), TF)
    return (
        pltpu.make_async_copy(wg_hbm.at[layer, :, cols], wg_buf.at[slot], sem.at[0, slot]),
        pltpu.make_async_copy(wu_hbm.at[layer, :, cols], wu_buf.at[slot], sem.at[1, slot]),
        pltpu.make_async_copy(wd_hbm.at[layer, cols, :], wd_buf.at[slot], sem.at[2, slot]),
    )


def _ffn_start(x, g_ref, o_ref, xn_ref):
    o_ref[...] = x
    xn_ref[...] = _rms_bf16(x, g_ref[...])


def _ffn_blocks(layer, w_hbm, w_buf, sem, o_ref, xn_ref):
    tile = pl.program_id(0)
    copies = functools.partial(_ffn_weight_copies, layer, w_hbm=w_hbm, w_buf=w_buf, sem=sem)
    wg_buf, wu_buf, wd_buf = w_buf

    @pl.when(tile == 0)
    def _():
        for ahead in range(W_AHEAD):
            for c in copies(ahead):
                c.start()

    def prefetch(step):
        for c in copies(step):
            c.start()

    def block(step):
        slot = step % W_RING
        xn = xn_ref[...]
        gt = _dot(xn, wg_buf[slot].astype(BF16))
        up = _dot(xn, wu_buf[slot].astype(BF16))
        hd = (gt * jax.nn.sigmoid(gt)) * up * 0.5
        o_ref[...] += _dot(hd.astype(BF16), wd_buf[slot].astype(BF16))

    def block_pair(jp, carry):
        step = tile * N_FSTEP + 2 * jp
        prefetch(step + W_AHEAD)
        for c in copies(step) + copies(step + 1):
            c.wait()
        block(step)
        prefetch(step + 1 + W_AHEAD)
        block(step + 1)
        return carry

    lax.fori_loop(0, N_FSTEP // 2, block_pair, 0)

    @pl.when(tile == N_TILE_FFN - 1)
    def _():
        for ahead in range(W_AHEAD):
            for c in copies(N_STEP_FFN + ahead):
                c.wait()


def _ffn_kernel(h_ref, g_ref, wg_hbm, wu_hbm, wd_hbm, o_ref,
                xn_ref, wg_buf, wu_buf, wd_buf, sem, *, layer):
    _ffn_start(h_ref[...], g_ref, o_ref, xn_ref)
    _ffn_blocks(layer, (wg_hbm, wu_hbm, wd_hbm), (wg_buf, wu_buf, wd_buf), sem, o_ref, xn_ref)


def _ffn_split_kernel(xp_ref, xs_ref, g_ref, wg_hbm, wu_hbm, wd_hbm, o_ref,
                      xn_ref, wg_buf, wu_buf, wd_buf, sem, *, layer):
    tile = pl.program_id(0)

    @pl.when(tile < N_TILE_FFN - 1)
    def _():
        _ffn_start(xp_ref[...], g_ref, o_ref, xn_ref)

    @pl.when(tile == N_TILE_FFN - 1)
    def _():
        x = jnp.concatenate([xp_ref[0:SAMPLE_ROW0_FFN, :], xs_ref[...]], axis=0)
        _ffn_start(x, g_ref, o_ref, xn_ref)

    _ffn_blocks(layer, (wg_hbm, wu_hbm, wd_hbm), (wg_buf, wu_buf, wd_buf), sem, o_ref, xn_ref)


def _ffn(h, gain, w_gate, w_up, w_down, layer):
    split = isinstance(h, tuple)
    row_specs = [pl.BlockSpec((TM_FFN, D_MODEL), lambda i: (i, 0))]
    if split:
        row_specs.append(pl.BlockSpec((N_SAMPLE, D_MODEL), lambda i: (0, 0)))
    hbm = pl.BlockSpec(memory_space=pl.ANY)
    return pl.pallas_call(
        functools.partial(_ffn_split_kernel if split else _ffn_kernel, layer=layer),
        grid=(N_TILE_FFN,),
        in_specs=row_specs + [
            pl.BlockSpec((None, 1, D_MODEL), lambda i: (layer, 0, 0)), hbm, hbm, hbm],
        out_specs=pl.BlockSpec((TM_FFN, D_MODEL), lambda i: (i, 0)),
        out_shape=jax.ShapeDtypeStruct((N_TOK, D_MODEL), F32),
        scratch_shapes=[
            pltpu.VMEM((TM_FFN, D_MODEL), BF16),
            pltpu.VMEM((W_RING, D_MODEL, TF), F32),
            pltpu.VMEM((W_RING, D_MODEL, TF), F32),
            pltpu.VMEM((W_RING, TF, D_MODEL), F32),
            pltpu.SemaphoreType.DMA((3, W_RING)),
        ],
        compiler_params=_ARB1,
        name="ffn",
    )(*(h if split else (h,)), gain, w_gate, w_up, w_down)


def _gmlp_weight_copies(jm, win_hbm, wout_hbm, win_buf, wout_buf, sem):
    return (pltpu.make_async_copy(win_hbm.at[jm], win_buf, sem.at[0]),
            pltpu.make_async_copy(wout_hbm.at[jm], wout_buf, sem.at[1]))


def _gmlp_kernel(h_ref, g_ref, win_hbm, wout_hbm, lng_ref, lnb_ref, ws_ref, bs_ref,
                 o_ref, vo_ref,
                 win_buf, wout_buf, xn_ref, v_ref, y_ref, sem, *, jm):
    tile = pl.program_id(0)
    is_last_tile = tile == N_TILE - 1
    copies = _gmlp_weight_copies(jm, win_hbm, wout_hbm, win_buf, wout_buf, sem)

    @pl.when(tile == 0)
    def _():
        for c in copies:
            c.start()

    xn_ref[...] = _rms_bf16(h_ref[...], g_ref[...])

    @pl.when(tile == 0)
    def _():
        for c in copies:
            c.wait()

    xn = xn_ref[...]

    def head_cols(first, head):
        return slice(first + head * D_HEAD, first + (head + 1) * D_HEAD)

    v_sum = jnp.zeros((TM, 1), F32)
    for head in range(N_HEAD):
        v = _gelu_tanh(_dot(xn, win_buf[:, head_cols(E_MIX, head)]))
        v_ref[head] = v
        v_sum = v_sum + jnp.sum(v, axis=-1, keepdims=True)
    mu = v_sum * (1.0 / E_MIX)
    var = jnp.zeros((TM, 1), F32)
    for head in range(N_HEAD):
        d = v_ref[head] - mu
        var = var + jnp.sum(d * d, axis=-1, keepdims=True)
    rstd = lax.rsqrt(var * (1.0 / E_MIX) + EPS)

    row = lax.broadcasted_iota(jnp.int32, (CHUNK, CHUNK), 0)
    col = lax.broadcasted_iota(jnp.int32, (CHUNK, CHUNK), 1)
    n_chunk = TM // CHUNK
    for head in range(N_HEAD):
        u = _gelu_tanh(_dot(xn, win_buf[:, head_cols(0, head)]))
        vn = (v_ref[head] - mu) * rstd * lng_ref[head] + lnb_ref[head]

        vo_ref[head] = vn[SAMPLE_ROW0:, :]
        w = ws_ref[head]
        b = bs_ref[head]
        w_causal = jnp.where(col <= row, w, 0.0)
        w_sample = jnp.where(col == row, w[0:1, 0:1], 0.0)
        b_sample = jnp.broadcast_to(b[0:1, :], (CHUNK, 1))
        w_last = jnp.where(is_last_tile, w_sample, w_causal).astype(BF16)
        b_last = jnp.where(is_last_tile, b_sample, b)
        w_causal = w_causal.astype(BF16)
        for c in range(n_chunk):
            rows = slice(c * CHUNK, (c + 1) * CHUNK)
            wm, bm = (w_last, b_last) if c == n_chunk - 1 else (w_causal, b)
            s = _dot(wm, vn[rows, :].astype(BF16)) + bm
            y_ref[rows, head_cols(0, head)] = (u[rows, :] * s).astype(BF16)

    o_ref[...] = h_ref[...] + _dot(y_ref[...], wout_buf[...])


def _gmlp(h, gain, w_in, ln_g, ln_b, w_s, b_s, w_out, layer, jm):
    hbm = pl.BlockSpec(memory_space=pl.ANY)
    return pl.pallas_call(
        functools.partial(_gmlp_kernel, jm=jm),
        grid=(N_TILE,),
        in_specs=[
            pl.BlockSpec((TM, D_MODEL), lambda i: (i, 0)),
            pl.BlockSpec((None, 1, D_MODEL), lambda i: (layer, 0, 0)),
            hbm, hbm,
            pl.BlockSpec((None, N_HEAD, 1, D_HEAD), lambda i: (jm, 0, 0, 0)),
            pl.BlockSpec((None, N_HEAD, 1, D_HEAD), lambda i: (jm, 0, 0, 0)),
            pl.BlockSpec((None, N_HEAD, CHUNK, CHUNK), lambda i: (jm, 0, 0, 0)),
            pl.BlockSpec((None, N_HEAD, CHUNK, 1), lambda i: (jm, 0, 0, 0)),
        ],
        out_specs=[
            pl.BlockSpec((TM, D_MODEL), lambda i: (i, 0)),
            pl.BlockSpec((N_HEAD, N_SAMPLE, D_HEAD), lambda i: (0, 0, 0)),
        ],
        out_shape=[
            jax.ShapeDtypeStruct((N_TOK, D_MODEL), F32),
            jax.ShapeDtypeStruct((N_HEAD, N_SAMPLE, D_HEAD), F32),
        ],
        scratch_shapes=[
            pltpu.VMEM((D_MODEL, 2 * E_MIX), BF16),
            pltpu.VMEM((E_MIX, D_MODEL), BF16),
            pltpu.VMEM((TM, D_MODEL), BF16),
            pltpu.VMEM((N_HEAD, TM, D_HEAD), F32),
            pltpu.VMEM((TM, E_MIX), BF16),
            pltpu.SemaphoreType.DMA((2,)),
        ],
        compiler_params=_ARB1,
        name="gmlp",
    )(h, gain, w_in, w_out, ln_g, ln_b, w_s, b_s)


_HALO = 8
CB = 256
N_CB = E_MIX // CB
CONV_PAIRS = N_CB // 2
CONV_SLOTS = 4
assert CONV_PAIRS % 2 == 0

_SEQ_TAIL = [divmod((s + 1) * SEQ - _HALO, TM) for s in range(N_PROMPT // SEQ)]


def _conv_kernel(h_ref, g_ref, win_hbm, wout_hbm, wk_ref, st0_ref, st1_ref,
                 o_ref, cs_ref, cp_ref,
                 win_buf, wout_buf, xn_ref, buf_ref, carry_ref, tails_ref, y_ref,
                 sem_in, sem_out, *, jm):
    tile = pl.program_id(0)
    is_last_tile = tile == N_TILE - 1

    def win_copy(block, slot):
        return pltpu.make_async_copy(win_hbm.at[jm, block], win_buf.at[slot], sem_in.at[slot])

    wout_copy = pltpu.make_async_copy(wout_hbm.at[jm], wout_buf, sem_out.at[0])

    @pl.when(tile == 0)
    def _():
        wout_copy.start()
        win_copy(0, 0).start()
        win_copy(1, 1).start()
        carry_ref[...] = jnp.zeros_like(carry_ref)

    xn_ref[...] = _rms_bf16(h_ref[...], g_ref[...])
    xn = xn_ref[...]
    pos = (lax.broadcasted_iota(jnp.int32, (TM, 1), 0) + tile * TM) & (SEQ - 1)

    def conv_block(block, slot):
        cols = slice(block * CB, (block + 1) * CB)
        bg = _dot(xn, win_buf[slot, 0])
        ci = _dot(xn, win_buf[slot, 1]) * _dot(xn, win_buf[slot, 2])

        for s, (_, row0) in enumerate(_SEQ_TAIL):
            tails_ref[s, block] = ci[row0:row0 + _HALO, :]
        ci_s = ci[SAMPLE_ROW0:, :]
        cs_ref[block] = ci_s

        buf = buf_ref.at[block % 2]
        buf[0:_HALO, :] = carry_ref[block]
        buf[_HALO:, :] = ci
        carry_ref[block] = ci[TM - _HALO:, :]
        prev1 = jnp.where(pos >= 1, buf[_HALO - 1:_HALO - 1 + TM, :], 0.0)
        prev2 = jnp.where(pos >= 2, buf[_HALO - 2:_HALO - 2 + TM, :], 0.0)
        w0 = wk_ref[0:1, cols]
        w1 = wk_ref[1:2, cols]
        w2 = wk_ref[2:3, cols]
        co = w0 * prev2 + w1 * prev1 + w2 * ci
        co_s = w0 * st0_ref[:, cols] + w1 * st1_ref[:, cols] + w2 * ci_s
        co_tail = jnp.where(is_last_tile, co_s, co[SAMPLE_ROW0:, :])
        y_ref[0:SAMPLE_ROW0, cols] = (bg[0:SAMPLE_ROW0, :] * co[0:SAMPLE_ROW0, :]).astype(BF16)
        y_ref[SAMPLE_ROW0:, cols] = (bg[SAMPLE_ROW0:, :] * co_tail).astype(BF16)

    for pair in range(CONV_PAIRS):
        blocks = (2 * pair, 2 * pair + 1)
        slots = (0, 1) if pair % 2 == 0 else (2, 3)
        free_slots = (2, 3) if pair % 2 == 0 else (0, 1)
        for b, slot in zip(blocks, free_slots):
            win_copy((b + 2) % N_CB, slot).start()
        for b, slot in zip(blocks, slots):
            win_copy(b, slot).wait()
        for b, slot in zip(blocks, slots):
            conv_block(b, slot)

    @pl.when(tile == 0)
    def _():
        wout_copy.wait()

    o_ref[...] = h_ref[...] + _dot(y_ref[...], wout_buf[...])

    for s, (seq_tile, _) in enumerate(_SEQ_TAIL):
        @pl.when(tile == seq_tile)
        def _():
            cp_ref[s] = tails_ref[s]

    @pl.when(is_last_tile)
    def _():
        for b in (0, 1):
            win_copy(b, b).wait()


def _conv(h, gain, w_in_blocked, w_conv, st0, st1, w_out, layer, jm):
    n_seq = N_PROMPT // SEQ
    hbm = pl.BlockSpec(memory_space=pl.ANY)
    return pl.pallas_call(
        functools.partial(_conv_kernel, jm=jm),
        grid=(N_TILE,),
        in_specs=[
            pl.BlockSpec((TM, D_MODEL), lambda i: (i, 0)),
            pl.BlockSpec((None, 1, D_MODEL), lambda i: (layer, 0, 0)),
            hbm, hbm,
            pl.BlockSpec((None, 3, E_MIX), lambda i: (jm, 0, 0)),
            pl.BlockSpec((N_SAMPLE, E_MIX), lambda i: (0, 0)),
            pl.BlockSpec((N_SAMPLE, E_MIX), lambda i: (0, 0)),
        ],
        out_specs=[
            pl.BlockSpec((TM, D_MODEL), lambda i: (i, 0)),
            pl.BlockSpec((N_CB, N_SAMPLE, CB), lambda i: (0, 0, 0)),
            pl.BlockSpec((n_seq, N_CB, _HALO, CB), lambda i: (0, 0, 0, 0)),
        ],
        out_shape=[
            jax.ShapeDtypeStruct((N_TOK, D_MODEL), F32),
            jax.ShapeDtypeStruct((N_CB, N_SAMPLE, CB), F32),
            jax.ShapeDtypeStruct((n_seq, N_CB, _HALO, CB), F32),
        ],
        scratch_shapes=[
            pltpu.VMEM((CONV_SLOTS, 3, D_MODEL, CB), BF16),
            pltpu.VMEM((E_MIX, D_MODEL), BF16),
            pltpu.VMEM((TM, D_MODEL), BF16),
            pltpu.VMEM((2, TM + _HALO, CB), F32),
            pltpu.VMEM((N_CB, _HALO, CB), F32),
            pltpu.VMEM((n_seq, N_CB, _HALO, CB), F32),
            pltpu.VMEM((TM, E_MIX), BF16),
            pltpu.SemaphoreType.DMA((CONV_SLOTS,)),
            pltpu.SemaphoreType.DMA((1,)),
        ],
        compiler_params=_ARB1,
        name="sconv",
    )(h, gain, w_in_blocked, w_out, w_conv, st0, st1)


def _ple_kernel(h_ref, g_ref, wg_ref, pp_ref, ps_ref, wp_ref, gf_ref, o_ref, *rest, final_norm):
    if final_norm:
        os_ref, pb_ref = rest
    else:
        (pb_ref,) = rest
    i = pl.program_id(0)

    @pl.when(i < N_TILE - 1)
    def _():
        pb_ref[...] = pp_ref[...].astype(BF16)

    @pl.when(i == N_TILE - 1)
    def _():
        pb_ref[0:SAMPLE_ROW0, :] = pp_ref[0:SAMPLE_ROW0, :].astype(BF16)
        pb_ref[SAMPLE_ROW0:, :] = ps_ref[...].astype(BF16)

    xn = _rms_bf16(h_ref[...], g_ref[...])
    pb = pb_ref[...]
    for c in range(D_MODEL // TN):
        cols = slice(c * TN, (c + 1) * TN)
        gate = jax.nn.sigmoid(_dot(xn, wg_ref[:, cols]))
        o_ref[:, cols] = h_ref[:, cols] + gate * _dot(pb, wp_ref[:, cols])
    if final_norm:
        hn = o_ref[...]
        ms = jnp.mean(hn * hn, axis=-1, keepdims=True)
        o_ref[...] = hn * lax.rsqrt(ms + EPS) * gf_ref[...]

        @pl.when(i == N_TILE - 1)
        def _():
            os_ref[...] = o_ref[SAMPLE_ROW0:, :]


def _ple(h, gain, w_gate, p_prompt, p_sample, w_proj, g_final, layer, final_norm):
    row_block = pl.BlockSpec((TM, D_MODEL), lambda i: (i, 0))
    if final_norm:
        out_specs = [row_block, pl.BlockSpec((N_SAMPLE, D_MODEL), lambda i: (0, 0))]
        out_shape = [jax.ShapeDtypeStruct((N_PROMPT, D_MODEL), F32),
                     jax.ShapeDtypeStruct((N_SAMPLE, D_MODEL), F32)]
    else:
        out_specs = row_block
        out_shape = jax.ShapeDtypeStruct((N_TOK, D_MODEL), F32)
    return pl.pallas_call(
        functools.partial(_ple_kernel, final_norm=final_norm),
        grid=(N_TILE,),
        in_specs=[
            row_block,
            pl.BlockSpec((None, 1, D_MODEL), lambda i: (layer, 0, 0)),
            pl.BlockSpec((None, D_MODEL, D_MODEL), lambda i: (layer, 0, 0)),
            pl.BlockSpec((None, TM, D_PLE), lambda i: (layer, i, 0)),
            pl.BlockSpec((None, N_SAMPLE, D_PLE), lambda i: (layer, 0, 0)),
            pl.BlockSpec((None, D_PLE, D_MODEL), lambda i: (layer, 0, 0)),
            pl.BlockSpec((1, D_MODEL), lambda i: (0, 0)),
        ],
        out_specs=out_specs,
        out_shape=out_shape,
        scratch_shapes=[pltpu.VMEM((TM, D_PLE), BF16)],
        compiler_params=_ARB1,
        name="ple",
    )(h, gain, w_gate, p_prompt, p_sample, w_proj, g_final)


def kernel(x_prompt, x_sample, state_conv, p_prompt, p_sample, ffn1_norm, ffn1_w_gate, ffn1_w_up, ffn1_w_down, mix_norm, a_w_in, a_ln_g, a_ln_b, a_w_s, a_b_s, a_w_out, c_w_in, c_w_conv, c_w_out, ffn2_norm, ffn2_w_gate, ffn2_w_up, ffn2_w_down, ple_norm, ple_w_gate, ple_w_proj, final_norm):
    bf = lambda w: w.astype(BF16)
    gain3 = lambda g: g.reshape(g.shape[0], 1, D_MODEL)

    h = (x_prompt.reshape(N_PROMPT, D_MODEL), x_sample.reshape(N_SAMPLE, D_MODEL))
    p_prompt = p_prompt.reshape(DEPTH, N_PROMPT, D_PLE)
    p_sample = p_sample.reshape(DEPTH, N_SAMPLE, D_PLE)

    f1 = (gain3(ffn1_norm), ffn1_w_gate, ffn1_w_up, ffn1_w_down)
    f2 = (gain3(ffn2_norm), ffn2_w_gate, ffn2_w_up, ffn2_w_down)
    mix_gain = gain3(mix_norm)
    ple_gain = gain3(ple_norm)
    a_w_in_b, a_w_out_b = bf(a_w_in), bf(a_w_out)
    c_w_in_b = bf(c_w_in.reshape(-1, D_MODEL, 3, N_CB, CB).transpose(0, 3, 2, 1, 4))
    c_w_out_b = bf(c_w_out)
    ple_w_gate_b, ple_w_proj_b = bf(ple_w_gate), bf(ple_w_proj)
    ln_g = a_ln_g.reshape(-1, N_HEAD, 1, D_HEAD)
    ln_b = a_ln_b.reshape(-1, N_HEAD, 1, D_HEAD)
    b_s = a_b_s.reshape(-1, N_HEAD, CHUNK, 1)
    g_final = final_norm.reshape(1, D_MODEL)

    conv_new_prompt, conv_new_sample, v_new = [], [], []
    for layer in range(DEPTH):
        h = _ffn(h, *f1, layer)
        jm = layer // 2
        if layer % 2 == 0:
            h, v_s = _gmlp(h, mix_gain, a_w_in_b, ln_g, ln_b, a_w_s, b_s, a_w_out_b, layer, jm)
            v_new.append(v_s.transpose(1, 0, 2).reshape(N_SAMPLE, 1, E_MIX))
        else:
            st0 = state_conv[jm, :, 0, :]
            st1 = state_conv[jm, :, 1, :]
            h, ci_s, ci_p = _conv(h, mix_gain, c_w_in_b, c_w_conv, st0, st1, c_w_out_b, layer, jm)
            ci_s = ci_s.transpose(1, 0, 2).reshape(N_SAMPLE, E_MIX)
            ci_p = ci_p.transpose(0, 2, 1, 3).reshape(N_PROMPT // SEQ, _HALO, E_MIX)
            conv_new_prompt.append(ci_p[:, _HALO - 2:, :])
            conv_new_sample.append(jnp.stack([st1, ci_s], axis=1))
        h = _ffn(h, *f2, layer)
        h = _ple(h, ple_gain, ple_w_gate_b, p_prompt, p_sample, ple_w_proj_b, g_final, layer,
                 final_norm=(layer == DEPTH - 1))

    y_prompt, y_sample = h
    return (y_prompt.reshape(N_PROMPT // SEQ, SEQ, D_MODEL), y_sample.reshape(N_SAMPLE, 1, D_MODEL),
            jnp.stack(conv_new_prompt), jnp.stack(conv_new_sample), jnp.stack(v_new))
```

```python
import functools

import jax
import jax.numpy as jnp
from jax import lax
from jax.experimental import pallas as pl
from jax.experimental.pallas import tpu as pltpu

F32 = jnp.float32
BF16 = jnp.bfloat16

D_MODEL = 2048
D_FF = 5632
E_MIX = 2048
N_HEAD = 8
D_HEAD = 256
CHUNK = 128
D_PLE = 256
SEQ = 2048
N_PROMPT = 4 * SEQ
N_SAMPLE = 128
N_TOK = N_PROMPT + N_SAMPLE
DEPTH = 2
EPS = 1e-6

TM = 640
N_TILE = N_TOK // TM
SAMPLE_ROW0 = TM - N_SAMPLE
TM_FFN = 1040
N_TILE_FFN = N_TOK // TM_FFN
SAMPLE_ROW0_FFN = TM_FFN - N_SAMPLE
TF = 512
TN = 512
V7X_VMEM_BYTES = 64 * 1024 * 1024
VMEM_LIMIT = V7X_VMEM_BYTES - 2 * 1024 * 1024

_ARB1 = pltpu.CompilerParams(dimension_semantics=("arbitrary",),
                             vmem_limit_bytes=VMEM_LIMIT)


def _rms_bf16(x, g):
    ms = jnp.mean(x * x, axis=-1, keepdims=True)
    return (x * lax.rsqrt(ms + EPS) * g).astype(BF16)


def _dot(a, b):
    return jnp.dot(a, b, preferred_element_type=F32)


def _gelu_tanh(x):
    c = 0.7978845608028654
    return x * (0.5 * (1.0 + jnp.tanh(c * (x + 0.044715 * (x * x * x)))))


N_FSTEP = D_FF // TF
N_STEP_FFN = N_TILE_FFN * N_FSTEP
W_SLOTS = 2
FIRST_NORM_STEP = 2
NORM_ROWS = 128
assert (N_FSTEP - FIRST_NORM_STEP) * NORM_ROWS >= TM_FFN and NORM_ROWS % 16 == 0


def _ffn_weight_copies(layer, step, w_hbm, w_buf, sem):
    wg_hbm, wu_hbm, wd_hbm = w_hbm
    wg_buf, wu_buf, wd_buf = w_buf
    slot = step % W_SLOTS
    cols = pl.ds(pl.multiple_of((step % N_FSTEP) * TF, TF), TF)
    return (
        pltpu.make_async_copy(wg_hbm.at[layer, :, cols], wg_buf.at[slot], sem.at[0, slot]),
        pltpu.make_async_copy(wu_hbm.at[layer, :, cols], wu_buf.at[slot], sem.at[1, slot]),
        pltpu.make_async_copy(wd_hbm.at[layer, cols, :], wd_buf.at[slot], sem.at[2, slot]),
    )


def _ffn_rows_in(src, tile, buf, sem, action):
    row0 = tile * TM_FFN
    rows = pl.ds(row0 if isinstance(row0, int) else pl.multiple_of(row0, 8), TM_FFN)
    if not isinstance(src, tuple):
        action(pltpu.make_async_copy(src.at[rows], buf, sem))
        return
    xp_hbm, xs_hbm = src

    @pl.when(tile < N_TILE_FFN - 1)
    def _():
        action(pltpu.make_async_copy(xp_hbm.at[rows], buf, sem))

    @pl.when(tile == N_TILE_FFN - 1)
    def _():
        tail = pl.ds((N_TILE_FFN - 1) * TM_FFN, SAMPLE_ROW0_FFN)
        action(pltpu.make_async_copy(xp_hbm.at[tail], buf.at[pl.ds(0, SAMPLE_ROW0_FFN)], sem))
        action(pltpu.make_async_copy(xs_hbm, buf.at[pl.ds(SAMPLE_ROW0_FFN, N_SAMPLE)], sem))


def _start(copy):
    copy.start()


def _wait(copy):
    copy.wait()


def _ffn_kernel(*refs, layer, split):
    if split:
        xp_hbm, xs_hbm, *refs = refs
        src = (xp_hbm, xs_hbm)
    else:
        src, *refs = refs
    (g_ref, wg_hbm, wu_hbm, wd_hbm, o_hbm,
     tbuf0, tbuf1, xn_buf0, xn_buf1, wg_buf, wu_buf, wd_buf, w_sem, in_sem, out_sem) = refs
    tbufs = (tbuf0, tbuf1)
    xn_bufs = (xn_buf0, xn_buf1)
    tile = pl.program_id(0)
    has_next = tile + 1 < N_TILE_FFN
    w_copies = functools.partial(_ffn_weight_copies, layer, w_hbm=(wg_hbm, wu_hbm, wd_hbm),
                                 w_buf=(wg_buf, wu_buf, wd_buf), sem=w_sem)

    def rows_out(t, slot):
        rows = pl.ds(pl.multiple_of(t * TM_FFN, 8), TM_FFN)
        return pltpu.make_async_copy(tbufs[slot], o_hbm.at[rows], out_sem.at[slot])

    def norm_rows(slot, rows):
        xn_bufs[slot][rows, :] = _rms_bf16(tbufs[slot][rows, :], g_ref[...])

    @pl.when(tile == 0)
    def _():
        for step in range(W_SLOTS):
            for c in w_copies(step):
                c.start()
        _ffn_rows_in(src, 0, tbufs[0], in_sem.at[0], _start)
        _ffn_rows_in(src, 0, tbufs[0], in_sem.at[0], _wait)
        norm_rows(0, slice(None))

    def run_tile(cur):
        nxt = 1 - cur

        def block(j, norm_next):
            step = tile * N_FSTEP + j
            slot = step % W_SLOTS
            for c in w_copies(step):
                c.wait()
            xn = xn_bufs[cur][...]
            gt = _dot(xn, wg_buf[slot].astype(BF16))
            up = _dot(xn, wu_buf[slot].astype(BF16))
            hd = (gt * jax.nn.sigmoid(gt)) * up * 0.5
            tbufs[cur][...] += _dot(hd.astype(BF16), wd_buf[slot].astype(BF16))
            if norm_next:
                row0 = jnp.minimum((j - FIRST_NORM_STEP) * NORM_ROWS, TM_FFN - NORM_ROWS)
                norm_rows(nxt, pl.ds(pl.multiple_of(row0, 16), NORM_ROWS))
            for c in w_copies(step + W_SLOTS):
                c.start()

        block(0, False)

        @pl.when(tile >= 1)
        def _():
            rows_out(tile - 1, nxt).wait()

        @pl.when(has_next)
        def _():
            _ffn_rows_in(src, tile + 1, tbufs[nxt], in_sem.at[nxt], _start)

        block(1, False)

        @pl.when(has_next)
        def _():
            _ffn_rows_in(src, tile + 1, tbufs[nxt], in_sem.at[nxt], _wait)

        def norm_block(j, carry):
            block(j, True)
            return carry

        lax.fori_loop(FIRST_NORM_STEP, N_FSTEP, norm_block, 0)
        rows_out(tile, cur).start()

        @pl.when(tile == N_TILE_FFN - 1)
        def _():
            rows_out(tile, cur).wait()
            for step in range(W_SLOTS):
                for c in w_copies(N_STEP_FFN + step):
                    c.wait()

    for parity in range(2):
        pl.when(tile % 2 == parity)(functools.partial(run_tile, parity))


def _ffn(h, gain, w_gate, w_up, w_down, layer):
    split = isinstance(h, tuple)
    hbm = pl.BlockSpec(memory_space=pl.ANY)
    return pl.pallas_call(
        functools.partial(_ffn_kernel, layer=layer, split=split),
        grid=(N_TILE_FFN,),
        in_specs=[hbm] * (2 if split else 1) + [
            pl.BlockSpec((None, 1, D_MODEL), lambda i: (layer, 0, 0)), hbm, hbm, hbm],
        out_specs=hbm,
        out_shape=jax.ShapeDtypeStruct((N_TOK, D_MODEL), F32),
        scratch_shapes=[
            pltpu.VMEM((TM_FFN, D_MODEL), F32),
            pltpu.VMEM((TM_FFN, D_MODEL), F32),
            pltpu.VMEM((TM_FFN, D_MODEL), BF16),
            pltpu.VMEM((TM_FFN, D_MODEL), BF16),
            pltpu.VMEM((W_SLOTS, D_MODEL, TF), F32),
            pltpu.VMEM((W_SLOTS, D_MODEL, TF), F32),
            pltpu.VMEM((W_SLOTS, TF, D_MODEL), F32),
            pltpu.SemaphoreType.DMA((3, W_SLOTS)),
            pltpu.SemaphoreType.DMA((2,)),
            pltpu.SemaphoreType.DMA((2,)),
        ],
        compiler_params=_ARB1,
        name="ffn",
    )(*(h if split else (h,)), gain, w_gate, w_up, w_down)


def _gmlp_weight_copies(jm, win_hbm, wout_hbm, win_buf, wout_buf, sem):
    return (pltpu.make_async_copy(win_hbm.at[jm], win_buf, sem.at[0]),
            pltpu.make_async_copy(wout_hbm.at[jm], wout_buf, sem.at[1]))


def _gmlp_kernel(h_ref, g_ref, win_hbm, wout_hbm, lng_ref, lnb_ref, ws_ref, bs_ref,
                 o_ref, vo_ref,
                 win_buf, wout_buf, xn_ref, v_ref, y_ref, sem, *, jm):
    tile = pl.program_id(0)
    is_last_tile = tile == N_TILE - 1
    copies = _gmlp_weight_copies(jm, win_hbm, wout_hbm, win_buf, wout_buf, sem)

    @pl.when(tile == 0)
    def _():
        for c in copies:
            c.start()

    xn_ref[...] = _rms_bf16(h_ref[...], g_ref[...])

    @pl.when(tile == 0)
    def _():
        for c in copies:
            c.wait()

    xn = xn_ref[...]

    def head_cols(first, head):
        return slice(first + head * D_HEAD, first + (head + 1) * D_HEAD)

    v_sum = jnp.zeros((TM, 1), F32)
    for head in range(N_HEAD):
        v = _gelu_tanh(_dot(xn, win_buf[:, head_cols(E_MIX, head)]))
        v_ref[head] = v
        v_sum = v_sum + jnp.sum(v, axis=-1, keepdims=True)
    mu = v_sum * (1.0 / E_MIX)
    var = jnp.zeros((TM, 1), F32)
    for head in range(N_HEAD):
        d = v_ref[head] - mu
        var = var + jnp.sum(d * d, axis=-1, keepdims=True)
    rstd = lax.rsqrt(var * (1.0 / E_MIX) + EPS)

    row = lax.broadcasted_iota(jnp.int32, (CHUNK, CHUNK), 0)
    col = lax.broadcasted_iota(jnp.int32, (CHUNK, CHUNK), 1)
    n_chunk = TM // CHUNK
    for head in range(N_HEAD):
        u = _gelu_tanh(_dot(xn, win_buf[:, head_cols(0, head)]))
        vn = (v_ref[head] - mu) * rstd * lng_ref[head] + lnb_ref[head]

        vo_ref[head] = vn[SAMPLE_ROW0:, :]
        w = ws_ref[head]
        b = bs_ref[head]
        w_causal = jnp.where(col <= row, w, 0.0)
        w_sample = jnp.where(col == row, w[0:1, 0:1], 0.0)
        b_sample = jnp.broadcast_to(b[0:1, :], (CHUNK, 1))
        w_last = jnp.where(is_last_tile, w_sample, w_causal).astype(BF16)
        b_last = jnp.where(is_last_tile, b_sample, b)
        w_causal = w_causal.astype(BF16)
        for c in range(n_chunk):
            rows = slice(c * CHUNK, (c + 1) * CHUNK)
            wm, bm = (w_last, b_last) if c == n_chunk - 1 else (w_causal, b)
            s = _dot(wm, vn[rows, :].astype(BF16)) + bm
            y_ref[rows, head_cols(0, head)] = (u[rows, :] * s).astype(BF16)

    o_ref[...] = h_ref[...] + _dot(y_ref[...], wout_buf[...])


def _gmlp(h, gain, w_in, ln_g, ln_b, w_s, b_s, w_out, layer, jm):
    hbm = pl.BlockSpec(memory_space=pl.ANY)
    return pl.pallas_call(
        functools.partial(_gmlp_kernel, jm=jm),
        grid=(N_TILE,),
        in_specs=[
            pl.BlockSpec((TM, D_MODEL), lambda i: (i, 0)),
            pl.BlockSpec((None, 1, D_MODEL), lambda i: (layer, 0, 0)),
            hbm, hbm,
            pl.BlockSpec((None, N_HEAD, 1, D_HEAD), lambda i: (jm, 0, 0, 0)),
            pl.BlockSpec((None, N_HEAD, 1, D_HEAD), lambda i: (jm, 0, 0, 0)),
            pl.BlockSpec((None, N_HEAD, CHUNK, CHUNK), lambda i: (jm, 0, 0, 0)),
            pl.BlockSpec((None, N_HEAD, CHUNK, 1), lambda i: (jm, 0, 0, 0)),
        ],
        out_specs=[
            pl.BlockSpec((TM, D_MODEL), lambda i: (i, 0)),
            pl.BlockSpec((N_HEAD, N_SAMPLE, D_HEAD), lambda i: (0, 0, 0)),
        ],
        out_shape=[
            jax.ShapeDtypeStruct((N_TOK, D_MODEL), F32),
            jax.ShapeDtypeStruct((N_HEAD, N_SAMPLE, D_HEAD), F32),
        ],
        scratch_shapes=[
            pltpu.VMEM((D_MODEL, 2 * E_MIX), BF16),
            pltpu.VMEM((E_MIX, D_MODEL), BF16),
            pltpu.VMEM((TM, D_MODEL), BF16),
            pltpu.VMEM((N_HEAD, TM, D_HEAD), F32),
            pltpu.VMEM((TM, E_MIX), BF16),
            pltpu.SemaphoreType.DMA((2,)),
        ],
        compiler_params=_ARB1,
        name="gmlp",
    )(h, gain, w_in, w_out, ln_g, ln_b, w_s, b_s)


_HALO = 8
CB = 256
N_CB = E_MIX // CB
CONV_PAIRS = N_CB // 2
CONV_SLOTS = 4
assert CONV_PAIRS % 2 == 0

_SEQ_TAIL = [divmod((s + 1) * SEQ - _HALO, TM) for s in range(N_PROMPT // SEQ)]


def _conv_kernel(h_ref, g_ref, win_hbm, wout_hbm, wk_ref, st0_ref, st1_ref,
                 o_ref, cs_ref, cp_ref,
                 win_buf, wout_buf, xn_ref, buf_ref, carry_ref, tails_ref, y_ref,
                 sem_in, sem_out, *, jm):
    tile = pl.program_id(0)
    is_last_tile = tile == N_TILE - 1

    def win_copy(block, slot):
        return pltpu.make_async_copy(win_hbm.at[jm, block], win_buf.at[slot], sem_in.at[slot])

    wout_copy = pltpu.make_async_copy(wout_hbm.at[jm], wout_buf, sem_out.at[0])

    @pl.when(tile == 0)
    def _():
        wout_copy.start()
        win_copy(0, 0).start()
        win_copy(1, 1).start()
        carry_ref[...] = jnp.zeros_like(carry_ref)

    xn_ref[...] = _rms_bf16(h_ref[...], g_ref[...])
    xn = xn_ref[...]
    pos = (lax.broadcasted_iota(jnp.int32, (TM, 1), 0) + tile * TM) & (SEQ - 1)

    def conv_block(block, slot):
        cols = slice(block * CB, (block + 1) * CB)
        bg = _dot(xn, win_buf[slot, 0])
        ci = _dot(xn, win_buf[slot, 1]) * _dot(xn, win_buf[slot, 2])

        for s, (_, row0) in enumerate(_SEQ_TAIL):
            tails_ref[s, block] = ci[row0:row0 + _HALO, :]
        ci_s = ci[SAMPLE_ROW0:, :]
        cs_ref[block] = ci_s

        buf = buf_ref.at[block % 2]
        buf[0:_HALO, :] = carry_ref[block]
        buf[_HALO:, :] = ci
        carry_ref[block] = ci[TM - _HALO:, :]
        prev1 = jnp.where(pos >= 1, buf[_HALO - 1:_HALO - 1 + TM, :], 0.0)
        prev2 = jnp.where(pos >= 2, buf[_HALO - 2:_HALO - 2 + TM, :], 0.0)
        w0 = wk_ref[0:1, cols]
        w1 = wk_ref[1:2, cols]
        w2 = wk_ref[2:3, cols]
        co = w0 * prev2 + w1 * prev1 + w2 * ci
        co_s = w0 * st0_ref[:, cols] + w1 * st1_ref[:, cols] + w2 * ci_s
        co_tail = jnp.where(is_last_tile, co_s, co[SAMPLE_ROW0:, :])
        y_ref[0:SAMPLE_ROW0, cols] = (bg[0:SAMPLE_ROW0, :] * co[0:SAMPLE_ROW0, :]).astype(BF16)
        y_ref[SAMPLE_ROW0:, cols] = (bg[SAMPLE_ROW0:, :] * co_tail).astype(BF16)

    for pair in range(CONV_PAIRS):
        blocks = (2 * pair, 2 * pair + 1)
        slots = (0, 1) if pair % 2 == 0 else (2, 3)
        free_slots = (2, 3) if pair % 2 == 0 else (0, 1)
        for b, slot in zip(blocks, free_slots):
            win_copy((b + 2) % N_CB, slot).start()
        for b, slot in zip(blocks, slots):
            win_copy(b, slot).wait()
        for b, slot in zip(blocks, slots):
            conv_block(b, slot)

    @pl.when(tile == 0)
    def _():
        wout_copy.wait()

    o_ref[...] = h_ref[...] + _dot(y_ref[...], wout_buf[...])

    for s, (seq_tile, _) in enumerate(_SEQ_TAIL):
        @pl.when(tile == seq_tile)
        def _():
            cp_ref[s] = tails_ref[s]

    @pl.when(is_last_tile)
    def _():
        for b in (0, 1):
            win_copy(b, b).wait()


def _conv(h, gain, w_in_blocked, w_conv, st0, st1, w_out, layer, jm):
    n_seq = N_PROMPT // SEQ
    hbm = pl.BlockSpec(memory_space=pl.ANY)
    return pl.pallas_call(
        functools.partial(_conv_kernel, jm=jm),
        grid=(N_TILE,),
        in_specs=[
            pl.BlockSpec((TM, D_MODEL), lambda i: (i, 0)),
            pl.BlockSpec((None, 1, D_MODEL), lambda i: (layer, 0, 0)),
            hbm, hbm,
            pl.BlockSpec((None, 3, E_MIX), lambda i: (jm, 0, 0)),
            pl.BlockSpec((N_SAMPLE, E_MIX), lambda i: (0, 0)),
            pl.BlockSpec((N_SAMPLE, E_MIX), lambda i: (0, 0)),
        ],
        out_specs=[
            pl.BlockSpec((TM, D_MODEL), lambda i: (i, 0)),
            pl.BlockSpec((N_CB, N_SAMPLE, CB), lambda i: (0, 0, 0)),
            pl.BlockSpec((n_seq, N_CB, _HALO, CB), lambda i: (0, 0, 0, 0)),
        ],
        out_shape=[
            jax.ShapeDtypeStruct((N_TOK, D_MODEL), F32),
            jax.ShapeDtypeStruct((N_CB, N_SAMPLE, CB), F32),
            jax.ShapeDtypeStruct((n_seq, N_CB, _HALO, CB), F32),
        ],
        scratch_shapes=[
            pltpu.VMEM((CONV_SLOTS, 3, D_MODEL, CB), BF16),
            pltpu.VMEM((E_MIX, D_MODEL), BF16),
            pltpu.VMEM((TM, D_MODEL), BF16),
            pltpu.VMEM((2, TM + _HALO, CB), F32),
            pltpu.VMEM((N_CB, _HALO, CB), F32),
            pltpu.VMEM((n_seq, N_CB, _HALO, CB), F32),
            pltpu.VMEM((TM, E_MIX), BF16),
            pltpu.SemaphoreType.DMA((CONV_SLOTS,)),
            pltpu.SemaphoreType.DMA((1,)),
        ],
        compiler_params=_ARB1,
        name="sconv",
    )(h, gain, w_in_blocked, w_out, w_conv, st0, st1)


def _ple_kernel(h_ref, g_ref, wg_ref, pp_ref, ps_ref, wp_ref, gf_ref, o_ref, *rest, final_norm):
    if final_norm:
        os_ref, pb_ref = rest
    else:
        (pb_ref,) = rest
    i = pl.program_id(0)

    @pl.when(i < N_TILE - 1)
    def _():
        pb_ref[...] = pp_ref[...].astype(BF16)

    @pl.when(i == N_TILE - 1)
    def _():
        pb_ref[0:SAMPLE_ROW0, :] = pp_ref[0:SAMPLE_ROW0, :].astype(BF16)
        pb_ref[SAMPLE_ROW0:, :] = ps_ref[...].astype(BF16)

    xn = _rms_bf16(h_ref[...], g_ref[...])
    pb = pb_ref[...]
    for c in range(D_MODEL // TN):
        cols = slice(c * TN, (c + 1) * TN)
        gate = jax.nn.sigmoid(_dot(xn, wg_ref[:, cols]))
        o_ref[:, cols] = h_ref[:, cols] + gate * _dot(pb, wp_ref[:, cols])
    if final_norm:
        hn = o_ref[...]
        ms = jnp.mean(hn * hn, axis=-1, keepdims=True)
        o_ref[...] = hn * lax.rsqrt(ms + EPS) * gf_ref[...]

        @pl.when(i == N_TILE - 1)
        def _():
            os_ref[...] = o_ref[SAMPLE_ROW0:, :]


def _ple(h, gain, w_gate, p_prompt, p_sample, w_proj, g_final, layer, final_norm):
    row_block = pl.BlockSpec((TM, D_MODEL), lambda i: (i, 0))
    if final_norm:
        out_specs = [row_block, pl.BlockSpec((N_SAMPLE, D_MODEL), lambda i: (0, 0))]
        out_shape = [jax.ShapeDtypeStruct((N_PROMPT, D_MODEL), F32),
                     jax.ShapeDtypeStruct((N_SAMPLE, D_MODEL), F32)]
    else:
        out_specs = row_block
        out_shape = jax.ShapeDtypeStruct((N_TOK, D_MODEL), F32)
    return pl.pallas_call(
        functools.partial(_ple_kernel, final_norm=final_norm),
        grid=(N_TILE,),
        in_specs=[
            row_block,
            pl.BlockSpec((None, 1, D_MODEL), lambda i: (layer, 0, 0)),
            pl.BlockSpec((None, D_MODEL, D_MODEL), lambda i: (layer, 0, 0)),
            pl.BlockSpec((None, TM, D_PLE), lambda i: (layer, i, 0)),
            pl.BlockSpec((None, N_SAMPLE, D_PLE), lambda i: (layer, 0, 0)),
            pl.BlockSpec((None, D_PLE, D_MODEL), lambda i: (layer, 0, 0)),
            pl.BlockSpec((1, D_MODEL), lambda i: (0, 0)),
        ],
        out_specs=out_specs,
        out_shape=out_shape,
        scratch_shapes=[pltpu.VMEM((TM, D_PLE), BF16)],
        compiler_params=_ARB1,
        name="ple",
    )(h, gain, w_gate, p_prompt, p_sample, w_proj, g_final)


def kernel(x_prompt, x_sample, state_conv, p_prompt, p_sample, ffn1_norm, ffn1_w_gate, ffn1_w_up, ffn1_w_down, mix_norm, a_w_in, a_ln_g, a_ln_b, a_w_s, a_b_s, a_w_out, c_w_in, c_w_conv, c_w_out, ffn2_norm, ffn2_w_gate, ffn2_w_up, ffn2_w_down, ple_norm, ple_w_gate, ple_w_proj, final_norm):
    bf = lambda w: w.astype(BF16)
    gain3 = lambda g: g.reshape(g.shape[0], 1, D_MODEL)

    h = (x_prompt.reshape(N_PROMPT, D_MODEL), x_sample.reshape(N_SAMPLE, D_MODEL))
    p_prompt = p_prompt.reshape(DEPTH, N_PROMPT, D_PLE)
    p_sample = p_sample.reshape(DEPTH, N_SAMPLE, D_PLE)

    f1 = (gain3(ffn1_norm), ffn1_w_gate, ffn1_w_up, ffn1_w_down)
    f2 = (gain3(ffn2_norm), ffn2_w_gate, ffn2_w_up, ffn2_w_down)
    mix_gain = gain3(mix_norm)
    ple_gain = gain3(ple_norm)
    a_w_in_b, a_w_out_b = bf(a_w_in), bf(a_w_out)
    c_w_in_b = bf(c_w_in).reshape(-1, D_MODEL, 3, N_CB, CB).transpose(0, 3, 2, 1, 4)
    c_w_out_b = bf(c_w_out)
    ple_w_gate_b, ple_w_proj_b = bf(ple_w_gate), bf(ple_w_proj)
    ln_g = a_ln_g.reshape(-1, N_HEAD, 1, D_HEAD)
    ln_b = a_ln_b.reshape(-1, N_HEAD, 1, D_HEAD)
    b_s = a_b_s.reshape(-1, N_HEAD, CHUNK, 1)
    g_final = final_norm.reshape(1, D_MODEL)

    conv_new_prompt, conv_new_sample, v_new = [], [], []
    for layer in range(DEPTH):
        h = _ffn(h, *f1, layer)
        jm = layer // 2
        if layer % 2 == 0:
            h, v_s = _gmlp(h, mix_gain, a_w_in_b, ln_g, ln_b, a_w_s, b_s, a_w_out_b, layer, jm)
            v_new.append(v_s.transpose(1, 0, 2).reshape(N_SAMPLE, 1, E_MIX))
        else:
            st0 = state_conv[jm, :, 0, :]
            st1 = state_conv[jm, :, 1, :]
            h, ci_s, ci_p = _conv(h, mix_gain, c_w_in_b, c_w_conv, st0, st1, c_w_out_b, layer, jm)
            ci_s = ci_s.transpose(1, 0, 2).reshape(N_SAMPLE, E_MIX)
            ci_p = ci_p.transpose(0, 2, 1, 3).reshape(N_PROMPT // SEQ, _HALO, E_MIX)
            conv_new_prompt.append(ci_p[:, _HALO - 2:, :])
            conv_new_sample.append(jnp.stack([st1, ci_s], axis=1))
        h = _ffn(h, *f2, layer)
        h = _ple(h, ple_gain, ple_w_gate_b, p_prompt, p_sample, ple_w_proj_b, g_final, layer,
                 final_norm=(layer == DEPTH - 1))

    y_prompt, y_sample = h
    return (y_prompt.reshape(N_PROMPT // SEQ, SEQ, D_MODEL), y_sample.reshape(N_SAMPLE, 1, D_MODEL),
            jnp.stack(conv_new_prompt), jnp.stack(conv_new_sample), jnp.stack(v_new))
```

```python
import functools

import jax
import jax.numpy as jnp
from jax import lax
from jax.experimental import pallas as pl
from jax.experimental.pallas import tpu as pltpu

F32 = jnp.float32
BF16 = jnp.bfloat16

D_MODEL = 2048
D_FF = 5632
E_MIX = 2048
N_HEAD = 8
D_HEAD = 256
CHUNK = 128
D_PLE = 256
SEQ = 2048
N_PROMPT = 4 * SEQ
N_SAMPLE = 128
N_TOK = N_PROMPT + N_SAMPLE
DEPTH = 2
EPS = 1e-6

TM = 640
N_TILE = N_TOK // TM
SAMPLE_ROW0 = TM - N_SAMPLE
TM_FFN = 1664
N_TILE_FFN = N_TOK // TM_FFN
SAMPLE_ROW0_FFN = TM_FFN - N_SAMPLE
TF = 256
TN = 512
V7X_VMEM_BYTES = 64 * 1024 * 1024
VMEM_LIMIT = V7X_VMEM_BYTES - 2 * 1024 * 1024

_ARB1 = pltpu.CompilerParams(dimension_semantics=("arbitrary",),
                             vmem_limit_bytes=VMEM_LIMIT)


def _rms_bf16(x, g):
    ms = jnp.mean(x * x, axis=-1, keepdims=True)
    return (x * lax.rsqrt(ms + EPS) * g).astype(BF16)


def _dot(a, b):
    return jnp.dot(a, b, preferred_element_type=F32)


def _gelu_tanh(x):
    c = 0.7978845608028654
    return x * (0.5 * (1.0 + jnp.tanh(c * (x + 0.044715 * (x * x * x)))))


N_FSTEP = D_FF // TF
N_STEP_FFN = N_TILE_FFN * N_FSTEP
W_SLOTS = 3
W_AHEAD = W_SLOTS - 1
assert N_FSTEP % 2 == 0


def _ffn_weight_copies(layer, step, w_hbm, w_buf, sem):
    wg_hbm, wu_hbm, wd_hbm = w_hbm
    wg_buf, wu_buf, wd_buf = w_buf
    slot = step % W_SLOTS
    cols = pl.ds(pl.multiple_of((step % N_FSTEP) * TF, TF), TF)
    return (
        pltpu.make_async_copy(wg_hbm.at[layer, :, cols], wg_buf.at[slot], sem.at[0, slot]),
        pltpu.make_async_copy(wu_hbm.at[layer, :, cols], wu_buf.at[slot], sem.at[1, slot]),
        pltpu.make_async_copy(wd_hbm.at[layer, cols, :], wd_buf.at[slot], sem.at[2, slot]),
    )


def _ffn_rows_in(src, tile, buf, sem, action):
    row0 = tile * TM_FFN
    rows = pl.ds(row0 if isinstance(row0, int) else pl.multiple_of(row0, 8), TM_FFN)
    if not isinstance(src, tuple):
        action(pltpu.make_async_copy(src.at[rows], buf, sem))
        return
    xp_hbm, xs_hbm = src

    @pl.when(tile < N_TILE_FFN - 1)
    def _():
        action(pltpu.make_async_copy(xp_hbm.at[rows], buf, sem))

    @pl.when(tile == N_TILE_FFN - 1)
    def _():
        tail = pl.ds((N_TILE_FFN - 1) * TM_FFN, SAMPLE_ROW0_FFN)
        action(pltpu.make_async_copy(xp_hbm.at[tail], buf.at[pl.ds(0, SAMPLE_ROW0_FFN)], sem))
        action(pltpu.make_async_copy(xs_hbm, buf.at[pl.ds(SAMPLE_ROW0_FFN, N_SAMPLE)], sem))


def _start(copy):
    copy.start()


def _wait(copy):
    copy.wait()


def _ffn_kernel(*refs, layer, split):
    if split:
        xp_hbm, xs_hbm, *refs = refs
        src = (xp_hbm, xs_hbm)
    else:
        src, *refs = refs
    (g_ref, wg_hbm, wu_hbm, wd_hbm, o_hbm,
     tbuf0, tbuf1, xn_ref, wg_buf, wu_buf, wd_buf, w_sem, in_sem, out_sem) = refs
    tbufs = (tbuf0, tbuf1)
    tile = pl.program_id(0)
    has_next = tile + 1 < N_TILE_FFN
    w_copies = functools.partial(_ffn_weight_copies, layer, w_hbm=(wg_hbm, wu_hbm, wd_hbm),
                                 w_buf=(wg_buf, wu_buf, wd_buf), sem=w_sem)

    def rows_out(t, slot):
        rows = pl.ds(pl.multiple_of(t * TM_FFN, 8), TM_FFN)
        return pltpu.make_async_copy(tbufs[slot], o_hbm.at[rows], out_sem.at[slot])

    @pl.when(tile == 0)
    def _():
        for step in range(W_AHEAD):
            for c in w_copies(step):
                c.start()
        _ffn_rows_in(src, 0, tbufs[0], in_sem.at[0], _start)

    def run_tile(cur):
        nxt = 1 - cur
        tbuf = tbufs[cur]
        _ffn_rows_in(src, tile, tbuf, in_sem.at[cur], _wait)
        xn_ref[...] = _rms_bf16(tbuf[...], g_ref[...])

        def prefetch(step):
            for c in w_copies(step):
                c.start()

        def block(step):
            slot = step % W_SLOTS
            xn = xn_ref[...]
            gt = _dot(xn, wg_buf[slot].astype(BF16))
            up = _dot(xn, wu_buf[slot].astype(BF16))
            hd = (gt * jax.nn.sigmoid(gt)) * up * 0.5
            tbuf[...] += _dot(hd.astype(BF16), wd_buf[slot].astype(BF16))

        def block_pair(jp, carry):
            step = tile * N_FSTEP + 2 * jp
            prefetch(step + W_AHEAD)
            for c in w_copies(step) + w_copies(step + 1):
                c.wait()
            block(step)
            prefetch(step + 1 + W_AHEAD)
            block(step + 1)
            return carry

        block_pair(0, 0)

        @pl.when(tile >= 1)
        def _():
            rows_out(tile - 1, nxt).wait()

        @pl.when(has_next)
        def _():
            _ffn_rows_in(src, tile + 1, tbufs[nxt], in_sem.at[nxt], _start)

        lax.fori_loop(1, N_FSTEP // 2, block_pair, 0)
        rows_out(tile, cur).start()

        @pl.when(tile == N_TILE_FFN - 1)
        def _():
            rows_out(tile, cur).wait()
            for step in range(W_AHEAD):
                for c in w_copies(N_STEP_FFN + step):
                    c.wait()

    for parity in range(2):
        pl.when(tile % 2 == parity)(functools.partial(run_tile, parity))


def _ffn(h, gain, w_gate, w_up, w_down, layer):
    split = isinstance(h, tuple)
    hbm = pl.BlockSpec(memory_space=pl.ANY)
    return pl.pallas_call(
        functools.partial(_ffn_kernel, layer=layer, split=split),
        grid=(N_TILE_FFN,),
        in_specs=[hbm] * (2 if split else 1) + [
            pl.BlockSpec((None, 1, D_MODEL), lambda i: (layer, 0, 0)), hbm, hbm, hbm],
        out_specs=hbm,
        out_shape=jax.ShapeDtypeStruct((N_TOK, D_MODEL), F32),
        scratch_shapes=[
            pltpu.VMEM((TM_FFN, D_MODEL), F32),
            pltpu.VMEM((TM_FFN, D_MODEL), F32),
            pltpu.VMEM((TM_FFN, D_MODEL), BF16),
            pltpu.VMEM((W_SLOTS, D_MODEL, TF), F32),
            pltpu.VMEM((W_SLOTS, D_MODEL, TF), F32),
            pltpu.VMEM((W_SLOTS, TF, D_MODEL), F32),
            pltpu.SemaphoreType.DMA((3, W_SLOTS)),
            pltpu.SemaphoreType.DMA((2,)),
            pltpu.SemaphoreType.DMA((2,)),
        ],
        compiler_params=_ARB1,
        name="ffn",
    )(*(h if split else (h,)), gain, w_gate, w_up, w_down)


def _gmlp_weight_copies(jm, win_hbm, wout_hbm, win_buf, wout_buf, sem):
    return (pltpu.make_async_copy(win_hbm.at[jm], win_buf, sem.at[0]),
            pltpu.make_async_copy(wout_hbm.at[jm], wout_buf, sem.at[1]))


def _gmlp_kernel(h_ref, g_ref, win_hbm, wout_hbm, lng_ref, lnb_ref, ws_ref, bs_ref,
                 o_ref, vo_ref,
                 win_buf, wout_buf, xn_ref, v_ref, y_ref, sem, *, jm):
    tile = pl.program_id(0)
    is_last_tile = tile == N_TILE - 1
    copies = _gmlp_weight_copies(jm, win_hbm, wout_hbm, win_buf, wout_buf, sem)

    @pl.when(tile == 0)
    def _():
        for c in copies:
            c.start()

    xn_ref[...] = _rms_bf16(h_ref[...], g_ref[...])

    @pl.when(tile == 0)
    def _():
        for c in copies:
            c.wait()

    xn = xn_ref[...]

    def head_cols(first, head):
        return slice(first + head * D_HEAD, first + (head + 1) * D_HEAD)

    v_sum = jnp.zeros((TM, 1), F32)
    for head in range(N_HEAD):
        v = _gelu_tanh(_dot(xn, win_buf[:, head_cols(E_MIX, head)]))
        v_ref[head] = v
        v_sum = v_sum + jnp.sum(v, axis=-1, keepdims=True)
    mu = v_sum * (1.0 / E_MIX)
    var = jnp.zeros((TM, 1), F32)
    for head in range(N_HEAD):
        d = v_ref[head] - mu
        var = var + jnp.sum(d * d, axis=-1, keepdims=True)
    rstd = lax.rsqrt(var * (1.0 / E_MIX) + EPS)

    row = lax.broadcasted_iota(jnp.int32, (CHUNK, CHUNK), 0)
    col = lax.broadcasted_iota(jnp.int32, (CHUNK, CHUNK), 1)
    n_chunk = TM // CHUNK
    for head in range(N_HEAD):
        u = _gelu_tanh(_dot(xn, win_buf[:, head_cols(0, head)]))
        vn = (v_ref[head] - mu) * rstd * lng_ref[head] + lnb_ref[head]

        vo_ref[head] = vn[SAMPLE_ROW0:, :]
        w = ws_ref[head]
        b = bs_ref[head]
        w_causal = jnp.where(col <= row, w, 0.0)
        w_sample = jnp.where(col == row, w[0:1, 0:1], 0.0)
        b_sample = jnp.broadcast_to(b[0:1, :], (CHUNK, 1))
        w_last = jnp.where(is_last_tile, w_sample, w_causal).astype(BF16)
        b_last = jnp.where(is_last_tile, b_sample, b)
        w_causal = w_causal.astype(BF16)
        for c in range(n_chunk):
            rows = slice(c * CHUNK, (c + 1) * CHUNK)
            wm, bm = (w_last, b_last) if c == n_chunk - 1 else (w_causal, b)
            s = _dot(wm, vn[rows, :].astype(BF16)) + bm
            y_ref[rows, head_cols(0, head)] = (u[rows, :] * s).astype(BF16)

    o_ref[...] = h_ref[...] + _dot(y_ref[...], wout_buf[...])


def _gmlp(h, gain, w_in, ln_g, ln_b, w_s, b_s, w_out, layer, jm):
    hbm = pl.BlockSpec(memory_space=pl.ANY)
    return pl.pallas_call(
        functools.partial(_gmlp_kernel, jm=jm),
        grid=(N_TILE,),
        in_specs=[
            pl.BlockSpec((TM, D_MODEL), lambda i: (i, 0)),
            pl.BlockSpec((None, 1, D_MODEL), lambda i: (layer, 0, 0)),
            hbm, hbm,
            pl.BlockSpec((None, N_HEAD, 1, D_HEAD), lambda i: (jm, 0, 0, 0)),
            pl.BlockSpec((None, N_HEAD, 1, D_HEAD), lambda i: (jm, 0, 0, 0)),
            pl.BlockSpec((None, N_HEAD, CHUNK, CHUNK), lambda i: (jm, 0, 0, 0)),
            pl.BlockSpec((None, N_HEAD, CHUNK, 1), lambda i: (jm, 0, 0, 0)),
        ],
        out_specs=[
            pl.BlockSpec((TM, D_MODEL), lambda i: (i, 0)),
            pl.BlockSpec((N_HEAD, N_SAMPLE, D_HEAD), lambda i: (0, 0, 0)),
        ],
        out_shape=[
            jax.ShapeDtypeStruct((N_TOK, D_MODEL), F32),
            jax.ShapeDtypeStruct((N_HEAD, N_SAMPLE, D_HEAD), F32),
        ],
        scratch_shapes=[
            pltpu.VMEM((D_MODEL, 2 * E_MIX), BF16),
            pltpu.VMEM((E_MIX, D_MODEL), BF16),
            pltpu.VMEM((TM, D_MODEL), BF16),
            pltpu.VMEM((N_HEAD, TM, D_HEAD), F32),
            pltpu.VMEM((TM, E_MIX), BF16),
            pltpu.SemaphoreType.DMA((2,)),
        ],
        compiler_params=_ARB1,
        name="gmlp",
    )(h, gain, w_in, w_out, ln_g, ln_b, w_s, b_s)


_HALO = 8
CB = 256
N_CB = E_MIX // CB
CONV_PAIRS = N_CB // 2
CONV_SLOTS = 4
assert CONV_PAIRS % 2 == 0

_SEQ_TAIL = [divmod((s + 1) * SEQ - _HALO, TM) for s in range(N_PROMPT // SEQ)]


def _conv_kernel(h_ref, g_ref, win_hbm, wout_hbm, wk_ref, st0_ref, st1_ref,
                 o_ref, cs_ref, cp_ref,
                 win_buf, wout_buf, xn_ref, buf_ref, carry_ref, tails_ref, y_ref,
                 sem_in, sem_out, *, jm):
    tile = pl.program_id(0)
    is_last_tile = tile == N_TILE - 1

    def win_copy(block, slot):
        return pltpu.make_async_copy(win_hbm.at[jm, block], win_buf.at[slot], sem_in.at[slot])

    wout_copy = pltpu.make_async_copy(wout_hbm.at[jm], wout_buf, sem_out.at[0])

    @pl.when(tile == 0)
    def _():
        wout_copy.start()
        win_copy(0, 0).start()
        win_copy(1, 1).start()
        carry_ref[...] = jnp.zeros_like(carry_ref)

    xn_ref[...] = _rms_bf16(h_ref[...], g_ref[...])
    xn = xn_ref[...]
    pos = (lax.broadcasted_iota(jnp.int32, (TM, 1), 0) + tile * TM) & (SEQ - 1)

    def conv_block(block, slot):
        cols = slice(block * CB, (block + 1) * CB)
        bg = _dot(xn, win_buf[slot, 0])
        ci = _dot(xn, win_buf[slot, 1]) * _dot(xn, win_buf[slot, 2])

        for s, (_, row0) in enumerate(_SEQ_TAIL):
            tails_ref[s, block] = ci[row0:row0 + _HALO, :]
        ci_s = ci[SAMPLE_ROW0:, :]
        cs_ref[block] = ci_s

        buf = buf_ref.at[block % 2]
        buf[0:_HALO, :] = carry_ref[block]
        buf[_HALO:, :] = ci
        carry_ref[block] = ci[TM - _HALO:, :]
        prev1 = jnp.where(pos >= 1, buf[_HALO - 1:_HALO - 1 + TM, :], 0.0)
        prev2 = jnp.where(pos >= 2, buf[_HALO - 2:_HALO - 2 + TM, :], 0.0)
        w0 = wk_ref[0:1, cols]
        w1 = wk_ref[1:2, cols]
        w2 = wk_ref[2:3, cols]
        co = w0 * prev2 + w1 * prev1 + w2 * ci
        co_s = w0 * st0_ref[:, cols] + w1 * st1_ref[:, cols] + w2 * ci_s
        co_tail = jnp.where(is_last_tile, co_s, co[SAMPLE_ROW0:, :])
        y_ref[0:SAMPLE_ROW0, cols] = (bg[0:SAMPLE_ROW0, :] * co[0:SAMPLE_ROW0, :]).astype(BF16)
        y_ref[SAMPLE_ROW0:, cols] = (bg[SAMPLE_ROW0:, :] * co_tail).astype(BF16)

    for pair in range(CONV_PAIRS):
        blocks = (2 * pair, 2 * pair + 1)
        slots = (0, 1) if pair % 2 == 0 else (2, 3)
        free_slots = (2, 3) if pair % 2 == 0 else (0, 1)
        for b, slot in zip(blocks, free_slots):
            win_copy((b + 2) % N_CB, slot).start()
        for b, slot in zip(blocks, slots):
            win_copy(b, slot).wait()
        for b, slot in zip(blocks, slots):
            conv_block(b, slot)

    @pl.when(tile == 0)
    def _():
        wout_copy.wait()

    o_ref[...] = h_ref[...] + _dot(y_ref[...], wout_buf[...])

    for s, (seq_tile, _) in enumerate(_SEQ_TAIL):
        @pl.when(tile == seq_tile)
        def _():
            cp_ref[s] = tails_ref[s]

    @pl.when(is_last_tile)
    def _():
        for b in (0, 1):
            win_copy(b, b).wait()


def _conv(h, gain, w_in_blocked, w_conv, st0, st1, w_out, layer, jm):
    n_seq = N_PROMPT // SEQ
    hbm = pl.BlockSpec(memory_space=pl.ANY)
    return pl.pallas_call(
        functools.partial(_conv_kernel, jm=jm),
        grid=(N_TILE,),
        in_specs=[
            pl.BlockSpec((TM, D_MODEL), lambda i: (i, 0)),
            pl.BlockSpec((None, 1, D_MODEL), lambda i: (layer, 0, 0)),
            hbm, hbm,
            pl.BlockSpec((None, 3, E_MIX), lambda i: (jm, 0, 0)),
            pl.BlockSpec((N_SAMPLE, E_MIX), lambda i: (0, 0)),
            pl.BlockSpec((N_SAMPLE, E_MIX), lambda i: (0, 0)),
        ],
        out_specs=[
            pl.BlockSpec((TM, D_MODEL), lambda i: (i, 0)),
            pl.BlockSpec((N_CB, N_SAMPLE, CB), lambda i: (0, 0, 0)),
            pl.BlockSpec((n_seq, N_CB, _HALO, CB), lambda i: (0, 0, 0, 0)),
        ],
        out_shape=[
            jax.ShapeDtypeStruct((N_TOK, D_MODEL), F32),
            jax.ShapeDtypeStruct((N_CB, N_SAMPLE, CB), F32),
            jax.ShapeDtypeStruct((n_seq, N_CB, _HALO, CB), F32),
        ],
        scratch_shapes=[
            pltpu.VMEM((CONV_SLOTS, 3, D_MODEL, CB), BF16),
            pltpu.VMEM((E_MIX, D_MODEL), BF16),
            pltpu.VMEM((TM, D_MODEL), BF16),
            pltpu.VMEM((2, TM + _HALO, CB), F32),
            pltpu.VMEM((N_CB, _HALO, CB), F32),
            pltpu.VMEM((n_seq, N_CB, _HALO, CB), F32),
            pltpu.VMEM((TM, E_MIX), BF16),
            pltpu.SemaphoreType.DMA((CONV_SLOTS,)),
            pltpu.SemaphoreType.DMA((1,)),
        ],
        compiler_params=_ARB1,
        name="sconv",
    )(h, gain, w_in_blocked, w_out, w_conv, st0, st1)


def _ple_kernel(h_ref, g_ref, wg_ref, pp_ref, ps_ref, wp_ref, gf_ref, o_ref, *rest, final_norm):
    if final_norm:
        os_ref, pb_ref = rest
    else:
        (pb_ref,) = rest
    i = pl.program_id(0)

    @pl.when(i < N_TILE - 1)
    def _():
        pb_ref[...] = pp_ref[...].astype(BF16)

    @pl.when(i == N_TILE - 1)
    def _():
        pb_ref[0:SAMPLE_ROW0, :] = pp_ref[0:SAMPLE_ROW0, :].astype(BF16)
        pb_ref[SAMPLE_ROW0:, :] = ps_ref[...].astype(BF16)

    xn = _rms_bf16(h_ref[...], g_ref[...])
    pb = pb_ref[...]
    for c in range(D_MODEL // TN):
        cols = slice(c * TN, (c + 1) * TN)
        gate = jax.nn.sigmoid(_dot(xn, wg_ref[:, cols]))
        o_ref[:, cols] = h_ref[:, cols] + gate * _dot(pb, wp_ref[:, cols])
    if final_norm:
        hn = o_ref[...]
        ms = jnp.mean(hn * hn, axis=-1, keepdims=True)
        o_ref[...] = hn * lax.rsqrt(ms + EPS) * gf_ref[...]

        @pl.when(i == N_TILE - 1)
        def _():
            os_ref[...] = o_ref[SAMPLE_ROW0:, :]


def _ple(h, gain, w_gate, p_prompt, p_sample, w_proj, g_final, layer, final_norm):
    row_block = pl.BlockSpec((TM, D_MODEL), lambda i: (i, 0))
    if final_norm:
        out_specs = [row_block, pl.BlockSpec((N_SAMPLE, D_MODEL), lambda i: (0, 0))]
        out_shape = [jax.ShapeDtypeStruct((N_PROMPT, D_MODEL), F32),
                     jax.ShapeDtypeStruct((N_SAMPLE, D_MODEL), F32)]
    else:
        out_specs = row_block
        out_shape = jax.ShapeDtypeStruct((N_TOK, D_MODEL), F32)
    return pl.pallas_call(
        functools.partial(_ple_kernel, final_norm=final_norm),
        grid=(N_TILE,),
        in_specs=[
            row_block,
            pl.BlockSpec((None, 1, D_MODEL), lambda i: (layer, 0, 0)),
            pl.BlockSpec((None, D_MODEL, D_MODEL), lambda i: (layer, 0, 0)),
            pl.BlockSpec((None, TM, D_PLE), lambda i: (layer, i, 0)),
            pl.BlockSpec((None, N_SAMPLE, D_PLE), lambda i: (layer, 0, 0)),
            pl.BlockSpec((None, D_PLE, D_MODEL), lambda i: (layer, 0, 0)),
            pl.BlockSpec((1, D_MODEL), lambda i: (0, 0)),
        ],
        out_specs=out_specs,
        out_shape=out_shape,
        scratch_shapes=[pltpu.VMEM((TM, D_PLE), BF16)],
        compiler_params=_ARB1,
        name="ple",
    )(h, gain, w_gate, p_prompt, p_sample, w_proj, g_final)


def kernel(x_prompt, x_sample, state_conv, p_prompt, p_sample, ffn1_norm, ffn1_w_gate, ffn1_w_up, ffn1_w_down, mix_norm, a_w_in, a_ln_g, a_ln_b, a_w_s, a_b_s, a_w_out, c_w_in, c_w_conv, c_w_out, ffn2_norm, ffn2_w_gate, ffn2_w_up, ffn2_w_down, ple_norm, ple_w_gate, ple_w_proj, final_norm):
    bf = lambda w: w.astype(BF16)
    gain3 = lambda g: g.reshape(g.shape[0], 1, D_MODEL)

    h = (x_prompt.reshape(N_PROMPT, D_MODEL), x_sample.reshape(N_SAMPLE, D_MODEL))
    p_prompt = p_prompt.reshape(DEPTH, N_PROMPT, D_PLE)
    p_sample = p_sample.reshape(DEPTH, N_SAMPLE, D_PLE)

    f1 = (gain3(ffn1_norm), ffn1_w_gate, ffn1_w_up, ffn1_w_down)
    f2 = (gain3(ffn2_norm), ffn2_w_gate, ffn2_w_up, ffn2_w_down)
    mix_gain = gain3(mix_norm)
    ple_gain = gain3(ple_norm)
    a_w_in_b, a_w_out_b = bf(a_w_in), bf(a_w_out)
    c_w_in_b = bf(c_w_in).reshape(-1, D_MODEL, 3, N_CB, CB).transpose(0, 3, 2, 1, 4)
    c_w_out_b = bf(c_w_out)
    ple_w_gate_b, ple_w_proj_b = bf(ple_w_gate), bf(ple_w_proj)
    ln_g = a_ln_g.reshape(-1, N_HEAD, 1, D_HEAD)
    ln_b = a_ln_b.reshape(-1, N_HEAD, 1, D_HEAD)
    b_s = a_b_s.reshape(-1, N_HEAD, CHUNK, 1)
    g_final = final_norm.reshape(1, D_MODEL)

    conv_new_prompt, conv_new_sample, v_new = [], [], []
    for layer in range(DEPTH):
        h = _ffn(h, *f1, layer)
        jm = layer // 2
        if layer % 2 == 0:
            h, v_s = _gmlp(h, mix_gain, a_w_in_b, ln_g, ln_b, a_w_s, b_s, a_w_out_b, layer, jm)
            v_new.append(v_s.transpose(1, 0, 2).reshape(N_SAMPLE, 1, E_MIX))
        else:
            st0 = state_conv[jm, :, 0, :]
            st1 = state_conv[jm, :, 1, :]
            h, ci_s, ci_p = _conv(h, mix_gain, c_w_in_b, c_w_conv, st0, st1, c_w_out_b, layer, jm)
            ci_s = ci_s.transpose(1, 0, 2).reshape(N_SAMPLE, E_MIX)
            ci_p = ci_p.transpose(0, 2, 1, 3).reshape(N_PROMPT // SEQ, _HALO, E_MIX)
            conv_new_prompt.append(ci_p[:, _HALO - 2:, :])
            conv_new_sample.append(jnp.stack([st1, ci_s], axis=1))
        h = _ffn(h, *f2, layer)
        h = _ple(h, ple_gain, ple_w_gate_b, p_prompt, p_sample, ple_w_proj_b, g_final, layer,
                 final_norm=(layer == DEPTH - 1))

    y_prompt, y_sample = h
    return (y_prompt.reshape(N_PROMPT // SEQ, SEQ, D_MODEL), y_sample.reshape(N_SAMPLE, 1, D_MODEL),
            jnp.stack(conv_new_prompt), jnp.stack(conv_new_sample), jnp.stack(v_new))
```

```python
import functools

import jax
import jax.numpy as jnp
from jax import lax
from jax.experimental import pallas as pl
from jax.experimental.pallas import tpu as pltpu

F32 = jnp.float32
BF16 = jnp.bfloat16

D_MODEL = 2048
D_FF = 5632
E_MIX = 2048
N_HEAD = 8
D_HEAD = 256
CHUNK = 128
D_PLE = 256
SEQ = 2048
N_PROMPT = 4 * SEQ
N_SAMPLE = 128
N_TOK = N_PROMPT + N_SAMPLE
DEPTH = 2
EPS = 1e-6

TM = 640
N_TILE = N_TOK // TM
SAMPLE_ROW0 = TM - N_SAMPLE
TM_FFN = 1664
N_TILE_FFN = N_TOK // TM_FFN
SAMPLE_ROW0_FFN = TM_FFN - N_SAMPLE
TF = 256
TN = 512
V7X_VMEM_BYTES = 64 * 1024 * 1024
VMEM_LIMIT = V7X_VMEM_BYTES - 2 * 1024 * 1024

_ARB1 = pltpu.CompilerParams(dimension_semantics=("arbitrary",),
                             vmem_limit_bytes=VMEM_LIMIT)


def _rms_bf16(x, g):
    ms = jnp.mean(x * x, axis=-1, keepdims=True)
    return (x * lax.rsqrt(ms + EPS) * g).astype(BF16)


def _dot(a, b):
    return jnp.dot(a, b, preferred_element_type=F32)


def _gelu_tanh(x):
    c = 0.7978845608028654
    return x * (0.5 * (1.0 + jnp.tanh(c * (x + 0.044715 * (x * x * x)))))


N_FSTEP = D_FF // TF
N_STEP_FFN = N_TILE_FFN * N_FSTEP
W_SLOTS = 3
W_AHEAD = W_SLOTS - 1
assert N_FSTEP % 2 == 0


def _ffn_weight_copies(layer, step, w_hbm, w_buf, sem):
    wg_hbm, wu_hbm, wd_hbm = w_hbm
    wg_buf, wu_buf, wd_buf = w_buf
    slot = step % W_SLOTS
    cols = pl.ds(pl.multiple_of((step % N_FSTEP) * TF, TF), TF)
    return (
        pltpu.make_async_copy(wg_hbm.at[layer, :, cols], wg_buf.at[slot], sem.at[0, slot]),
        pltpu.make_async_copy(wu_hbm.at[layer, :, cols], wu_buf.at[slot], sem.at[1, slot]),
        pltpu.make_async_copy(wd_hbm.at[layer, cols, :], wd_buf.at[slot], sem.at[2, slot]),
    )


def _ffn_rows_in(src, tile, buf, sem, action):
    row0 = tile * TM_FFN
    rows = pl.ds(row0 if isinstance(row0, int) else pl.multiple_of(row0, 8), TM_FFN)
    if not isinstance(src, tuple):
        action(pltpu.make_async_copy(src.at[rows], buf, sem))
        return
    xp_hbm, xs_hbm = src

    @pl.when(tile < N_TILE_FFN - 1)
    def _():
        action(pltpu.make_async_copy(xp_hbm.at[rows], buf, sem))

    @pl.when(tile == N_TILE_FFN - 1)
    def _():
        tail = pl.ds((N_TILE_FFN - 1) * TM_FFN, SAMPLE_ROW0_FFN)
        action(pltpu.make_async_copy(xp_hbm.at[tail], buf.at[pl.ds(0, SAMPLE_ROW0_FFN)], sem))
        action(pltpu.make_async_copy(xs_hbm, buf.at[pl.ds(SAMPLE_ROW0_FFN, N_SAMPLE)], sem))


def _start(copy):
    copy.start()


def _wait(copy):
    copy.wait()


def _ffn_kernel(*refs, layer, split):
    if split:
        xp_hbm, xs_hbm, *refs = refs
        src = (xp_hbm, xs_hbm)
    else:
        src, *refs = refs
    (g_ref, wg_hbm, wu_hbm, wd_hbm, o_hbm,
     tbufs, xn_ref, wg_buf, wu_buf, wd_buf, w_sem, in_sem, out_sem) = refs
    tile = pl.program_id(0)
    cur = tile % 2
    nxt = 1 - cur
    tbuf = tbufs.at[cur]
    w_copies = functools.partial(_ffn_weight_copies, layer, w_hbm=(wg_hbm, wu_hbm, wd_hbm),
                                 w_buf=(wg_buf, wu_buf, wd_buf), sem=w_sem)

    def rows_out(t, slot):
        rows = pl.ds(pl.multiple_of(t * TM_FFN, 8), TM_FFN)
        return pltpu.make_async_copy(tbufs.at[slot], o_hbm.at[rows], out_sem.at[slot])

    @pl.when(tile == 0)
    def _():
        for step in range(W_AHEAD):
            for c in w_copies(step):
                c.start()
        _ffn_rows_in(src, 0, tbufs.at[0], in_sem.at[0], _start)

    _ffn_rows_in(src, tile, tbuf, in_sem.at[cur], _wait)
    xn_ref[...] = _rms_bf16(tbuf[...], g_ref[...])

    def prefetch(step):
        for c in w_copies(step):
            c.start()

    def block(step):
        slot = step % W_SLOTS
        xn = xn_ref[...]
        gt = _dot(xn, wg_buf[slot].astype(BF16))
        up = _dot(xn, wu_buf[slot].astype(BF16))
        hd = (gt * jax.nn.sigmoid(gt)) * up * 0.5
        tbuf[...] += _dot(hd.astype(BF16), wd_buf[slot].astype(BF16))

    def block_pair(jp, carry):
        @pl.when((jp == 1) & (tile >= 1))
        def _():
            rows_out(tile - 1, nxt).wait()

        @pl.when((jp == 1) & (tile + 1 < N_TILE_FFN))
        def _():
            _ffn_rows_in(src, tile + 1, tbufs.at[nxt], in_sem.at[nxt], _start)

        step = tile * N_FSTEP + 2 * jp
        prefetch(step + W_AHEAD)
        for c in w_copies(step) + w_copies(step + 1):
            c.wait()
        block(step)
        prefetch(step + 1 + W_AHEAD)
        block(step + 1)
        return carry

    lax.fori_loop(0, N_FSTEP // 2, block_pair, 0)
    rows_out(tile, cur).start()

    @pl.when(tile == N_TILE_FFN - 1)
    def _():
        rows_out(tile, cur).wait()
        for step in range(W_AHEAD):
            for c in w_copies(N_STEP_FFN + step):
                c.wait()


def _ffn(h, gain, w_gate, w_up, w_down, layer):
    split = isinstance(h, tuple)
    hbm = pl.BlockSpec(memory_space=pl.ANY)
    return pl.pallas_call(
        functools.partial(_ffn_kernel, layer=layer, split=split),
        grid=(N_TILE_FFN,),
        in_specs=[hbm] * (2 if split else 1) + [
            pl.BlockSpec((None, 1, D_MODEL), lambda i: (layer, 0, 0)), hbm, hbm, hbm],
        out_specs=hbm,
        out_shape=jax.ShapeDtypeStruct((N_TOK, D_MODEL), F32),
        scratch_shapes=[
            pltpu.VMEM((2, TM_FFN, D_MODEL), F32),
            pltpu.VMEM((TM_FFN, D_MODEL), BF16),
            pltpu.VMEM((W_SLOTS, D_MODEL, TF), F32),
            pltpu.VMEM((W_SLOTS, D_MODEL, TF), F32),
            pltpu.VMEM((W_SLOTS, TF, D_MODEL), F32),
            pltpu.SemaphoreType.DMA((3, W_SLOTS)),
            pltpu.SemaphoreType.DMA((2,)),
            pltpu.SemaphoreType.DMA((2,)),
        ],
        compiler_params=_ARB1,
        name="ffn",
    )(*(h if split else (h,)), gain, w_gate, w_up, w_down)


def _gmlp_weight_copies(jm, win_hbm, wout_hbm, win_buf, wout_buf, sem):
    return (pltpu.make_async_copy(win_hbm.at[jm], win_buf, sem.at[0]),
            pltpu.make_async_copy(wout_hbm.at[jm], wout_buf, sem.at[1]))


def _gmlp_kernel(h_ref, g_ref, win_hbm, wout_hbm, lng_ref, lnb_ref, ws_ref, bs_ref,
                 o_ref, vo_ref,
                 win_buf, wout_buf, xn_ref, v_ref, y_ref, sem, *, jm):
    tile = pl.program_id(0)
    is_last_tile = tile == N_TILE - 1
    copies = _gmlp_weight_copies(jm, win_hbm, wout_hbm, win_buf, wout_buf, sem)

    @pl.when(tile == 0)
    def _():
        for c in copies:
            c.start()

    xn_ref[...] = _rms_bf16(h_ref[...], g_ref[...])

    @pl.when(tile == 0)
    def _():
        for c in copies:
            c.wait()

    xn = xn_ref[...]

    def head_cols(first, head):
        return slice(first + head * D_HEAD, first + (head + 1) * D_HEAD)

    v_sum = jnp.zeros((TM, 1), F32)
    for head in range(N_HEAD):
        v = _gelu_tanh(_dot(xn, win_buf[:, head_cols(E_MIX, head)]))
        v_ref[head] = v
        v_sum = v_sum + jnp.sum(v, axis=-1, keepdims=True)
    mu = v_sum * (1.0 / E_MIX)
    var = jnp.zeros((TM, 1), F32)
    for head in range(N_HEAD):
        d = v_ref[head] - mu
        var = var + jnp.sum(d * d, axis=-1, keepdims=True)
    rstd = lax.rsqrt(var * (1.0 / E_MIX) + EPS)

    row = lax.broadcasted_iota(jnp.int32, (CHUNK, CHUNK), 0)
    col = lax.broadcasted_iota(jnp.int32, (CHUNK, CHUNK), 1)
    n_chunk = TM // CHUNK
    for head in range(N_HEAD):
        u = _gelu_tanh(_dot(xn, win_buf[:, head_cols(0, head)]))
        vn = (v_ref[head] - mu) * rstd * lng_ref[head] + lnb_ref[head]

        vo_ref[head] = vn[SAMPLE_ROW0:, :]
        w = ws_ref[head]
        b = bs_ref[head]
        w_causal = jnp.where(col <= row, w, 0.0)
        w_sample = jnp.where(col == row, w[0:1, 0:1], 0.0)
        b_sample = jnp.broadcast_to(b[0:1, :], (CHUNK, 1))
        w_last = jnp.where(is_last_tile, w_sample, w_causal).astype(BF16)
        b_last = jnp.where(is_last_tile, b_sample, b)
        w_causal = w_causal.astype(BF16)
        for c in range(n_chunk):
            rows = slice(c * CHUNK, (c + 1) * CHUNK)
            wm, bm = (w_last, b_last) if c == n_chunk - 1 else (w_causal, b)
            s = _dot(wm, vn[rows, :].astype(BF16)) + bm
            y_ref[rows, head_cols(0, head)] = (u[rows, :] * s).astype(BF16)

    o_ref[...] = h_ref[...] + _dot(y_ref[...], wout_buf[...])


def _gmlp(h, gain, w_in, ln_g, ln_b, w_s, b_s, w_out, layer, jm):
    hbm = pl.BlockSpec(memory_space=pl.ANY)
    return pl.pallas_call(
        functools.partial(_gmlp_kernel, jm=jm),
        grid=(N_TILE,),
        in_specs=[
            pl.BlockSpec((TM, D_MODEL), lambda i: (i, 0)),
            pl.BlockSpec((None, 1, D_MODEL), lambda i: (layer, 0, 0)),
            hbm, hbm,
            pl.BlockSpec((None, N_HEAD, 1, D_HEAD), lambda i: (jm, 0, 0, 0)),
            pl.BlockSpec((None, N_HEAD, 1, D_HEAD), lambda i: (jm, 0, 0, 0)),
            pl.BlockSpec((None, N_HEAD, CHUNK, CHUNK), lambda i: (jm, 0, 0, 0)),
            pl.BlockSpec((None, N_HEAD, CHUNK, 1), lambda i: (jm, 0, 0, 0)),
        ],
        out_specs=[
            pl.BlockSpec((TM, D_MODEL), lambda i: (i, 0)),
            pl.BlockSpec((N_HEAD, N_SAMPLE, D_HEAD), lambda i: (0, 0, 0)),
        ],
        out_shape=[
            jax.ShapeDtypeStruct((N_TOK, D_MODEL), F32),
            jax.ShapeDtypeStruct((N_HEAD, N_SAMPLE, D_HEAD), F32),
        ],
        scratch_shapes=[
            pltpu.VMEM((D_MODEL, 2 * E_MIX), BF16),
            pltpu.VMEM((E_MIX, D_MODEL), BF16),
            pltpu.VMEM((TM, D_MODEL), BF16),
            pltpu.VMEM((N_HEAD, TM, D_HEAD), F32),
            pltpu.VMEM((TM, E_MIX), BF16),
            pltpu.SemaphoreType.DMA((2,)),
        ],
        compiler_params=_ARB1,
        name="gmlp",
    )(h, gain, w_in, w_out, ln_g, ln_b, w_s, b_s)


_HALO = 8
CB = 256
N_CB = E_MIX // CB
CONV_PAIRS = N_CB // 2
CONV_SLOTS = 4
assert CONV_PAIRS % 2 == 0

_SEQ_TAIL = [divmod((s + 1) * SEQ - _HALO, TM) for s in range(N_PROMPT // SEQ)]


def _conv_kernel(h_ref, g_ref, win_hbm, wout_hbm, wk_ref, st0_ref, st1_ref,
                 o_ref, cs_ref, cp_ref,
                 win_buf, wout_buf, xn_ref, buf_ref, carry_ref, tails_ref, y_ref,
                 sem_in, sem_out, *, jm):
    tile = pl.program_id(0)
    is_last_tile = tile == N_TILE - 1

    def win_copy(block, slot):
        return pltpu.make_async_copy(win_hbm.at[jm, block], win_buf.at[slot], sem_in.at[slot])

    wout_copy = pltpu.make_async_copy(wout_hbm.at[jm], wout_buf, sem_out.at[0])

    @pl.when(tile == 0)
    def _():
        wout_copy.start()
        win_copy(0, 0).start()
        win_copy(1, 1).start()
        carry_ref[...] = jnp.zeros_like(carry_ref)

    xn_ref[...] = _rms_bf16(h_ref[...], g_ref[...])
    xn = xn_ref[...]
    pos = (lax.broadcasted_iota(jnp.int32, (TM, 1), 0) + tile * TM) & (SEQ - 1)

    def conv_block(block, slot):
        cols = slice(block * CB, (block + 1) * CB)
        bg = _dot(xn, win_buf[slot, 0])
        ci = _dot(xn, win_buf[slot, 1]) * _dot(xn, win_buf[slot, 2])

        for s, (_, row0) in enumerate(_SEQ_TAIL):
            tails_ref[s, block] = ci[row0:row0 + _HALO, :]
        ci_s = ci[SAMPLE_ROW0:, :]
        cs_ref[block] = ci_s

        buf = buf_ref.at[block % 2]
        buf[0:_HALO, :] = carry_ref[block]
        buf[_HALO:, :] = ci
        carry_ref[block] = ci[TM - _HALO:, :]
        prev1 = jnp.where(pos >= 1, buf[_HALO - 1:_HALO - 1 + TM, :], 0.0)
        prev2 = jnp.where(pos >= 2, buf[_HALO - 2:_HALO - 2 + TM, :], 0.0)
        w0 = wk_ref[0:1, cols]
        w1 = wk_ref[1:2, cols]
        w2 = wk_ref[2:3, cols]
        co = w0 * prev2 + w1 * prev1 + w2 * ci
        co_s = w0 * st0_ref[:, cols] + w1 * st1_ref[:, cols] + w2 * ci_s
        co_tail = jnp.where(is_last_tile, co_s, co[SAMPLE_ROW0:, :])
        y_ref[0:SAMPLE_ROW0, cols] = (bg[0:SAMPLE_ROW0, :] * co[0:SAMPLE_ROW0, :]).astype(BF16)
        y_ref[SAMPLE_ROW0:, cols] = (bg[SAMPLE_ROW0:, :] * co_tail).astype(BF16)

    for pair in range(CONV_PAIRS):
        blocks = (2 * pair, 2 * pair + 1)
        slots = (0, 1) if pair % 2 == 0 else (2, 3)
        free_slots = (2, 3) if pair % 2 == 0 else (0, 1)
        for b, slot in zip(blocks, free_slots):
            win_copy((b + 2) % N_CB, slot).start()
        for b, slot in zip(blocks, slots):
            win_copy(b, slot).wait()
        for b, slot in zip(blocks, slots):
            conv_block(b, slot)

    @pl.when(tile == 0)
    def _():
        wout_copy.wait()

    o_ref[...] = h_ref[...] + _dot(y_ref[...], wout_buf[...])

    for s, (seq_tile, _) in enumerate(_SEQ_TAIL):
        @pl.when(tile == seq_tile)
        def _():
            cp_ref[s] = tails_ref[s]

    @pl.when(is_last_tile)
    def _():
        for b in (0, 1):
            win_copy(b, b).wait()


def _conv(h, gain, w_in_blocked, w_conv, st0, st1, w_out, layer, jm):
    n_seq = N_PROMPT // SEQ
    hbm = pl.BlockSpec(memory_space=pl.ANY)
    return pl.pallas_call(
        functools.partial(_conv_kernel, jm=jm),
        grid=(N_TILE,),
        in_specs=[
            pl.BlockSpec((TM, D_MODEL), lambda i: (i, 0)),
            pl.BlockSpec((None, 1, D_MODEL), lambda i: (layer, 0, 0)),
            hbm, hbm,
            pl.BlockSpec((None, 3, E_MIX), lambda i: (jm, 0, 0)),
            pl.BlockSpec((N_SAMPLE, E_MIX), lambda i: (0, 0)),
            pl.BlockSpec((N_SAMPLE, E_MIX), lambda i: (0, 0)),
        ],
        out_specs=[
            pl.BlockSpec((TM, D_MODEL), lambda i: (i, 0)),
            pl.BlockSpec((N_CB, N_SAMPLE, CB), lambda i: (0, 0, 0)),
            pl.BlockSpec((n_seq, N_CB, _HALO, CB), lambda i: (0, 0, 0, 0)),
        ],
        out_shape=[
            jax.ShapeDtypeStruct((N_TOK, D_MODEL), F32),
            jax.ShapeDtypeStruct((N_CB, N_SAMPLE, CB), F32),
            jax.ShapeDtypeStruct((n_seq, N_CB, _HALO, CB), F32),
        ],
        scratch_shapes=[
            pltpu.VMEM((CONV_SLOTS, 3, D_MODEL, CB), BF16),
            pltpu.VMEM((E_MIX, D_MODEL), BF16),
            pltpu.VMEM((TM, D_MODEL), BF16),
            pltpu.VMEM((2, TM + _HALO, CB), F32),
            pltpu.VMEM((N_CB, _HALO, CB), F32),
            pltpu.VMEM((n_seq, N_CB, _HALO, CB), F32),
            pltpu.VMEM((TM, E_MIX), BF16),
            pltpu.SemaphoreType.DMA((CONV_SLOTS,)),
            pltpu.SemaphoreType.DMA((1,)),
        ],
        compiler_params=_ARB1,
        name="sconv",
    )(h, gain, w_in_blocked, w_out, w_conv, st0, st1)


def _ple_kernel(h_ref, g_ref, wg_ref, pp_ref, ps_ref, wp_ref, gf_ref, o_ref, *rest, final_norm):
    if final_norm:
        os_ref, pb_ref = rest
    else:
        (pb_ref,) = rest
    i = pl.program_id(0)

    @pl.when(i < N_TILE - 1)
    def _():
        pb_ref[...] = pp_ref[...].astype(BF16)

    @pl.when(i == N_TILE - 1)
    def _():
        pb_ref[0:SAMPLE_ROW0, :] = pp_ref[0:SAMPLE_ROW0, :].astype(BF16)
        pb_ref[SAMPLE_ROW0:, :] = ps_ref[...].astype(BF16)

    xn = _rms_bf16(h_ref[...], g_ref[...])
    pb = pb_ref[...]
    for c in range(D_MODEL // TN):
        cols = slice(c * TN, (c + 1) * TN)
        gate = jax.nn.sigmoid(_dot(xn, wg_ref[:, cols]))
        o_ref[:, cols] = h_ref[:, cols] + gate * _dot(pb, wp_ref[:, cols])
    if final_norm:
        hn = o_ref[...]
        ms = jnp.mean(hn * hn, axis=-1, keepdims=True)
        o_ref[...] = hn * lax.rsqrt(ms + EPS) * gf_ref[...]

        @pl.when(i == N_TILE - 1)
        def _():
            os_ref[...] = o_ref[SAMPLE_ROW0:, :]


def _ple(h, gain, w_gate, p_prompt, p_sample, w_proj, g_final, layer, final_norm):
    row_block = pl.BlockSpec((TM, D_MODEL), lambda i: (i, 0))
    if final_norm:
        out_specs = [row_block, pl.BlockSpec((N_SAMPLE, D_MODEL), lambda i: (0, 0))]
        out_shape = [jax.ShapeDtypeStruct((N_PROMPT, D_MODEL), F32),
                     jax.ShapeDtypeStruct((N_SAMPLE, D_MODEL), F32)]
    else:
        out_specs = row_block
        out_shape = jax.ShapeDtypeStruct((N_TOK, D_MODEL), F32)
    return pl.pallas_call(
        functools.partial(_ple_kernel, final_norm=final_norm),
        grid=(N_TILE,),
        in_specs=[
            row_block,
            pl.BlockSpec((None, 1, D_MODEL), lambda i: (layer, 0, 0)),
            pl.BlockSpec((None, D_MODEL, D_MODEL), lambda i: (layer, 0, 0)),
            pl.BlockSpec((None, TM, D_PLE), lambda i: (layer, i, 0)),
            pl.BlockSpec((None, N_SAMPLE, D_PLE), lambda i: (layer, 0, 0)),
            pl.BlockSpec((None, D_PLE, D_MODEL), lambda i: (layer, 0, 0)),
            pl.BlockSpec((1, D_MODEL), lambda i: (0, 0)),
        ],
        out_specs=out_specs,
        out_shape=out_shape,
        scratch_shapes=[pltpu.VMEM((TM, D_PLE), BF16)],
        compiler_params=_ARB1,
        name="ple",
    )(h, gain, w_gate, p_prompt, p_sample, w_proj, g_final)


def kernel(x_prompt, x_sample, state_conv, p_prompt, p_sample, ffn1_norm, ffn1_w_gate, ffn1_w_up, ffn1_w_down, mix_norm, a_w_in, a_ln_g, a_ln_b, a_w_s, a_b_s, a_w_out, c_w_in, c_w_conv, c_w_out, ffn2_norm, ffn2_w_gate, ffn2_w_up, ffn2_w_down, ple_norm, ple_w_gate, ple_w_proj, final_norm):
    bf = lambda w: w.astype(BF16)
    gain3 = lambda g: g.reshape(g.shape[0], 1, D_MODEL)

    h = (x_prompt.reshape(N_PROMPT, D_MODEL), x_sample.reshape(N_SAMPLE, D_MODEL))
    p_prompt = p_prompt.reshape(DEPTH, N_PROMPT, D_PLE)
    p_sample = p_sample.reshape(DEPTH, N_SAMPLE, D_PLE)

    f1 = (gain3(ffn1_norm), ffn1_w_gate, ffn1_w_up, ffn1_w_down)
    f2 = (gain3(ffn2_norm), ffn2_w_gate, ffn2_w_up, ffn2_w_down)
    mix_gain = gain3(mix_norm)
    ple_gain = gain3(ple_norm)
    a_w_in_b, a_w_out_b = bf(a_w_in), bf(a_w_out)
    c_w_in_b = bf(c_w_in).reshape(-1, D_MODEL, 3, N_CB, CB).transpose(0, 3, 2, 1, 4)
    c_w_out_b = bf(c_w_out)
    ple_w_gate_b, ple_w_proj_b = bf(ple_w_gate), bf(ple_w_proj)
    ln_g = a_ln_g.reshape(-1, N_HEAD, 1, D_HEAD)
    ln_b = a_ln_b.reshape(-1, N_HEAD, 1, D_HEAD)
    b_s = a_b_s.reshape(-1, N_HEAD, CHUNK, 1)
    g_final = final_norm.reshape(1, D_MODEL)

    conv_new_prompt, conv_new_sample, v_new = [], [], []
    for layer in range(DEPTH):
        h = _ffn(h, *f1, layer)
        jm = layer // 2
        if layer % 2 == 0:
            h, v_s = _gmlp(h, mix_gain, a_w_in_b, ln_g, ln_b, a_w_s, b_s, a_w_out_b, layer, jm)
            v_new.append(v_s.transpose(1, 0, 2).reshape(N_SAMPLE, 1, E_MIX))
        else:
            st0 = state_conv[jm, :, 0, :]
            st1 = state_conv[jm, :, 1, :]
            h, ci_s, ci_p = _conv(h, mix_gain, c_w_in_b, c_w_conv, st0, st1, c_w_out_b, layer, jm)
            ci_s = ci_s.transpose(1, 0, 2).reshape(N_SAMPLE, E_MIX)
            ci_p = ci_p.transpose(0, 2, 1, 3).reshape(N_PROMPT // SEQ, _HALO, E_MIX)
            conv_new_prompt.append(ci_p[:, _HALO - 2:, :])
            conv_new_sample.append(jnp.stack([st1, ci_s], axis=1))
        h = _ffn(h, *f2, layer)
        h = _ple(h, ple_gain, ple_w_gate_b, p_prompt, p_sample, ple_w_proj_b, g_final, layer,
                 final_norm=(layer == DEPTH - 1))

    y_prompt, y_sample = h
    return (y_prompt.reshape(N_PROMPT // SEQ, SEQ, D_MODEL), y_sample.reshape(N_SAMPLE, 1, D_MODEL),
            jnp.stack(conv_new_prompt), jnp.stack(conv_new_sample), jnp.stack(v_new))
```

```python
import functools

import jax
import jax.numpy as jnp
from jax import lax
from jax.experimental import pallas as pl
from jax.experimental.pallas import tpu as pltpu

F32 = jnp.float32
BF16 = jnp.bfloat16

D_MODEL = 2048
D_FF = 5632
E_MIX = 2048
N_HEAD = 8
D_HEAD = 256
CHUNK = 128
D_PLE = 256
SEQ = 2048
N_PROMPT = 4 * SEQ
N_SAMPLE = 128
N_TOK = N_PROMPT + N_SAMPLE
DEPTH = 2
EPS = 1e-6

TM = 640
N_TILE = N_TOK // TM
SAMPLE_ROW0 = TM - N_SAMPLE
TM_FFN = 1664
N_TILE_FFN = N_TOK // TM_FFN
SAMPLE_ROW0_FFN = TM_FFN - N_SAMPLE
TF = 256
TN = 512
V7X_VMEM_BYTES = 64 * 1024 * 1024
VMEM_LIMIT = V7X_VMEM_BYTES - 2 * 1024 * 1024

_ARB1 = pltpu.CompilerParams(dimension_semantics=("arbitrary",),
                             vmem_limit_bytes=VMEM_LIMIT)


def _rms_bf16(x, g):
    ms = jnp.mean(x * x, axis=-1, keepdims=True)
    return (x * lax.rsqrt(ms + EPS) * g).astype(BF16)


def _dot(a, b):
    return jnp.dot(a, b, preferred_element_type=F32)


def _gelu_tanh(x):
    c = 0.7978845608028654
    return x * (0.5 * (1.0 + jnp.tanh(c * (x + 0.044715 * (x * x * x)))))


N_FSTEP = D_FF // TF
N_STEP_FFN = N_TILE_FFN * N_FSTEP
W_SLOTS = 3
W_AHEAD = W_SLOTS - 1
assert N_FSTEP % 2 == 0


def _ffn_weight_copies(layer, step, w_hbm, w_buf, sem):
    wg_hbm, wu_hbm, wd_hbm = w_hbm
    wg_buf, wu_buf, wd_buf = w_buf
    slot = step % W_SLOTS
    cols = pl.ds(pl.multiple_of((step % N_FSTEP) * TF, TF), TF)
    return (
        pltpu.make_async_copy(wg_hbm.at[layer, :, cols], wg_buf.at[slot], sem.at[0, slot]),
        pltpu.make_async_copy(wu_hbm.at[layer, :, cols], wu_buf.at[slot], sem.at[1, slot]),
        pltpu.make_async_copy(wd_hbm.at[layer, cols, :], wd_buf.at[slot], sem.at[2, slot]),
    )


def _ffn_rows_in(src, tile, buf, sem, action):
    row0 = tile * TM_FFN
    rows = pl.ds(row0 if isinstance(row0, int) else pl.multiple_of(row0, 8), TM_FFN)
    if not isinstance(src, tuple):
        action(pltpu.make_async_copy(src.at[rows], buf, sem))
        return
    xp_hbm, xs_hbm = src

    @pl.when(tile < N_TILE_FFN - 1)
    def _():
        action(pltpu.make_async_copy(xp_hbm.at[rows], buf, sem))

    @pl.when(tile == N_TILE_FFN - 1)
    def _():
        tail = pl.ds((N_TILE_FFN - 1) * TM_FFN, SAMPLE_ROW0_FFN)
        action(pltpu.make_async_copy(xp_hbm.at[tail], buf.at[pl.ds(0, SAMPLE_ROW0_FFN)], sem))
        action(pltpu.make_async_copy(xs_hbm, buf.at[pl.ds(SAMPLE_ROW0_FFN, N_SAMPLE)], sem))


def _start(copy):
    copy.start()


def _wait(copy):
    copy.wait()


def _ffn_kernel(*refs, layer, split):
    if split:
        xp_hbm, xs_hbm, *refs = refs
        src = (xp_hbm, xs_hbm)
    else:
        src, *refs = refs
    (g_ref, wg_hbm, wu_hbm, wd_hbm, o_hbm,
     tbufs, xn_ref, wg_buf, wu_buf, wd_buf, w_sem, in_sem, out_sem) = refs
    tile = pl.program_id(0)
    cur = tile % 2
    nxt = 1 - cur
    tbuf = tbufs.at[cur]
    w_copies = functools.partial(_ffn_weight_copies, layer, w_hbm=(wg_hbm, wu_hbm, wd_hbm),
                                 w_buf=(wg_buf, wu_buf, wd_buf), sem=w_sem)

    def rows_out(t, slot):
        rows = pl.ds(pl.multiple_of(t * TM_FFN, 8), TM_FFN)
        return pltpu.make_async_copy(tbufs.at[slot], o_hbm.at[rows], out_sem.at[slot])

    @pl.when(tile == 0)
    def _():
        for step in range(W_AHEAD):
            for c in w_copies(step):
                c.start()
        _ffn_rows_in(src, 0, tbufs.at[0], in_sem.at[0], _start)

    _ffn_rows_in(src, tile, tbuf, in_sem.at[cur], _wait)
    xn_ref[...] = _rms_bf16(tbuf[...], g_ref[...])

    def prefetch(step):
        for c in w_copies(step):
            c.start()

    def block(step):
        slot = step % W_SLOTS
        xn = xn_ref[...]
        gt = _dot(xn, wg_buf[slot].astype(BF16))
        up = _dot(xn, wu_buf[slot].astype(BF16))
        hd = (gt * jax.nn.sigmoid(gt)) * up * 0.5
        tbuf[...] += _dot(hd.astype(BF16), wd_buf[slot].astype(BF16))

    def block_pair(jp, carry):
        @pl.when((jp == 1) & (tile >= 1))
        def _():
            rows_out(tile - 1, nxt).wait()

        @pl.when((jp == 1) & (tile + 1 < N_TILE_FFN))
        def _():
            _ffn_rows_in(src, tile + 1, tbufs.at[nxt], in_sem.at[nxt], _start)

        step = tile * N_FSTEP + 2 * jp
        prefetch(step + W_AHEAD)
        for c in w_copies(step) + w_copies(step + 1):
            c.wait()
        block(step)
        prefetch(step + 1 + W_AHEAD)
        block(step + 1)
        return carry

    lax.fori_loop(0, N_FSTEP // 2, block_pair, 0)
    rows_out(tile, cur).start()

    @pl.when(tile == N_TILE_FFN - 1)
    def _():
        rows_out(tile, cur).wait()
        for step in range(W_AHEAD):
            for c in w_copies(N_STEP_FFN + step):
                c.wait()


def _ffn(h, gain, w_gate, w_up, w_down, layer):
    split = isinstance(h, tuple)
    hbm = pl.BlockSpec(memory_space=pl.ANY)
    return pl.pallas_call(
        functools.partial(_ffn_kernel, layer=layer, split=split),
        grid=(N_TILE_FFN,),
        in_specs=[hbm] * (2 if split else 1) + [
            pl.BlockSpec((None, 1, D_MODEL), lambda i: (layer, 0, 0)), hbm, hbm, hbm],
        out_specs=hbm,
        out_shape=jax.ShapeDtypeStruct((N_TOK, D_MODEL), F32),
        scratch_shapes=[
            pltpu.VMEM((2, TM_FFN, D_MODEL), F32),
            pltpu.VMEM((TM_FFN, D_MODEL), BF16),
            pltpu.VMEM((W_SLOTS, D_MODEL, TF), F32),
            pltpu.VMEM((W_SLOTS, D_MODEL, TF), F32),
            pltpu.VMEM((W_SLOTS, TF, D_MODEL), F32),
            pltpu.SemaphoreType.DMA((3, W_SLOTS)),
            pltpu.SemaphoreType.DMA((2,)),
            pltpu.SemaphoreType.DMA((2,)),
        ],
        compiler_params=_ARB1,
        name="ffn",
    )(*(h if split else (h,)), gain, w_gate, w_up, w_down)


def _gmlp_weight_copies(jm, win_hbm, wout_hbm, win_buf, wout_buf, sem):
    return (pltpu.make_async_copy(win_hbm.at[jm], win_buf, sem.at[0]),
            pltpu.make_async_copy(wout_hbm.at[jm], wout_buf, sem.at[1]))


def _gmlp_kernel(h_ref, g_ref, win_hbm, wout_hbm, lng_ref, lnb_ref, ws_ref, bs_ref,
                 o_ref, vo_ref,
                 win_buf, wout_buf, xn_ref, v_ref, y_ref, sem, *, jm):
    tile = pl.program_id(0)
    is_last_tile = tile == N_TILE - 1
    copies = _gmlp_weight_copies(jm, win_hbm, wout_hbm, win_buf, wout_buf, sem)

    @pl.when(tile == 0)
    def _():
        for c in copies:
            c.start()

    xn_ref[...] = _rms_bf16(h_ref[...], g_ref[...])

    @pl.when(tile == 0)
    def _():
        for c in copies:
            c.wait()

    xn = xn_ref[...]

    def head_cols(first, head):
        return slice(first + head * D_HEAD, first + (head + 1) * D_HEAD)

    v_sum = jnp.zeros((TM, 1), F32)
    for head in range(N_HEAD):
        v = _gelu_tanh(_dot(xn, win_buf[:, head_cols(E_MIX, head)]))
        v_ref[head] = v
        v_sum = v_sum + jnp.sum(v, axis=-1, keepdims=True)
    mu = v_sum * (1.0 / E_MIX)
    var = jnp.zeros((TM, 1), F32)
    for head in range(N_HEAD):
        d = v_ref[head] - mu
        var = var + jnp.sum(d * d, axis=-1, keepdims=True)
    rstd = lax.rsqrt(var * (1.0 / E_MIX) + EPS)

    row = lax.broadcasted_iota(jnp.int32, (CHUNK, CHUNK), 0)
    col = lax.broadcasted_iota(jnp.int32, (CHUNK, CHUNK), 1)
    n_chunk = TM // CHUNK
    for head in range(N_HEAD):
        u = _gelu_tanh(_dot(xn, win_buf[:, head_cols(0, head)]))
        vn = (v_ref[head] - mu) * rstd * lng_ref[head] + lnb_ref[head]

        vo_ref[head] = vn[SAMPLE_ROW0:, :]
        w = ws_ref[head]
        b = bs_ref[head]
        w_causal = jnp.where(col <= row, w, 0.0)
        w_sample = jnp.where(col == row, w[0:1, 0:1], 0.0)
        b_sample = jnp.broadcast_to(b[0:1, :], (CHUNK, 1))
        w_last = jnp.where(is_last_tile, w_sample, w_causal).astype(BF16)
        b_last = jnp.where(is_last_tile, b_sample, b)
        w_causal = w_causal.astype(BF16)
        for c in range(n_chunk):
            rows = slice(c * CHUNK, (c + 1) * CHUNK)
            wm, bm = (w_last, b_last) if c == n_chunk - 1 else (w_causal, b)
            s = _dot(wm, vn[rows, :].astype(BF16)) + bm
            y_ref[rows, head_cols(0, head)] = (u[rows, :] * s).astype(BF16)

    o_ref[...] = h_ref[...] + _dot(y_ref[...], wout_buf[...])


def _gmlp(h, gain, w_in, ln_g, ln_b, w_s, b_s, w_out, layer, jm):
    hbm = pl.BlockSpec(memory_space=pl.ANY)
    return pl.pallas_call(
        functools.partial(_gmlp_kernel, jm=jm),
        grid=(N_TILE,),
        in_specs=[
            pl.BlockSpec((TM, D_MODEL), lambda i: (i, 0)),
            pl.BlockSpec((None, 1, D_MODEL), lambda i: (layer, 0, 0)),
            hbm, hbm,
            pl.BlockSpec((None, N_HEAD, 1, D_HEAD), lambda i: (jm, 0, 0, 0)),
            pl.BlockSpec((None, N_HEAD, 1, D_HEAD), lambda i: (jm, 0, 0, 0)),
            pl.BlockSpec((None, N_HEAD, CHUNK, CHUNK), lambda i: (jm, 0, 0, 0)),
            pl.BlockSpec((None, N_HEAD, CHUNK, 1), lambda i: (jm, 0, 0, 0)),
        ],
        out_specs=[
            pl.BlockSpec((TM, D_MODEL), lambda i: (i, 0)),
            pl.BlockSpec((N_HEAD, N_SAMPLE, D_HEAD), lambda i: (0, 0, 0)),
        ],
        out_shape=[
            jax.ShapeDtypeStruct((N_TOK, D_MODEL), F32),
            jax.ShapeDtypeStruct((N_HEAD, N_SAMPLE, D_HEAD), F32),
        ],
        scratch_shapes=[
            pltpu.VMEM((D_MODEL, 2 * E_MIX), BF16),
            pltpu.VMEM((E_MIX, D_MODEL), BF16),
            pltpu.VMEM((TM, D_MODEL), BF16),
            pltpu.VMEM((N_HEAD, TM, D_HEAD), F32),
            pltpu.VMEM((TM, E_MIX), BF16),
            pltpu.SemaphoreType.DMA((2,)),
        ],
        compiler_params=_ARB1,
        name="gmlp",
    )(h, gain, w_in, w_out, ln_g, ln_b, w_s, b_s)


_HALO = 8
CB = 256
N_CB = E_MIX // CB
CONV_PAIRS = N_CB // 2
CONV_SLOTS = 4
assert CONV_PAIRS % 2 == 0

_SEQ_TAIL = [divmod((s + 1) * SEQ - _HALO, TM) for s in range(N_PROMPT // SEQ)]


def _conv_kernel(h_ref, g_ref, win_hbm, wout_hbm, wk_ref, st0_ref, st1_ref,
                 o_ref, cs_ref, cp_ref,
                 win_buf, wout_buf, xn_ref, buf_ref, carry_ref, tails_ref, y_ref,
                 sem_in, sem_out, *, jm):
    tile = pl.program_id(0)
    is_last_tile = tile == N_TILE - 1

    def win_copy(block, slot):
        return pltpu.make_async_copy(win_hbm.at[jm, block], win_buf.at[slot], sem_in.at[slot])

    wout_copy = pltpu.make_async_copy(wout_hbm.at[jm], wout_buf, sem_out.at[0])

    @pl.when(tile == 0)
    def _():
        wout_copy.start()
        win_copy(0, 0).start()
        win_copy(1, 1).start()
        carry_ref[...] = jnp.zeros_like(carry_ref)

    xn_ref[...] = _rms_bf16(h_ref[...], g_ref[...])
    xn = xn_ref[...]
    pos = (lax.broadcasted_iota(jnp.int32, (TM, 1), 0) + tile * TM) & (SEQ - 1)

    def conv_block(block, slot):
        cols = slice(block * CB, (block + 1) * CB)
        bg = _dot(xn, win_buf[slot, 0])
        ci = _dot(xn, win_buf[slot, 1]) * _dot(xn, win_buf[slot, 2])

        for s, (_, row0) in enumerate(_SEQ_TAIL):
            tails_ref[s, block] = ci[row0:row0 + _HALO, :]
        ci_s = ci[SAMPLE_ROW0:, :]
        cs_ref[block] = ci_s

        buf = buf_ref.at[block % 2]
        buf[0:_HALO, :] = carry_ref[block]
        buf[_HALO:, :] = ci
        carry_ref[block] = ci[TM - _HALO:, :]
        prev1 = jnp.where(pos >= 1, buf[_HALO - 1:_HALO - 1 + TM, :], 0.0)
        prev2 = jnp.where(pos >= 2, buf[_HALO - 2:_HALO - 2 + TM, :], 0.0)
        w0 = wk_ref[0:1, cols]
        w1 = wk_ref[1:2, cols]
        w2 = wk_ref[2:3, cols]
        co = w0 * prev2 + w1 * prev1 + w2 * ci
        co_s = w0 * st0_ref[:, cols] + w1 * st1_ref[:, cols] + w2 * ci_s
        co_tail = jnp.where(is_last_tile, co_s, co[SAMPLE_ROW0:, :])
        y_ref[0:SAMPLE_ROW0, cols] = (bg[0:SAMPLE_ROW0, :] * co[0:SAMPLE_ROW0, :]).astype(BF16)
        y_ref[SAMPLE_ROW0:, cols] = (bg[SAMPLE_ROW0:, :] * co_tail).astype(BF16)

    for pair in range(CONV_PAIRS):
        blocks = (2 * pair, 2 * pair + 1)
        slots = (0, 1) if pair % 2 == 0 else (2, 3)
        free_slots = (2, 3) if pair % 2 == 0 else (0, 1)
        for b, slot in zip(blocks, free_slots):
            win_copy((b + 2) % N_CB, slot).start()
        for b, slot in zip(blocks, slots):
            win_copy(b, slot).wait()
        for b, slot in zip(blocks, slots):
            conv_block(b, slot)

    @pl.when(tile == 0)
    def _():
        wout_copy.wait()

    o_ref[...] = h_ref[...] + _dot(y_ref[...], wout_buf[...])

    for s, (seq_tile, _) in enumerate(_SEQ_TAIL):
        @pl.when(tile == seq_tile)
        def _():
            cp_ref[s] = tails_ref[s]

    @pl.when(is_last_tile)
    def _():
        for b in (0, 1):
            win_copy(b, b).wait()


def _conv_w_in_blocks_kernel(wb_ref, wc_ref, wx_ref, o_ref):
    for part, w_ref in enumerate((wb_ref, wc_ref, wx_ref)):
        o_ref[part] = w_ref[...].astype(BF16)


def _conv_w_in_blocks(w_in):
    n_b = w_in.shape[0]

    def part_spec(part):
        return pl.BlockSpec((None, D_MODEL, CB), lambda m, b: (m, 0, part * N_CB + b))

    return pl.pallas_call(
        _conv_w_in_blocks_kernel,
        grid=(n_b, N_CB),
        in_specs=[part_spec(0), part_spec(1), part_spec(2)],
        out_specs=pl.BlockSpec((None, None, 3, D_MODEL, CB), lambda m, b: (m, b, 0, 0, 0)),
        out_shape=jax.ShapeDtypeStruct((n_b, N_CB, 3, D_MODEL, CB), BF16),
        compiler_params=pltpu.CompilerParams(dimension_semantics=("arbitrary", "arbitrary")),
        name="conv_w_in_blocks",
    )(w_in, w_in, w_in)


def _conv(h, gain, w_in_blocked, w_conv, st0, st1, w_out, layer, jm):
    n_seq = N_PROMPT // SEQ
    hbm = pl.BlockSpec(memory_space=pl.ANY)
    return pl.pallas_call(
        functools.partial(_conv_kernel, jm=jm),
        grid=(N_TILE,),
        in_specs=[
            pl.BlockSpec((TM, D_MODEL), lambda i: (i, 0)),
            pl.BlockSpec((None, 1, D_MODEL), lambda i: (layer, 0, 0)),
            hbm, hbm,
            pl.BlockSpec((None, 3, E_MIX), lambda i: (jm, 0, 0)),
            pl.BlockSpec((N_SAMPLE, E_MIX), lambda i: (0, 0)),
            pl.BlockSpec((N_SAMPLE, E_MIX), lambda i: (0, 0)),
        ],
        out_specs=[
            pl.BlockSpec((TM, D_MODEL), lambda i: (i, 0)),
            pl.BlockSpec((N_CB, N_SAMPLE, CB), lambda i: (0, 0, 0)),
            pl.BlockSpec((n_seq, N_CB, _HALO, CB), lambda i: (0, 0, 0, 0)),
        ],
        out_shape=[
            jax.ShapeDtypeStruct((N_TOK, D_MODEL), F32),
            jax.ShapeDtypeStruct((N_CB, N_SAMPLE, CB), F32),
            jax.ShapeDtypeStruct((n_seq, N_CB, _HALO, CB), F32),
        ],
        scratch_shapes=[
            pltpu.VMEM((CONV_SLOTS, 3, D_MODEL, CB), BF16),
            pltpu.VMEM((E_MIX, D_MODEL), BF16),
            pltpu.VMEM((TM, D_MODEL), BF16),
            pltpu.VMEM((2, TM + _HALO, CB), F32),
            pltpu.VMEM((N_CB, _HALO, CB), F32),
            pltpu.VMEM((n_seq, N_CB, _HALO, CB), F32),
            pltpu.VMEM((TM, E_MIX), BF16),
            pltpu.SemaphoreType.DMA((CONV_SLOTS,)),
            pltpu.SemaphoreType.DMA((1,)),
        ],
        compiler_params=_ARB1,
        name="sconv",
    )(h, gain, w_in_blocked, w_out, w_conv, st0, st1)


def _ple_kernel(h_ref, g_ref, wg_ref, pp_ref, ps_ref, wp_ref, gf_ref, o_ref, *rest, final_norm):
    if final_norm:
        os_ref, pb_ref = rest
    else:
        (pb_ref,) = rest
    i = pl.program_id(0)

    @pl.when(i < N_TILE - 1)
    def _():
        pb_ref[...] = pp_ref[...].astype(BF16)

    @pl.when(i == N_TILE - 1)
    def _():
        pb_ref[0:SAMPLE_ROW0, :] = pp_ref[0:SAMPLE_ROW0, :].astype(BF16)
        pb_ref[SAMPLE_ROW0:, :] = ps_ref[...].astype(BF16)

    xn = _rms_bf16(h_ref[...], g_ref[...])
    pb = pb_ref[...]
    for c in range(D_MODEL // TN):
        cols = slice(c * TN, (c + 1) * TN)
        gate = jax.nn.sigmoid(_dot(xn, wg_ref[:, cols]))
        o_ref[:, cols] = h_ref[:, cols] + gate * _dot(pb, wp_ref[:, cols])
    if final_norm:
        hn = o_ref[...]
        ms = jnp.mean(hn * hn, axis=-1, keepdims=True)
        o_ref[...] = hn * lax.rsqrt(ms + EPS) * gf_ref[...]

        @pl.when(i == N_TILE - 1)
        def _():
            os_ref[...] = o_ref[SAMPLE_ROW0:, :]


def _ple(h, gain, w_gate, p_prompt, p_sample, w_proj, g_final, layer, final_norm):
    row_block = pl.BlockSpec((TM, D_MODEL), lambda i: (i, 0))
    if final_norm:
        out_specs = [row_block, pl.BlockSpec((N_SAMPLE, D_MODEL), lambda i: (0, 0))]
        out_shape = [jax.ShapeDtypeStruct((N_PROMPT, D_MODEL), F32),
                     jax.ShapeDtypeStruct((N_SAMPLE, D_MODEL), F32)]
    else:
        out_specs = row_block
        out_shape = jax.ShapeDtypeStruct((N_TOK, D_MODEL), F32)
    return pl.pallas_call(
        functools.partial(_ple_kernel, final_norm=final_norm),
        grid=(N_TILE,),
        in_specs=[
            row_block,
            pl.BlockSpec((None, 1, D_MODEL), lambda i: (layer, 0, 0)),
            pl.BlockSpec((None, D_MODEL, D_MODEL), lambda i: (layer, 0, 0)),
            pl.BlockSpec((None, TM, D_PLE), lambda i: (layer, i, 0)),
            pl.BlockSpec((None, N_SAMPLE, D_PLE), lambda i: (layer, 0, 0)),
            pl.BlockSpec((None, D_PLE, D_MODEL), lambda i: (layer, 0, 0)),
            pl.BlockSpec((1, D_MODEL), lambda i: (0, 0)),
        ],
        out_specs=out_specs,
        out_shape=out_shape,
        scratch_shapes=[pltpu.VMEM((TM, D_PLE), BF16)],
        compiler_params=_ARB1,
        name="ple",
    )(h, gain, w_gate, p_prompt, p_sample, w_proj, g_final)


def kernel(x_prompt, x_sample, state_conv, p_prompt, p_sample, ffn1_norm, ffn1_w_gate, ffn1_w_up, ffn1_w_down, mix_norm, a_w_in, a_ln_g, a_ln_b, a_w_s, a_b_s, a_w_out, c_w_in, c_w_conv, c_w_out, ffn2_norm, ffn2_w_gate, ffn2_w_up, ffn2_w_down, ple_norm, ple_w_gate, ple_w_proj, final_norm):
    bf = lambda w: w.astype(BF16)
    gain3 = lambda g: g.reshape(g.shape[0], 1, D_MODEL)

    h = (x_prompt.reshape(N_PROMPT, D_MODEL), x_sample.reshape(N_SAMPLE, D_MODEL))
    p_prompt = p_prompt.reshape(DEPTH, N_PROMPT, D_PLE)
    p_sample = p_sample.reshape(DEPTH, N_SAMPLE, D_PLE)

    f1 = (gain3(ffn1_norm), ffn1_w_gate, ffn1_w_up, ffn1_w_down)
    f2 = (gain3(ffn2_norm), ffn2_w_gate, ffn2_w_up, ffn2_w_down)
    mix_gain = gain3(mix_norm)
    ple_gain = gain3(ple_norm)
    a_w_in_b, a_w_out_b = bf(a_w_in), bf(a_w_out)
    c_w_in_b = _conv_w_in_blocks(c_w_in)
    c_w_out_b = bf(c_w_out)
    ple_w_gate_b, ple_w_proj_b = bf(ple_w_gate), bf(ple_w_proj)
    ln_g = a_ln_g.reshape(-1, N_HEAD, 1, D_HEAD)
    ln_b = a_ln_b.reshape(-1, N_HEAD, 1, D_HEAD)
    b_s = a_b_s.reshape(-1, N_HEAD, CHUNK, 1)
    g_final = final_norm.reshape(1, D_MODEL)

    conv_new_prompt, conv_new_sample, v_new = [], [], []
    for layer in range(DEPTH):
        h = _ffn(h, *f1, layer)
        jm = layer // 2
        if layer % 2 == 0:
            h, v_s = _gmlp(h, mix_gain, a_w_in_b, ln_g, ln_b, a_w_s, b_s, a_w_out_b, layer, jm)
            v_new.append(v_s.transpose(1, 0, 2).reshape(N_SAMPLE, 1, E_MIX))
        else:
            st0 = state_conv[jm, :, 0, :]
            st1 = state_conv[jm, :, 1, :]
            h, ci_s, ci_p = _conv(h, mix_gain, c_w_in_b, c_w_conv, st0, st1, c_w_out_b, layer, jm)
            ci_s = ci_s.transpose(1, 0, 2).reshape(N_SAMPLE, E_MIX)
            ci_p = ci_p.transpose(0, 2, 1, 3).reshape(N_PROMPT // SEQ, _HALO, E_MIX)
            conv_new_prompt.append(ci_p[:, _HALO - 2:, :])
            conv_new_sample.append(jnp.stack([st1, ci_s], axis=1))
        h = _ffn(h, *f2, layer)
        h = _ple(h, ple_gain, ple_w_gate_b, p_prompt, p_sample, ple_w_proj_b, g_final, layer,
                 final_norm=(layer == DEPTH - 1))

    y_prompt, y_sample = h
    return (y_prompt.reshape(N_PROMPT // SEQ, SEQ, D_MODEL), y_sample.reshape(N_SAMPLE, 1, D_MODEL),
            jnp.stack(conv_new_prompt), jnp.stack(conv_new_sample), jnp.stack(v_new))
```

```python
import functools

import jax
import jax.numpy as jnp
from jax import lax
from jax.experimental import pallas as pl
from jax.experimental.pallas import tpu as pltpu

F32 = jnp.float32
BF16 = jnp.bfloat16

D_MODEL = 2048
D_FF = 5632
E_MIX = 2048
N_HEAD = 8
D_HEAD = 256
CHUNK = 128
D_PLE = 256
SEQ = 2048
N_PROMPT = 4 * SEQ
N_SAMPLE = 128
N_TOK = N_PROMPT + N_SAMPLE
DEPTH = 2
EPS = 1e-6

TM = 640
N_TILE = N_TOK // TM
SAMPLE_ROW0 = TM - N_SAMPLE
TM_FFN = 1664
N_TILE_FFN = N_TOK // TM_FFN
SAMPLE_ROW0_FFN = TM_FFN - N_SAMPLE
TF = 256
TN = 512
V7X_VMEM_BYTES = 64 * 1024 * 1024
VMEM_LIMIT = V7X_VMEM_BYTES - 2 * 1024 * 1024

_ARB1 = pltpu.CompilerParams(dimension_semantics=("arbitrary",),
                             vmem_limit_bytes=VMEM_LIMIT)


def _rms_bf16(x, g):
    ms = jnp.mean(x * x, axis=-1, keepdims=True)
    return (x * lax.rsqrt(ms + EPS) * g).astype(BF16)


def _dot(a, b):
    return jnp.dot(a, b, preferred_element_type=F32)


def _gelu_tanh(x):
    c = 0.7978845608028654
    return x * (0.5 * (1.0 + jnp.tanh(c * (x + 0.044715 * (x * x * x)))))


N_FSTEP = D_FF // TF
N_STEP_FFN = N_TILE_FFN * N_FSTEP
W_SLOTS = 3
W_AHEAD = W_SLOTS - 1
assert N_FSTEP % 2 == 0


def _ffn_weight_copies(layer, step, w_hbm, w_buf, sem):
    wg_hbm, wu_hbm, wd_hbm = w_hbm
    wg_buf, wu_buf, wd_buf = w_buf
    slot = step % W_SLOTS
    cols = pl.ds(pl.multiple_of((step % N_FSTEP) * TF, TF), TF)
    return (
        pltpu.make_async_copy(wg_hbm.at[layer, :, cols], wg_buf.at[slot], sem.at[0, slot]),
        pltpu.make_async_copy(wu_hbm.at[layer, :, cols], wu_buf.at[slot], sem.at[1, slot]),
        pltpu.make_async_copy(wd_hbm.at[layer, cols, :], wd_buf.at[slot], sem.at[2, slot]),
    )


def _ffn_rows_in(src, tile, buf, sem, action):
    row0 = tile * TM_FFN
    rows = pl.ds(row0 if isinstance(row0, int) else pl.multiple_of(row0, 8), TM_FFN)
    if not isinstance(src, tuple):
        action(pltpu.make_async_copy(src.at[rows], buf, sem))
        return
    xp_hbm, xs_hbm = src

    @pl.when(tile < N_TILE_FFN - 1)
    def _():
        action(pltpu.make_async_copy(xp_hbm.at[rows], buf, sem))

    @pl.when(tile == N_TILE_FFN - 1)
    def _():
        tail = pl.ds((N_TILE_FFN - 1) * TM_FFN, SAMPLE_ROW0_FFN)
        action(pltpu.make_async_copy(xp_hbm.at[tail], buf.at[pl.ds(0, SAMPLE_ROW0_FFN)], sem))
        action(pltpu.make_async_copy(xs_hbm, buf.at[pl.ds(SAMPLE_ROW0_FFN, N_SAMPLE)], sem))


def _start(copy):
    copy.start()


def _wait(copy):
    copy.wait()


def _ffn_kernel(*refs, layer, split):
    if split:
        xp_hbm, xs_hbm, *refs = refs
        src = (xp_hbm, xs_hbm)
    else:
        src, *refs = refs
    (g_ref, wg_hbm, wu_hbm, wd_hbm, o_hbm,
     tbufs, xn_ref, wg_buf, wu_buf, wd_buf, w_sem, in_sem, out_sem) = refs
    tile = pl.program_id(0)
    cur = tile % 2
    nxt = 1 - cur
    tbuf = tbufs.at[cur]
    w_copies = functools.partial(_ffn_weight_copies, layer, w_hbm=(wg_hbm, wu_hbm, wd_hbm),
                                 w_buf=(wg_buf, wu_buf, wd_buf), sem=w_sem)

    def rows_out(t, slot):
        rows = pl.ds(pl.multiple_of(t * TM_FFN, 8), TM_FFN)
        return pltpu.make_async_copy(tbufs.at[slot], o_hbm.at[rows], out_sem.at[slot])

    @pl.when(tile == 0)
    def _():
        for step in range(W_AHEAD):
            for c in w_copies(step):
                c.start()
        _ffn_rows_in(src, 0, tbufs.at[0], in_sem.at[0], _start)

    _ffn_rows_in(src, tile, tbuf, in_sem.at[cur], _wait)
    xn_ref[...] = _rms_bf16(tbuf[...], g_ref[...])

    def prefetch(step):
        for c in w_copies(step):
            c.start()

    def block(step):
        slot = step % W_SLOTS
        xn = xn_ref[...]
        gt = _dot(xn, wg_buf[slot].astype(BF16))
        up = _dot(xn, wu_buf[slot].astype(BF16))
        hd = (gt * jax.nn.sigmoid(gt)) * up * 0.5
        tbuf[...] += _dot(hd.astype(BF16), wd_buf[slot].astype(BF16))

    def block_pair(jp, carry):
        @pl.when((jp == 1) & (tile >= 1))
        def _():
            rows_out(tile - 1, nxt).wait()

        @pl.when((jp == 1) & (tile + 1 < N_TILE_FFN))
        def _():
            _ffn_rows_in(src, tile + 1, tbufs.at[nxt], in_sem.at[nxt], _start)

        step = tile * N_FSTEP + 2 * jp
        prefetch(step + W_AHEAD)
        for c in w_copies(step) + w_copies(step + 1):
            c.wait()
        block(step)
        prefetch(step + 1 + W_AHEAD)
        block(step + 1)
        return carry

    lax.fori_loop(0, N_FSTEP // 2, block_pair, 0)
    rows_out(tile, cur).start()

    @pl.when(tile == N_TILE_FFN - 1)
    def _():
        rows_out(tile, cur).wait()
        for step in range(W_AHEAD):
            for c in w_copies(N_STEP_FFN + step):
                c.wait()


def _ffn(h, gain, w_gate, w_up, w_down, layer):
    split = isinstance(h, tuple)
    hbm = pl.BlockSpec(memory_space=pl.ANY)
    return pl.pallas_call(
        functools.partial(_ffn_kernel, layer=layer, split=split),
        grid=(N_TILE_FFN,),
        in_specs=[hbm] * (2 if split else 1) + [
            pl.BlockSpec((None, 1, D_MODEL), lambda i: (layer, 0, 0)), hbm, hbm, hbm],
        out_specs=hbm,
        out_shape=jax.ShapeDtypeStruct((N_TOK, D_MODEL), F32),
        scratch_shapes=[
            pltpu.VMEM((2, TM_FFN, D_MODEL), F32),
            pltpu.VMEM((TM_FFN, D_MODEL), BF16),
            pltpu.VMEM((W_SLOTS, D_MODEL, TF), F32),
            pltpu.VMEM((W_SLOTS, D_MODEL, TF), F32),
            pltpu.VMEM((W_SLOTS, TF, D_MODEL), F32),
            pltpu.SemaphoreType.DMA((3, W_SLOTS)),
            pltpu.SemaphoreType.DMA((2,)),
            pltpu.SemaphoreType.DMA((2,)),
        ],
        compiler_params=_ARB1,
        name="ffn",
    )(*(h if split else (h,)), gain, w_gate, w_up, w_down)


def _gmlp_weight_copies(jm, win_hbm, wout_hbm, win_buf, wout_buf, sem):
    return (pltpu.make_async_copy(win_hbm.at[jm], win_buf, sem.at[0]),
            pltpu.make_async_copy(wout_hbm.at[jm], wout_buf, sem.at[1]))


def _gmlp_kernel(h_ref, g_ref, win_hbm, wout_hbm, lng_ref, lnb_ref, ws_ref, bs_ref,
                 o_ref, vo_ref,
                 win_buf, wout_buf, xn_ref, v_ref, y_ref, sem, *, jm):
    tile = pl.program_id(0)
    is_last_tile = tile == N_TILE - 1
    copies = _gmlp_weight_copies(jm, win_hbm, wout_hbm, win_buf, wout_buf, sem)

    @pl.when(tile == 0)
    def _():
        for c in copies:
            c.start()

    xn_ref[...] = _rms_bf16(h_ref[...], g_ref[...])

    @pl.when(tile == 0)
    def _():
        for c in copies:
            c.wait()

    xn = xn_ref[...]

    def head_cols(first, head):
        return slice(first + head * D_HEAD, first + (head + 1) * D_HEAD)

    v_sum = jnp.zeros((TM, 1), F32)
    for head in range(N_HEAD):
        v = _gelu_tanh(_dot(xn, win_buf[:, head_cols(E_MIX, head)]))
        v_ref[head] = v
        v_sum = v_sum + jnp.sum(v, axis=-1, keepdims=True)
    mu = v_sum * (1.0 / E_MIX)
    var = jnp.zeros((TM, 1), F32)
    for head in range(N_HEAD):
        d = v_ref[head] - mu
        var = var + jnp.sum(d * d, axis=-1, keepdims=True)
    rstd = lax.rsqrt(var * (1.0 / E_MIX) + EPS)

    row = lax.broadcasted_iota(jnp.int32, (CHUNK, CHUNK), 0)
    col = lax.broadcasted_iota(jnp.int32, (CHUNK, CHUNK), 1)
    n_chunk = TM // CHUNK
    for head in range(N_HEAD):
        u = _gelu_tanh(_dot(xn, win_buf[:, head_cols(0, head)]))
        vn = (v_ref[head] - mu) * rstd * lng_ref[head] + lnb_ref[head]

        vo_ref[head] = vn[SAMPLE_ROW0:, :]
        w = ws_ref[head]
        b = bs_ref[head]
        w_causal = jnp.where(col <= row, w, 0.0)
        w_sample = jnp.where(col == row, w[0:1, 0:1], 0.0)
        b_sample = jnp.broadcast_to(b[0:1, :], (CHUNK, 1))
        w_last = jnp.where(is_last_tile, w_sample, w_causal).astype(BF16)
        b_last = jnp.where(is_last_tile, b_sample, b)
        w_causal = w_causal.astype(BF16)
        for c in range(n_chunk):
            rows = slice(c * CHUNK, (c + 1) * CHUNK)
            wm, bm = (w_last, b_last) if c == n_chunk - 1 else (w_causal, b)
            s = _dot(wm, vn[rows, :].astype(BF16)) + bm
            y_ref[rows, head_cols(0, head)] = (u[rows, :] * s).astype(BF16)

    o_ref[...] = h_ref[...] + _dot(y_ref[...], wout_buf[...])


def _gmlp(h, gain, w_in, ln_g, ln_b, w_s, b_s, w_out, layer, jm):
    hbm = pl.BlockSpec(memory_space=pl.ANY)
    return pl.pallas_call(
        functools.partial(_gmlp_kernel, jm=jm),
        grid=(N_TILE,),
        in_specs=[
            pl.BlockSpec((TM, D_MODEL), lambda i: (i, 0)),
            pl.BlockSpec((None, 1, D_MODEL), lambda i: (layer, 0, 0)),
            hbm, hbm,
            pl.BlockSpec((None, N_HEAD, 1, D_HEAD), lambda i: (jm, 0, 0, 0)),
            pl.BlockSpec((None, N_HEAD, 1, D_HEAD), lambda i: (jm, 0, 0, 0)),
            pl.BlockSpec((None, N_HEAD, CHUNK, CHUNK), lambda i: (jm, 0, 0, 0)),
            pl.BlockSpec((None, N_HEAD, CHUNK, 1), lambda i: (jm, 0, 0, 0)),
        ],
        out_specs=[
            pl.BlockSpec((TM, D_MODEL), lambda i: (i, 0)),
            pl.BlockSpec((N_HEAD, N_SAMPLE, D_HEAD), lambda i: (0, 0, 0)),
        ],
        out_shape=[
            jax.ShapeDtypeStruct((N_TOK, D_MODEL), F32),
            jax.ShapeDtypeStruct((N_HEAD, N_SAMPLE, D_HEAD), F32),
        ],
        scratch_shapes=[
            pltpu.VMEM((D_MODEL, 2 * E_MIX), BF16),
            pltpu.VMEM((E_MIX, D_MODEL), BF16),
            pltpu.VMEM((TM, D_MODEL), BF16),
            pltpu.VMEM((N_HEAD, TM, D_HEAD), F32),
            pltpu.VMEM((TM, E_MIX), BF16),
            pltpu.SemaphoreType.DMA((2,)),
        ],
        compiler_params=_ARB1,
        name="gmlp",
    )(h, gain, w_in, w_out, ln_g, ln_b, w_s, b_s)


_HALO = 8
CB = 256
N_CB = E_MIX // CB
CONV_PAIRS = N_CB // 2
CONV_SLOTS = 4
assert CONV_PAIRS % 2 == 0

_SEQ_TAIL = [divmod((s + 1) * SEQ - _HALO, TM) for s in range(N_PROMPT // SEQ)]


def _conv_kernel(h_ref, g_ref, win_hbm, wout_hbm, wk_ref, st0_ref, st1_ref,
                 o_ref, cs_ref, cp_ref,
                 win_buf, wout_buf, xn_ref, buf_ref, carry_ref, tails_ref, y_ref,
                 sem_in, sem_out, *, jm):
    tile = pl.program_id(0)
    is_last_tile = tile == N_TILE - 1

    def win_copy(block, slot):
        return pltpu.make_async_copy(win_hbm.at[jm, block], win_buf.at[slot], sem_in.at[slot])

    wout_copy = pltpu.make_async_copy(wout_hbm.at[jm], wout_buf, sem_out.at[0])

    @pl.when(tile == 0)
    def _():
        wout_copy.start()
        win_copy(0, 0).start()
        win_copy(1, 1).start()
        carry_ref[...] = jnp.zeros_like(carry_ref)

    xn_ref[...] = _rms_bf16(h_ref[...], g_ref[...])
    xn = xn_ref[...]
    pos = (lax.broadcasted_iota(jnp.int32, (TM, 1), 0) + tile * TM) & (SEQ - 1)

    def conv_block(block, slot):
        cols = slice(block * CB, (block + 1) * CB)
        bg = _dot(xn, win_buf[slot, 0])
        ci = _dot(xn, win_buf[slot, 1]) * _dot(xn, win_buf[slot, 2])

        for s, (_, row0) in enumerate(_SEQ_TAIL):
            tails_ref[s, block] = ci[row0:row0 + _HALO, :]
        ci_s = ci[SAMPLE_ROW0:, :]
        cs_ref[block] = ci_s

        buf = buf_ref.at[block % 2]
        buf[0:_HALO, :] = carry_ref[block]
        buf[_HALO:, :] = ci
        carry_ref[block] = ci[TM - _HALO:, :]
        prev1 = jnp.where(pos >= 1, buf[_HALO - 1:_HALO - 1 + TM, :], 0.0)
        prev2 = jnp.where(pos >= 2, buf[_HALO - 2:_HALO - 2 + TM, :], 0.0)
        w0 = wk_ref[0:1, cols]
        w1 = wk_ref[1:2, cols]
        w2 = wk_ref[2:3, cols]
        co = w0 * prev2 + w1 * prev1 + w2 * ci
        co_s = w0 * st0_ref[:, cols] + w1 * st1_ref[:, cols] + w2 * ci_s
        co_tail = jnp.where(is_last_tile, co_s, co[SAMPLE_ROW0:, :])
        y_ref[0:SAMPLE_ROW0, cols] = (bg[0:SAMPLE_ROW0, :] * co[0:SAMPLE_ROW0, :]).astype(BF16)
        y_ref[SAMPLE_ROW0:, cols] = (bg[SAMPLE_ROW0:, :] * co_tail).astype(BF16)

    for pair in range(CONV_PAIRS):
        blocks = (2 * pair, 2 * pair + 1)
        slots = (0, 1) if pair % 2 == 0 else (2, 3)
        free_slots = (2, 3) if pair % 2 == 0 else (0, 1)
        for b, slot in zip(blocks, free_slots):
            win_copy((b + 2) % N_CB, slot).start()
        for b, slot in zip(blocks, slots):
            win_copy(b, slot).wait()
        for b, slot in zip(blocks, slots):
            conv_block(b, slot)

    @pl.when(tile == 0)
    def _():
        wout_copy.wait()

    o_ref[...] = h_ref[...] + _dot(y_ref[...], wout_buf[...])

    for s, (seq_tile, _) in enumerate(_SEQ_TAIL):
        @pl.when(tile == seq_tile)
        def _():
            cp_ref[s] = tails_ref[s]

    @pl.when(is_last_tile)
    def _():
        for b in (0, 1):
            win_copy(b, b).wait()


def _conv_w_in_blocks_kernel(wb_ref, wc_ref, wx_ref, o_ref):
    for part, w_ref in enumerate((wb_ref, wc_ref, wx_ref)):
        o_ref[part] = w_ref[...].astype(BF16)


def _conv_w_in_blocks(w_in):
    n_b = w_in.shape[0]

    def part_spec(part):
        return pl.BlockSpec((None, D_MODEL, CB), lambda m, b: (m, 0, part * N_CB + b))

    return pl.pallas_call(
        _conv_w_in_blocks_kernel,
        grid=(n_b, N_CB),
        in_specs=[part_spec(0), part_spec(1), part_spec(2)],
        out_specs=pl.BlockSpec((None, None, 3, D_MODEL, CB), lambda m, b: (m, b, 0, 0, 0)),
        out_shape=jax.ShapeDtypeStruct((n_b, N_CB, 3, D_MODEL, CB), BF16),
        compiler_params=pltpu.CompilerParams(dimension_semantics=("arbitrary", "arbitrary")),
        name="conv_w_in_blocks",
    )(w_in, w_in, w_in)


def _conv(h, gain, w_in_blocked, w_conv, st0, st1, w_out, layer, jm):
    n_seq = N_PROMPT // SEQ
    hbm = pl.BlockSpec(memory_space=pl.ANY)
    return pl.pallas_call(
        functools.partial(_conv_kernel, jm=jm),
        grid=(N_TILE,),
        in_specs=[
            pl.BlockSpec((TM, D_MODEL), lambda i: (i, 0)),
            pl.BlockSpec((None, 1, D_MODEL), lambda i: (layer, 0, 0)),
            hbm, hbm,
            pl.BlockSpec((None, 3, E_MIX), lambda i: (jm, 0, 0)),
            pl.BlockSpec((N_SAMPLE, E_MIX), lambda i: (0, 0)),
            pl.BlockSpec((N_SAMPLE, E_MIX), lambda i: (0, 0)),
        ],
        out_specs=[
            pl.BlockSpec((TM, D_MODEL), lambda i: (i, 0)),
            pl.BlockSpec((N_CB, N_SAMPLE, CB), lambda i: (0, 0, 0)),
            pl.BlockSpec((n_seq, N_CB, _HALO, CB), lambda i: (0, 0, 0, 0)),
        ],
        out_shape=[
            jax.ShapeDtypeStruct((N_TOK, D_MODEL), F32),
            jax.ShapeDtypeStruct((N_CB, N_SAMPLE, CB), F32),
            jax.ShapeDtypeStruct((n_seq, N_CB, _HALO, CB), F32),
        ],
        scratch_shapes=[
            pltpu.VMEM((CONV_SLOTS, 3, D_MODEL, CB), BF16),
            pltpu.VMEM((E_MIX, D_MODEL), BF16),
            pltpu.VMEM((TM, D_MODEL), BF16),
            pltpu.VMEM((2, TM + _HALO, CB), F32),
            pltpu.VMEM((N_CB, _HALO, CB), F32),
            pltpu.VMEM((n_seq, N_CB, _HALO, CB), F32),
            pltpu.VMEM((TM, E_MIX), BF16),
            pltpu.SemaphoreType.DMA((CONV_SLOTS,)),
            pltpu.SemaphoreType.DMA((1,)),
        ],
        compiler_params=_ARB1,
        name="sconv",
    )(h, gain, w_in_blocked, w_out, w_conv, st0, st1)


def _ple_kernel(h_ref, g_ref, wg_ref, pp_ref, ps_ref, wp_ref, gf_ref, o_ref, *rest, final_norm):
    if final_norm:
        os_ref, pb_ref = rest
    else:
        (pb_ref,) = rest
    i = pl.program_id(0)

    @pl.when(i < N_TILE - 1)
    def _():
        pb_ref[...] = pp_ref[...].astype(BF16)

    @pl.when(i == N_TILE - 1)
    def _():
        pb_ref[0:SAMPLE_ROW0, :] = pp_ref[0:SAMPLE_ROW0, :].astype(BF16)
        pb_ref[SAMPLE_ROW0:, :] = ps_ref[...].astype(BF16)

    xn = _rms_bf16(h_ref[...], g_ref[...])
    pb = pb_ref[...]
    for c in range(D_MODEL // TN):
        cols = slice(c * TN, (c + 1) * TN)
        gate = jax.nn.sigmoid(_dot(xn, wg_ref[:, cols].astype(BF16)))
        o_ref[:, cols] = h_ref[:, cols] + gate * _dot(pb, wp_ref[:, cols].astype(BF16))
    if final_norm:
        hn = o_ref[...]
        ms = jnp.mean(hn * hn, axis=-1, keepdims=True)
        o_ref[...] = hn * lax.rsqrt(ms + EPS) * gf_ref[...]

        @pl.when(i == N_TILE - 1)
        def _():
            os_ref[...] = o_ref[SAMPLE_ROW0:, :]


def _ple(h, gain, w_gate, p_prompt, p_sample, w_proj, g_final, layer, final_norm):
    row_block = pl.BlockSpec((TM, D_MODEL), lambda i: (i, 0))
    if final_norm:
        out_specs = [row_block, pl.BlockSpec((N_SAMPLE, D_MODEL), lambda i: (0, 0))]
        out_shape = [jax.ShapeDtypeStruct((N_PROMPT, D_MODEL), F32),
                     jax.ShapeDtypeStruct((N_SAMPLE, D_MODEL), F32)]
    else:
        out_specs = row_block
        out_shape = jax.ShapeDtypeStruct((N_TOK, D_MODEL), F32)
    return pl.pallas_call(
        functools.partial(_ple_kernel, final_norm=final_norm),
        grid=(N_TILE,),
        in_specs=[
            row_block,
            pl.BlockSpec((None, 1, D_MODEL), lambda i: (layer, 0, 0)),
            pl.BlockSpec((None, D_MODEL, D_MODEL), lambda i: (layer, 0, 0),
                         pipeline_mode=pl.Buffered(1)),
            pl.BlockSpec((None, TM, D_PLE), lambda i: (layer, i, 0)),
            pl.BlockSpec((None, N_SAMPLE, D_PLE), lambda i: (layer, 0, 0)),
            pl.BlockSpec((None, D_PLE, D_MODEL), lambda i: (layer, 0, 0),
                         pipeline_mode=pl.Buffered(1)),
            pl.BlockSpec((1, D_MODEL), lambda i: (0, 0)),
        ],
        out_specs=out_specs,
        out_shape=out_shape,
        scratch_shapes=[pltpu.VMEM((TM, D_PLE), BF16)],
        compiler_params=_ARB1,
        name="ple",
    )(h, gain, w_gate, p_prompt, p_sample, w_proj, g_final)


def kernel(x_prompt, x_sample, state_conv, p_prompt, p_sample, ffn1_norm, ffn1_w_gate, ffn1_w_up, ffn1_w_down, mix_norm, a_w_in, a_ln_g, a_ln_b, a_w_s, a_b_s, a_w_out, c_w_in, c_w_conv, c_w_out, ffn2_norm, ffn2_w_gate, ffn2_w_up, ffn2_w_down, ple_norm, ple_w_gate, ple_w_proj, final_norm):
    bf = lambda w: w.astype(BF16)
    gain3 = lambda g: g.reshape(g.shape[0], 1, D_MODEL)

    h = (x_prompt.reshape(N_PROMPT, D_MODEL), x_sample.reshape(N_SAMPLE, D_MODEL))
    p_prompt = p_prompt.reshape(DEPTH, N_PROMPT, D_PLE)
    p_sample = p_sample.reshape(DEPTH, N_SAMPLE, D_PLE)

    f1 = (gain3(ffn1_norm), ffn1_w_gate, ffn1_w_up, ffn1_w_down)
    f2 = (gain3(ffn2_norm), ffn2_w_gate, ffn2_w_up, ffn2_w_down)
    mix_gain = gain3(mix_norm)
    ple_gain = gain3(ple_norm)
    a_w_in_b, a_w_out_b = bf(a_w_in), bf(a_w_out)
    c_w_in_b = _conv_w_in_blocks(c_w_in)
    c_w_out_b = bf(c_w_out)
    ple_w_gate_b, ple_w_proj_b = ple_w_gate, ple_w_proj
    ln_g = a_ln_g.reshape(-1, N_HEAD, 1, D_HEAD)
    ln_b = a_ln_b.reshape(-1, N_HEAD, 1, D_HEAD)
    b_s = a_b_s.reshape(-1, N_HEAD, CHUNK, 1)
    g_final = final_norm.reshape(1, D_MODEL)

    conv_new_prompt, conv_new_sample, v_new = [], [], []
    for layer in range(DEPTH):
        h = _ffn(h, *f1, layer)
        jm = layer // 2
        if layer % 2 == 0:
            h, v_s = _gmlp(h, mix_gain, a_w_in_b, ln_g, ln_b, a_w_s, b_s, a_w_out_b, layer, jm)
            v_new.append(v_s.transpose(1, 0, 2).reshape(N_SAMPLE, 1, E_MIX))
        else:
            st0 = state_conv[jm, :, 0, :]
            st1 = state_conv[jm, :, 1, :]
            h, ci_s, ci_p = _conv(h, mix_gain, c_w_in_b, c_w_conv, st0, st1, c_w_out_b, layer, jm)
            ci_s = ci_s.transpose(1, 0, 2).reshape(N_SAMPLE, E_MIX)
            ci_p = ci_p.transpose(0, 2, 1, 3).reshape(N_PROMPT // SEQ, _HALO, E_MIX)
            conv_new_prompt.append(ci_p[:, _HALO - 2:, :])
            conv_new_sample.append(jnp.stack([st1, ci_s], axis=1))
        h = _ffn(h, *f2, layer)
        h = _ple(h, ple_gain, ple_w_gate_b, p_prompt, p_sample, ple_w_proj_b, g_final, layer,
                 final_norm=(layer == DEPTH - 1))

    y_prompt, y_sample = h
    return (y_prompt.reshape(N_PROMPT // SEQ, SEQ, D_MODEL), y_sample.reshape(N_SAMPLE, 1, D_MODEL),
            jnp.stack(conv_new_prompt), jnp.stack(conv_new_sample), jnp.stack(v_new))
```

```python
import functools

import jax
import jax.numpy as jnp
from jax import lax
from jax.experimental import pallas as pl
from jax.experimental.pallas import tpu as pltpu

F32 = jnp.float32
BF16 = jnp.bfloat16

D_MODEL = 2048
D_FF = 5632
E_MIX = 2048
N_HEAD = 8
D_HEAD = 256
CHUNK = 128
D_PLE = 256
SEQ = 2048
N_PROMPT = 4 * SEQ
N_SAMPLE = 128
N_TOK = N_PROMPT + N_SAMPLE
DEPTH = 2
EPS = 1e-6

TM = 640
N_TILE = N_TOK // TM
SAMPLE_ROW0 = TM - N_SAMPLE
TM_FFN = 1664
N_TILE_FFN = N_TOK // TM_FFN
SAMPLE_ROW0_FFN = TM_FFN - N_SAMPLE
TF = 256
TN = 512
V7X_VMEM_BYTES = 64 * 1024 * 1024
VMEM_LIMIT = V7X_VMEM_BYTES - 2 * 1024 * 1024

_ARB1 = pltpu.CompilerParams(dimension_semantics=("arbitrary",),
                             vmem_limit_bytes=VMEM_LIMIT)


def _rms_bf16(x, g):
    ms = jnp.mean(x * x, axis=-1, keepdims=True)
    return (x * lax.rsqrt(ms + EPS) * g).astype(BF16)


def _dot(a, b):
    return jnp.dot(a, b, preferred_element_type=F32)


def _gelu_tanh(x):
    c = 0.7978845608028654
    return x * (0.5 * (1.0 + jnp.tanh(c * (x + 0.044715 * (x * x * x)))))


N_FSTEP = D_FF // TF
N_STEP_FFN = N_TILE_FFN * N_FSTEP
W_SLOTS = 4
W_AHEAD = W_SLOTS - 1
assert N_FSTEP % 2 == 0


def _ffn_weight_copies(layer, step, w_hbm, w_buf, sem):
    wg_hbm, wu_hbm, wd_hbm = w_hbm
    wg_buf, wu_buf, wd_buf = w_buf
    slot = step % W_SLOTS
    cols = pl.ds(pl.multiple_of((step % N_FSTEP) * TF, TF), TF)
    return (
        pltpu.make_async_copy(wg_hbm.at[layer, :, cols], wg_buf.at[slot], sem.at[0, slot]),
        pltpu.make_async_copy(wu_hbm.at[layer, :, cols], wu_buf.at[slot], sem.at[1, slot]),
        pltpu.make_async_copy(wd_hbm.at[layer, cols, :], wd_buf.at[slot], sem.at[2, slot]),
    )


def _ffn_rows_in(src, tile, buf, sem, action):
    row0 = tile * TM_FFN
    rows = pl.ds(row0 if isinstance(row0, int) else pl.multiple_of(row0, 8), TM_FFN)
    if not isinstance(src, tuple):
        action(pltpu.make_async_copy(src.at[rows], buf, sem))
        return
    xp_hbm, xs_hbm = src

    @pl.when(tile < N_TILE_FFN - 1)
    def _():
        action(pltpu.make_async_copy(xp_hbm.at[rows], buf, sem))

    @pl.when(tile == N_TILE_FFN - 1)
    def _():
        tail = pl.ds((N_TILE_FFN - 1) * TM_FFN, SAMPLE_ROW0_FFN)
        action(pltpu.make_async_copy(xp_hbm.at[tail], buf.at[pl.ds(0, SAMPLE_ROW0_FFN)], sem))
        action(pltpu.make_async_copy(xs_hbm, buf.at[pl.ds(SAMPLE_ROW0_FFN, N_SAMPLE)], sem))


def _start(copy):
    copy.start()


def _wait(copy):
    copy.wait()


def _ffn_kernel(*refs, layer, split):
    if split:
        xp_hbm, xs_hbm, *refs = refs
        src = (xp_hbm, xs_hbm)
    else:
        src, *refs = refs
    (g_ref, wg_hbm, wu_hbm, wd_hbm, o_hbm,
     tbufs, xn_ref, wg_buf, wu_buf, wd_buf, w_sem, in_sem, out_sem) = refs
    tile = pl.program_id(0)
    cur = tile % 2
    nxt = 1 - cur
    tbuf = tbufs.at[cur]
    w_copies = functools.partial(_ffn_weight_copies, layer, w_hbm=(wg_hbm, wu_hbm, wd_hbm),
                                 w_buf=(wg_buf, wu_buf, wd_buf), sem=w_sem)

    def rows_out(t, slot):
        rows = pl.ds(pl.multiple_of(t * TM_FFN, 8), TM_FFN)
        return pltpu.make_async_copy(tbufs.at[slot], o_hbm.at[rows], out_sem.at[slot])

    @pl.when(tile == 0)
    def _():
        for step in range(W_AHEAD):
            for c in w_copies(step):
                c.start()
        _ffn_rows_in(src, 0, tbufs.at[0], in_sem.at[0], _start)

    _ffn_rows_in(src, tile, tbuf, in_sem.at[cur], _wait)
    xn_ref[...] = _rms_bf16(tbuf[...], g_ref[...])

    def prefetch(step):
        for c in w_copies(step):
            c.start()

    def block(step):
        slot = step % W_SLOTS
        xn = xn_ref[...]
        gt = _dot(xn, wg_buf[slot].astype(BF16))
        up = _dot(xn, wu_buf[slot].astype(BF16))
        hd = (gt * jax.nn.sigmoid(gt)) * up * 0.5
        tbuf[...] += _dot(hd.astype(BF16), wd_buf[slot].astype(BF16))

    def block_pair(jp, carry):
        @pl.when((jp == 1) & (tile >= 1))
        def _():
            rows_out(tile - 1, nxt).wait()

        @pl.when((jp == 1) & (tile + 1 < N_TILE_FFN))
        def _():
            _ffn_rows_in(src, tile + 1, tbufs.at[nxt], in_sem.at[nxt], _start)

        step = tile * N_FSTEP + 2 * jp
        prefetch(step + W_AHEAD)
        for c in w_copies(step) + w_copies(step + 1):
            c.wait()
        block(step)
        prefetch(step + 1 + W_AHEAD)
        block(step + 1)
        return carry

    lax.fori_loop(0, N_FSTEP // 2, block_pair, 0)
    rows_out(tile, cur).start()

    @pl.when(tile == N_TILE_FFN - 1)
    def _():
        rows_out(tile, cur).wait()
        for step in range(W_AHEAD):
            for c in w_copies(N_STEP_FFN + step):
                c.wait()


def _ffn(h, gain, w_gate, w_up, w_down, layer):
    split = isinstance(h, tuple)
    hbm = pl.BlockSpec(memory_space=pl.ANY)
    return pl.pallas_call(
        functools.partial(_ffn_kernel, layer=layer, split=split),
        grid=(N_TILE_FFN,),
        in_specs=[hbm] * (2 if split else 1) + [
            pl.BlockSpec((None, 1, D_MODEL), lambda i: (layer, 0, 0)), hbm, hbm, hbm],
        out_specs=hbm,
        out_shape=jax.ShapeDtypeStruct((N_TOK, D_MODEL), F32),
        scratch_shapes=[
            pltpu.VMEM((2, TM_FFN, D_MODEL), F32),
            pltpu.VMEM((TM_FFN, D_MODEL), BF16),
            pltpu.VMEM((W_SLOTS, D_MODEL, TF), F32),
            pltpu.VMEM((W_SLOTS, D_MODEL, TF), F32),
            pltpu.VMEM((W_SLOTS, TF, D_MODEL), F32),
            pltpu.SemaphoreType.DMA((3, W_SLOTS)),
            pltpu.SemaphoreType.DMA((2,)),
            pltpu.SemaphoreType.DMA((2,)),
        ],
        compiler_params=_ARB1,
        name="ffn",
    )(*(h if split else (h,)), gain, w_gate, w_up, w_down)


W_STAGE_ROWS = 128


def _stage_weight(w_hbm, jm, w_buf, stage, sem):
    n_rows, width = w_buf.shape
    n_chunk = n_rows // W_STAGE_ROWS

    def chunk_rows(c):
        return pl.ds(pl.multiple_of(c * W_STAGE_ROWS, W_STAGE_ROWS), W_STAGE_ROWS)

    def chunk_copy(c, slot):
        return pltpu.make_async_copy(w_hbm.at[jm, chunk_rows(c), :],
                                     stage.at[slot, :, pl.ds(0, width)], sem.at[slot])

    chunk_copy(0, 0).start()

    def chunk(c, carry):
        slot = c % 2

        @pl.when(c + 1 < n_chunk)
        def _():
            chunk_copy(c + 1, 1 - slot).start()

        chunk_copy(c, slot).wait()
        w_buf[chunk_rows(c), :] = stage[slot, :, pl.ds(0, width)].astype(BF16)
        return carry

    lax.fori_loop(0, n_chunk, chunk, 0)


def _gmlp_kernel(h_ref, g_ref, win_hbm, wout_hbm, lng_ref, lnb_ref, ws_ref, bs_ref,
                 o_ref, vo_ref,
                 win_buf, wout_buf, xn_ref, v_ref, y_ref, stage, sem, *, jm):
    tile = pl.program_id(0)
    is_last_tile = tile == N_TILE - 1

    @pl.when(tile == 0)
    def _():
        _stage_weight(win_hbm, jm, win_buf, stage, sem)
        _stage_weight(wout_hbm, jm, wout_buf, stage, sem)

    xn_ref[...] = _rms_bf16(h_ref[...], g_ref[...])
    xn = xn_ref[...]

    def head_cols(first, head):
        return slice(first + head * D_HEAD, first + (head + 1) * D_HEAD)

    v_sum = jnp.zeros((TM, 1), F32)
    for head in range(N_HEAD):
        v = _gelu_tanh(_dot(xn, win_buf[:, head_cols(E_MIX, head)]))
        v_ref[head] = v
        v_sum = v_sum + jnp.sum(v, axis=-1, keepdims=True)
    mu = v_sum * (1.0 / E_MIX)
    var = jnp.zeros((TM, 1), F32)
    for head in range(N_HEAD):
        d = v_ref[head] - mu
        var = var + jnp.sum(d * d, axis=-1, keepdims=True)
    rstd = lax.rsqrt(var * (1.0 / E_MIX) + EPS)

    row = lax.broadcasted_iota(jnp.int32, (CHUNK, CHUNK), 0)
    col = lax.broadcasted_iota(jnp.int32, (CHUNK, CHUNK), 1)
    n_chunk = TM // CHUNK
    for head in range(N_HEAD):
        u = _gelu_tanh(_dot(xn, win_buf[:, head_cols(0, head)]))
        vn = (v_ref[head] - mu) * rstd * lng_ref[head] + lnb_ref[head]

        vo_ref[head] = vn[SAMPLE_ROW0:, :]
        w = ws_ref[head]
        b = bs_ref[head]
        w_causal = jnp.where(col <= row, w, 0.0)
        w_sample = jnp.where(col == row, w[0:1, 0:1], 0.0)
        b_sample = jnp.broadcast_to(b[0:1, :], (CHUNK, 1))
        w_last = jnp.where(is_last_tile, w_sample, w_causal).astype(BF16)
        b_last = jnp.where(is_last_tile, b_sample, b)
        w_causal = w_causal.astype(BF16)
        for c in range(n_chunk):
            rows = slice(c * CHUNK, (c + 1) * CHUNK)
            wm, bm = (w_last, b_last) if c == n_chunk - 1 else (w_causal, b)
            s = _dot(wm, vn[rows, :].astype(BF16)) + bm
            y_ref[rows, head_cols(0, head)] = (u[rows, :] * s).astype(BF16)

    o_ref[...] = h_ref[...] + _dot(y_ref[...], wout_buf[...])


def _gmlp(h, gain, w_in, ln_g, ln_b, w_s, b_s, w_out, layer, jm):
    hbm = pl.BlockSpec(memory_space=pl.ANY)
    return pl.pallas_call(
        functools.partial(_gmlp_kernel, jm=jm),
        grid=(N_TILE,),
        in_specs=[
            pl.BlockSpec((TM, D_MODEL), lambda i: (i, 0)),
            pl.BlockSpec((None, 1, D_MODEL), lambda i: (layer, 0, 0)),
            hbm, hbm,
            pl.BlockSpec((None, N_HEAD, 1, D_HEAD), lambda i: (jm, 0, 0, 0)),
            pl.BlockSpec((None, N_HEAD, 1, D_HEAD), lambda i: (jm, 0, 0, 0)),
            pl.BlockSpec((None, N_HEAD, CHUNK, CHUNK), lambda i: (jm, 0, 0, 0)),
            pl.BlockSpec((None, N_HEAD, CHUNK, 1), lambda i: (jm, 0, 0, 0)),
        ],
        out_specs=[
            pl.BlockSpec((TM, D_MODEL), lambda i: (i, 0)),
            pl.BlockSpec((N_HEAD, N_SAMPLE, D_HEAD), lambda i: (0, 0, 0)),
        ],
        out_shape=[
            jax.ShapeDtypeStruct((N_TOK, D_MODEL), F32),
            jax.ShapeDtypeStruct((N_HEAD, N_SAMPLE, D_HEAD), F32),
        ],
        scratch_shapes=[
            pltpu.VMEM((D_MODEL, 2 * E_MIX), BF16),
            pltpu.VMEM((E_MIX, D_MODEL), BF16),
            pltpu.VMEM((TM, D_MODEL), BF16),
            pltpu.VMEM((N_HEAD, TM, D_HEAD), F32),
            pltpu.VMEM((TM, E_MIX), BF16),
            pltpu.VMEM((2, W_STAGE_ROWS, 2 * E_MIX), F32),
            pltpu.SemaphoreType.DMA((2,)),
        ],
        compiler_params=_ARB1,
        name="gmlp",
    )(h, gain, w_in, w_out, ln_g, ln_b, w_s, b_s)


_HALO = 8
CB = 256
N_CB = E_MIX // CB
CONV_PAIRS = N_CB // 2
CONV_SLOTS = 4
assert CONV_PAIRS % 2 == 0

_SEQ_TAIL = [divmod((s + 1) * SEQ - _HALO, TM) for s in range(N_PROMPT // SEQ)]


def _conv_kernel(h_ref, g_ref, win_hbm, wout_hbm, wk_ref, st0_ref, st1_ref,
                 o_ref, cs_ref, cp_ref,
                 win_buf, wout_buf, xn_ref, buf_ref, carry_ref, tails_ref, y_ref,
                 sem_in, sem_out, *, jm):
    tile = pl.program_id(0)
    is_last_tile = tile == N_TILE - 1

    def win_copy(block, slot):
        return pltpu.make_async_copy(win_hbm.at[jm, block], win_buf.at[slot], sem_in.at[slot])

    wout_copy = pltpu.make_async_copy(wout_hbm.at[jm], wout_buf, sem_out.at[0])

    @pl.when(tile == 0)
    def _():
        wout_copy.start()
        win_copy(0, 0).start()
        win_copy(1, 1).start()
        carry_ref[...] = jnp.zeros_like(carry_ref)

    xn_ref[...] = _rms_bf16(h_ref[...], g_ref[...])
    xn = xn_ref[...]
    pos = (lax.broadcasted_iota(jnp.int32, (TM, 1), 0) + tile * TM) & (SEQ - 1)

    def conv_block(block, slot):
        cols = slice(block * CB, (block + 1) * CB)
        bg = _dot(xn, win_buf[slot, 0])
        ci = _dot(xn, win_buf[slot, 1]) * _dot(xn, win_buf[slot, 2])

        for s, (_, row0) in enumerate(_SEQ_TAIL):
            tails_ref[s, block] = ci[row0:row0 + _HALO, :]
        ci_s = ci[SAMPLE_ROW0:, :]
        cs_ref[block] = ci_s

        buf = buf_ref.at[block % 2]
        buf[0:_HALO, :] = carry_ref[block]
        buf[_HALO:, :] = ci
        carry_ref[block] = ci[TM - _HALO:, :]
        prev1 = jnp.where(pos >= 1, buf[_HALO - 1:_HALO - 1 + TM, :], 0.0)
        prev2 = jnp.where(pos >= 2, buf[_HALO - 2:_HALO - 2 + TM, :], 0.0)
        w0 = wk_ref[0:1, cols]
        w1 = wk_ref[1:2, cols]
        w2 = wk_ref[2:3, cols]
        co = w0 * prev2 + w1 * prev1 + w2 * ci
        co_s = w0 * st0_ref[:, cols] + w1 * st1_ref[:, cols] + w2 * ci_s
        co_tail = jnp.where(is_last_tile, co_s, co[SAMPLE_ROW0:, :])
        y_ref[0:SAMPLE_ROW0, cols] = (bg[0:SAMPLE_ROW0, :] * co[0:SAMPLE_ROW0, :]).astype(BF16)
        y_ref[SAMPLE_ROW0:, cols] = (bg[SAMPLE_ROW0:, :] * co_tail).astype(BF16)

    for pair in range(CONV_PAIRS):
        blocks = (2 * pair, 2 * pair + 1)
        slots = (0, 1) if pair % 2 == 0 else (2, 3)
        free_slots = (2, 3) if pair % 2 == 0 else (0, 1)
        for b, slot in zip(blocks, free_slots):
            win_copy((b + 2) % N_CB, slot).start()
        for b, slot in zip(blocks, slots):
            win_copy(b, slot).wait()
        for b, slot in zip(blocks, slots):
            conv_block(b, slot)

    @pl.when(tile == 0)
    def _():
        wout_copy.wait()

    o_ref[...] = h_ref[...] + _dot(y_ref[...], wout_buf[...])

    for s, (seq_tile, _) in enumerate(_SEQ_TAIL):
        @pl.when(tile == seq_tile)
        def _():
            cp_ref[s] = tails_ref[s]

    @pl.when(is_last_tile)
    def _():
        for b in (0, 1):
            win_copy(b, b).wait()


def _conv_w_in_blocks_kernel(wb_ref, wc_ref, wx_ref, o_ref):
    for part, w_ref in enumerate((wb_ref, wc_ref, wx_ref)):
        o_ref[part] = w_ref[...].astype(BF16)


def _conv_w_in_blocks(w_in):
    n_b = w_in.shape[0]

    def part_spec(part):
        return pl.BlockSpec((None, D_MODEL, CB), lambda m, b: (m, 0, part * N_CB + b))

    return pl.pallas_call(
        _conv_w_in_blocks_kernel,
        grid=(n_b, N_CB),
        in_specs=[part_spec(0), part_spec(1), part_spec(2)],
        out_specs=pl.BlockSpec((None, None, 3, D_MODEL, CB), lambda m, b: (m, b, 0, 0, 0)),
        out_shape=jax.ShapeDtypeStruct((n_b, N_CB, 3, D_MODEL, CB), BF16),
        compiler_params=pltpu.CompilerParams(dimension_semantics=("arbitrary", "arbitrary")),
        name="conv_w_in_blocks",
    )(w_in, w_in, w_in)


def _conv(h, gain, w_in_blocked, w_conv, st0, st1, w_out, layer, jm):
    n_seq = N_PROMPT // SEQ
    hbm = pl.BlockSpec(memory_space=pl.ANY)
    return pl.pallas_call(
        functools.partial(_conv_kernel, jm=jm),
        grid=(N_TILE,),
        in_specs=[
            pl.BlockSpec((TM, D_MODEL), lambda i: (i, 0)),
            pl.BlockSpec((None, 1, D_MODEL), lambda i: (layer, 0, 0)),
            hbm, hbm,
            pl.BlockSpec((None, 3, E_MIX), lambda i: (jm, 0, 0)),
            pl.BlockSpec((N_SAMPLE, E_MIX), lambda i: (0, 0)),
            pl.BlockSpec((N_SAMPLE, E_MIX), lambda i: (0, 0)),
        ],
        out_specs=[
            pl.BlockSpec((TM, D_MODEL), lambda i: (i, 0)),
            pl.BlockSpec((N_CB, N_SAMPLE, CB), lambda i: (0, 0, 0)),
            pl.BlockSpec((n_seq, N_CB, _HALO, CB), lambda i: (0, 0, 0, 0)),
        ],
        out_shape=[
            jax.ShapeDtypeStruct((N_TOK, D_MODEL), F32),
            jax.ShapeDtypeStruct((N_CB, N_SAMPLE, CB), F32),
            jax.ShapeDtypeStruct((n_seq, N_CB, _HALO, CB), F32),
        ],
        scratch_shapes=[
            pltpu.VMEM((CONV_SLOTS, 3, D_MODEL, CB), BF16),
            pltpu.VMEM((E_MIX, D_MODEL), BF16),
            pltpu.VMEM((TM, D_MODEL), BF16),
            pltpu.VMEM((2, TM + _HALO, CB), F32),
            pltpu.VMEM((N_CB, _HALO, CB), F32),
            pltpu.VMEM((n_seq, N_CB, _HALO, CB), F32),
            pltpu.VMEM((TM, E_MIX), BF16),
            pltpu.SemaphoreType.DMA((CONV_SLOTS,)),
            pltpu.SemaphoreType.DMA((1,)),
        ],
        compiler_params=_ARB1,
        name="sconv",
    )(h, gain, w_in_blocked, w_out, w_conv, st0, st1)


def _ple_kernel(h_ref, g_ref, wg_ref, pp_ref, ps_ref, wp_ref, gf_ref, o_ref, *rest, final_norm):
    if final_norm:
        os_ref, pb_ref = rest
    else:
        (pb_ref,) = rest
    i = pl.program_id(0)

    @pl.when(i < N_TILE - 1)
    def _():
        pb_ref[...] = pp_ref[...].astype(BF16)

    @pl.when(i == N_TILE - 1)
    def _():
        pb_ref[0:SAMPLE_ROW0, :] = pp_ref[0:SAMPLE_ROW0, :].astype(BF16)
        pb_ref[SAMPLE_ROW0:, :] = ps_ref[...].astype(BF16)

    xn = _rms_bf16(h_ref[...], g_ref[...])
    pb = pb_ref[...]
    for c in range(D_MODEL // TN):
        cols = slice(c * TN, (c + 1) * TN)
        gate = jax.nn.sigmoid(_dot(xn, wg_ref[:, cols].astype(BF16)))
        o_ref[:, cols] = h_ref[:, cols] + gate * _dot(pb, wp_ref[:, cols].astype(BF16))
    if final_norm:
        hn = o_ref[...]
        ms = jnp.mean(hn * hn, axis=-1, keepdims=True)
        o_ref[...] = hn * lax.rsqrt(ms + EPS) * gf_ref[...]

        @pl.when(i == N_TILE - 1)
        def _():
            os_ref[...] = o_ref[SAMPLE_ROW0:, :]


def _ple(h, gain, w_gate, p_prompt, p_sample, w_proj, g_final, layer, final_norm):
    row_block = pl.BlockSpec((TM, D_MODEL), lambda i: (i, 0))
    if final_norm:
        out_specs = [row_block, pl.BlockSpec((N_SAMPLE, D_MODEL), lambda i: (0, 0))]
        out_shape = [jax.ShapeDtypeStruct((N_PROMPT, D_MODEL), F32),
                     jax.ShapeDtypeStruct((N_SAMPLE, D_MODEL), F32)]
    else:
        out_specs = row_block
        out_shape = jax.ShapeDtypeStruct((N_TOK, D_MODEL), F32)
    return pl.pallas_call(
        functools.partial(_ple_kernel, final_norm=final_norm),
        grid=(N_TILE,),
        in_specs=[
            row_block,
            pl.BlockSpec((None, 1, D_MODEL), lambda i: (layer, 0, 0)),
            pl.BlockSpec((None, D_MODEL, D_MODEL), lambda i: (layer, 0, 0),
                         pipeline_mode=pl.Buffered(1)),
            pl.BlockSpec((None, TM, D_PLE), lambda i: (layer, i, 0)),
            pl.BlockSpec((None, N_SAMPLE, D_PLE), lambda i: (layer, 0, 0)),
            pl.BlockSpec((None, D_PLE, D_MODEL), lambda i: (layer, 0, 0),
                         pipeline_mode=pl.Buffered(1)),
            pl.BlockSpec((1, D_MODEL), lambda i: (0, 0)),
        ],
        out_specs=out_specs,
        out_shape=out_shape,
        scratch_shapes=[pltpu.VMEM((TM, D_PLE), BF16)],
        compiler_params=_ARB1,
        name="ple",
    )(h, gain, w_gate, p_prompt, p_sample, w_proj, g_final)


def kernel(x_prompt, x_sample, state_conv, p_prompt, p_sample, ffn1_norm, ffn1_w_gate, ffn1_w_up, ffn1_w_down, mix_norm, a_w_in, a_ln_g, a_ln_b, a_w_s, a_b_s, a_w_out, c_w_in, c_w_conv, c_w_out, ffn2_norm, ffn2_w_gate, ffn2_w_up, ffn2_w_down, ple_norm, ple_w_gate, ple_w_proj, final_norm):
    bf = lambda w: w.astype(BF16)
    gain3 = lambda g: g.reshape(g.shape[0], 1, D_MODEL)

    h = (x_prompt.reshape(N_PROMPT, D_MODEL), x_sample.reshape(N_SAMPLE, D_MODEL))
    p_prompt = p_prompt.reshape(DEPTH, N_PROMPT, D_PLE)
    p_sample = p_sample.reshape(DEPTH, N_SAMPLE, D_PLE)

    f1 = (gain3(ffn1_norm), ffn1_w_gate, ffn1_w_up, ffn1_w_down)
    f2 = (gain3(ffn2_norm), ffn2_w_gate, ffn2_w_up, ffn2_w_down)
    mix_gain = gain3(mix_norm)
    ple_gain = gain3(ple_norm)
    a_w_in_b, a_w_out_b = a_w_in, a_w_out
    c_w_in_b = _conv_w_in_blocks(c_w_in)
    c_w_out_b = bf(c_w_out)
    ple_w_gate_b, ple_w_proj_b = ple_w_gate, ple_w_proj
    ln_g = a_ln_g.reshape(-1, N_HEAD, 1, D_HEAD)
    ln_b = a_ln_b.reshape(-1, N_HEAD, 1, D_HEAD)
    b_s = a_b_s.reshape(-1, N_HEAD, CHUNK, 1)
    g_final = final_norm.reshape(1, D_MODEL)

    conv_new_prompt, conv_new_sample, v_new = [], [], []
    for layer in range(DEPTH):
        h = _ffn(h, *f1, layer)
        jm = layer // 2
        if layer % 2 == 0:
            h, v_s = _gmlp(h, mix_gain, a_w_in_b, ln_g, ln_b, a_w_s, b_s, a_w_out_b, layer, jm)
            v_new.append(v_s.transpose(1, 0, 2).reshape(N_SAMPLE, 1, E_MIX))
        else:
            st0 = state_conv[jm, :, 0, :]
            st1 = state_conv[jm, :, 1, :]
            h, ci_s, ci_p = _conv(h, mix_gain, c_w_in_b, c_w_conv, st0, st1, c_w_out_b, layer, jm)
            ci_s = ci_s.transpose(1, 0, 2).reshape(N_SAMPLE, E_MIX)
            ci_p = ci_p.transpose(0, 2, 1, 3).reshape(N_PROMPT // SEQ, _HALO, E_MIX)
            conv_new_prompt.append(ci_p[:, _HALO - 2:, :])
            conv_new_sample.append(jnp.stack([st1, ci_s], axis=1))
        h = _ffn(h, *f2, layer)
        h = _ple(h, ple_gain, ple_w_gate_b, p_prompt, p_sample, ple_w_proj_b, g_final, layer,
                 final_norm=(layer == DEPTH - 1))

    y_prompt, y_sample = h
    return (y_prompt.reshape(N_PROMPT // SEQ, SEQ, D_MODEL), y_sample.reshape(N_SAMPLE, 1, D_MODEL),
            jnp.stack(conv_new_prompt), jnp.stack(conv_new_sample), jnp.stack(v_new))
```

```python
import functools

import jax
import jax.numpy as jnp
from jax import lax
from jax.experimental import pallas as pl
from jax.experimental.pallas import tpu as pltpu

F32 = jnp.float32
BF16 = jnp.bfloat16

D_MODEL = 2048
D_FF = 5632
E_MIX = 2048
N_HEAD = 8
D_HEAD = 256
CHUNK = 128
D_PLE = 256
SEQ = 2048
N_PROMPT = 4 * SEQ
N_SAMPLE = 128
N_TOK = N_PROMPT + N_SAMPLE
DEPTH = 2
EPS = 1e-6

TM = 640
N_TILE = N_TOK // TM
SAMPLE_ROW0 = TM - N_SAMPLE
TM_FFN = 1664
N_TILE_FFN = N_TOK // TM_FFN
SAMPLE_ROW0_FFN = TM_FFN - N_SAMPLE
TF = 256
TN = 512
V7X_VMEM_BYTES = 64 * 1024 * 1024
VMEM_LIMIT = V7X_VMEM_BYTES - 2 * 1024 * 1024

_ARB1 = pltpu.CompilerParams(dimension_semantics=("arbitrary",),
                             vmem_limit_bytes=VMEM_LIMIT)


def _rms_bf16(x, g):
    ms = jnp.mean(x * x, axis=-1, keepdims=True)
    return (x * lax.rsqrt(ms + EPS) * g).astype(BF16)


def _dot(a, b):
    return jnp.dot(a, b, preferred_element_type=F32)


def _gelu_tanh(x):
    c = 0.7978845608028654
    return x * (0.5 * (1.0 + jnp.tanh(c * (x + 0.044715 * (x * x * x)))))


TF2 = 2 * TF
N_FPAIR = D_FF // TF2
N_PAIR_FFN = N_TILE_FFN * N_FPAIR
W_SLOTS = 2


def _ffn_weight_copies(layer, pair, w_hbm, w_buf, sem):
    wg_hbm, wu_hbm, wd_hbm = w_hbm
    wg_buf, wu_buf, wd_buf = w_buf
    slot = pair % W_SLOTS
    col0 = (pair % N_FPAIR) * TF2
    cols = pl.ds(col0 if isinstance(col0, int) else pl.multiple_of(col0, TF2), TF2)
    return (
        pltpu.make_async_copy(wg_hbm.at[layer, :, cols], wg_buf.at[slot], sem.at[0, slot]),
        pltpu.make_async_copy(wu_hbm.at[layer, :, cols], wu_buf.at[slot], sem.at[1, slot]),
        pltpu.make_async_copy(wd_hbm.at[layer, cols, :], wd_buf.at[slot], sem.at[2, slot]),
    )


def _ffn_rows_in(src, tile, buf, sem, action):
    row0 = tile * TM_FFN
    rows = pl.ds(row0 if isinstance(row0, int) else pl.multiple_of(row0, 8), TM_FFN)
    if not isinstance(src, tuple):
        action(pltpu.make_async_copy(src.at[rows], buf, sem))
        return
    xp_hbm, xs_hbm = src

    @pl.when(tile < N_TILE_FFN - 1)
    def _():
        action(pltpu.make_async_copy(xp_hbm.at[rows], buf, sem))

    @pl.when(tile == N_TILE_FFN - 1)
    def _():
        tail = pl.ds((N_TILE_FFN - 1) * TM_FFN, SAMPLE_ROW0_FFN)
        action(pltpu.make_async_copy(xp_hbm.at[tail], buf.at[pl.ds(0, SAMPLE_ROW0_FFN)], sem))
        action(pltpu.make_async_copy(xs_hbm, buf.at[pl.ds(SAMPLE_ROW0_FFN, N_SAMPLE)], sem))


def _start(copy):
    copy.start()


def _wait(copy):
    copy.wait()


def _ffn_kernel(*refs, layer, split):
    if split:
        xp_hbm, xs_hbm, *refs = refs
        src = (xp_hbm, xs_hbm)
    else:
        src, *refs = refs
    (g_ref, wg_hbm, wu_hbm, wd_hbm, o_hbm,
     tbufs, xn_ref, wg_buf, wu_buf, wd_buf, w_sem, in_sem, out_sem) = refs
    tile = pl.program_id(0)
    cur = tile % 2
    nxt = 1 - cur
    tbuf = tbufs.at[cur]
    w_copies = functools.partial(_ffn_weight_copies, layer, w_hbm=(wg_hbm, wu_hbm, wd_hbm),
                                 w_buf=(wg_buf, wu_buf, wd_buf), sem=w_sem)

    def rows_out(t, slot):
        rows = pl.ds(pl.multiple_of(t * TM_FFN, 8), TM_FFN)
        return pltpu.make_async_copy(tbufs.at[slot], o_hbm.at[rows], out_sem.at[slot])

    @pl.when(tile == 0)
    def _():
        for c in w_copies(0):
            c.start()
        _ffn_rows_in(src, 0, tbufs.at[0], in_sem.at[0], _start)

    _ffn_rows_in(src, tile, tbuf, in_sem.at[cur], _wait)
    xn_ref[...] = _rms_bf16(tbuf[...], g_ref[...])

    def block(slot, half):
        cols = slice(half * TF, (half + 1) * TF)
        xn = xn_ref[...]
        gt = _dot(xn, wg_buf[slot, :, cols].astype(BF16))
        up = _dot(xn, wu_buf[slot, :, cols].astype(BF16))
        hd = (gt * jax.nn.sigmoid(gt)) * up * 0.5
        tbuf[...] += _dot(hd.astype(BF16), wd_buf[slot, cols, :].astype(BF16))

    def block_pair(jp, carry):
        @pl.when((jp == 1) & (tile >= 1))
        def _():
            rows_out(tile - 1, nxt).wait()

        @pl.when((jp == 1) & (tile + 1 < N_TILE_FFN))
        def _():
            _ffn_rows_in(src, tile + 1, tbufs.at[nxt], in_sem.at[nxt], _start)

        pair = tile * N_FPAIR + jp
        for c in w_copies(pair + 1):
            c.start()
        for c in w_copies(pair):
            c.wait()
        slot = pair % W_SLOTS
        block(slot, 0)
        block(slot, 1)
        return carry

    lax.fori_loop(0, N_FPAIR, block_pair, 0)
    rows_out(tile, cur).start()

    @pl.when(tile == N_TILE_FFN - 1)
    def _():
        rows_out(tile, cur).wait()
        for c in w_copies(N_PAIR_FFN):
            c.wait()


def _ffn(h, gain, w_gate, w_up, w_down, layer):
    split = isinstance(h, tuple)
    hbm = pl.BlockSpec(memory_space=pl.ANY)
    return pl.pallas_call(
        functools.partial(_ffn_kernel, layer=layer, split=split),
        grid=(N_TILE_FFN,),
        in_specs=[hbm] * (2 if split else 1) + [
            pl.BlockSpec((None, 1, D_MODEL), lambda i: (layer, 0, 0)), hbm, hbm, hbm],
        out_specs=hbm,
        out_shape=jax.ShapeDtypeStruct((N_TOK, D_MODEL), F32),
        scratch_shapes=[
            pltpu.VMEM((2, TM_FFN, D_MODEL), F32),
            pltpu.VMEM((TM_FFN, D_MODEL), BF16),
            pltpu.VMEM((W_SLOTS, D_MODEL, TF2), F32),
            pltpu.VMEM((W_SLOTS, D_MODEL, TF2), F32),
            pltpu.VMEM((W_SLOTS, TF2, D_MODEL), F32),
            pltpu.SemaphoreType.DMA((3, W_SLOTS)),
            pltpu.SemaphoreType.DMA((2,)),
            pltpu.SemaphoreType.DMA((2,)),
        ],
        compiler_params=_ARB1,
        name="ffn",
    )(*(h if split else (h,)), gain, w_gate, w_up, w_down)


W_STAGE_ROWS = 128


def _stage_weight(w_hbm, jm, w_buf, stage, sem):
    n_rows, width = w_buf.shape
    n_chunk = n_rows // W_STAGE_ROWS

    def chunk_rows(c):
        return pl.ds(pl.multiple_of(c * W_STAGE_ROWS, W_STAGE_ROWS), W_STAGE_ROWS)

    def chunk_copy(c, slot):
        return pltpu.make_async_copy(w_hbm.at[jm, chunk_rows(c), :],
                                     stage.at[slot, :, pl.ds(0, width)], sem.at[slot])

    chunk_copy(0, 0).start()

    def chunk(c, carry):
        slot = c % 2

        @pl.when(c + 1 < n_chunk)
        def _():
            chunk_copy(c + 1, 1 - slot).start()

        chunk_copy(c, slot).wait()
        w_buf[chunk_rows(c), :] = stage[slot, :, pl.ds(0, width)].astype(BF16)
        return carry

    lax.fori_loop(0, n_chunk, chunk, 0)


def _gmlp_kernel(h_ref, g_ref, win_hbm, wout_hbm, lng_ref, lnb_ref, ws_ref, bs_ref,
                 o_ref, vo_ref,
                 win_buf, wout_buf, xn_ref, v_ref, y_ref, stage, sem, *, jm):
    tile = pl.program_id(0)
    is_last_tile = tile == N_TILE - 1

    @pl.when(tile == 0)
    def _():
        _stage_weight(win_hbm, jm, win_buf, stage, sem)
        _stage_weight(wout_hbm, jm, wout_buf, stage, sem)

    xn_ref[...] = _rms_bf16(h_ref[...], g_ref[...])
    xn = xn_ref[...]

    def head_cols(first, head):
        return slice(first + head * D_HEAD, first + (head + 1) * D_HEAD)

    v_sum = jnp.zeros((TM, 1), F32)
    for head in range(N_HEAD):
        v = _gelu_tanh(_dot(xn, win_buf[:, head_cols(E_MIX, head)]))
        v_ref[head] = v
        v_sum = v_sum + jnp.sum(v, axis=-1, keepdims=True)
    mu = v_sum * (1.0 / E_MIX)
    var = jnp.zeros((TM, 1), F32)
    for head in range(N_HEAD):
        d = v_ref[head] - mu
        var = var + jnp.sum(d * d, axis=-1, keepdims=True)
    rstd = lax.rsqrt(var * (1.0 / E_MIX) + EPS)

    row = lax.broadcasted_iota(jnp.int32, (CHUNK, CHUNK), 0)
    col = lax.broadcasted_iota(jnp.int32, (CHUNK, CHUNK), 1)
    n_chunk = TM // CHUNK
    for head in range(N_HEAD):
        u = _gelu_tanh(_dot(xn, win_buf[:, head_cols(0, head)]))
        vn = (v_ref[head] - mu) * rstd * lng_ref[head] + lnb_ref[head]

        vo_ref[head] = vn[SAMPLE_ROW0:, :]
        w = ws_ref[head]
        b = bs_ref[head]
        w_causal = jnp.where(col <= row, w, 0.0)
        w_sample = jnp.where(col == row, w[0:1, 0:1], 0.0)
        b_sample = jnp.broadcast_to(b[0:1, :], (CHUNK, 1))
        w_last = jnp.where(is_last_tile, w_sample, w_causal).astype(BF16)
        b_last = jnp.where(is_last_tile, b_sample, b)
        w_causal = w_causal.astype(BF16)
        for c in range(n_chunk):
            rows = slice(c * CHUNK, (c + 1) * CHUNK)
            wm, bm = (w_last, b_last) if c == n_chunk - 1 else (w_causal, b)
            s = _dot(wm, vn[rows, :].astype(BF16)) + bm
            y_ref[rows, head_cols(0, head)] = (u[rows, :] * s).astype(BF16)

    o_ref[...] = h_ref[...] + _dot(y_ref[...], wout_buf[...])


def _gmlp(h, gain, w_in, ln_g, ln_b, w_s, b_s, w_out, layer, jm):
    hbm = pl.BlockSpec(memory_space=pl.ANY)
    return pl.pallas_call(
        functools.partial(_gmlp_kernel, jm=jm),
        grid=(N_TILE,),
        in_specs=[
            pl.BlockSpec((TM, D_MODEL), lambda i: (i, 0)),
            pl.BlockSpec((None, 1, D_MODEL), lambda i: (layer, 0, 0)),
            hbm, hbm,
            pl.BlockSpec((None, N_HEAD, 1, D_HEAD), lambda i: (jm, 0, 0, 0)),
            pl.BlockSpec((None, N_HEAD, 1, D_HEAD), lambda i: (jm, 0, 0, 0)),
            pl.BlockSpec((None, N_HEAD, CHUNK, CHUNK), lambda i: (jm, 0, 0, 0)),
            pl.BlockSpec((None, N_HEAD, CHUNK, 1), lambda i: (jm, 0, 0, 0)),
        ],
        out_specs=[
            pl.BlockSpec((TM, D_MODEL), lambda i: (i, 0)),
            pl.BlockSpec((N_HEAD, N_SAMPLE, D_HEAD), lambda i: (0, 0, 0)),
        ],
        out_shape=[
            jax.ShapeDtypeStruct((N_TOK, D_MODEL), F32),
            jax.ShapeDtypeStruct((N_HEAD, N_SAMPLE, D_HEAD), F32),
        ],
        scratch_shapes=[
            pltpu.VMEM((D_MODEL, 2 * E_MIX), BF16),
            pltpu.VMEM((E_MIX, D_MODEL), BF16),
            pltpu.VMEM((TM, D_MODEL), BF16),
            pltpu.VMEM((N_HEAD, TM, D_HEAD), F32),
            pltpu.VMEM((TM, E_MIX), BF16),
            pltpu.VMEM((2, W_STAGE_ROWS, 2 * E_MIX), F32),
            pltpu.SemaphoreType.DMA((2,)),
        ],
        compiler_params=_ARB1,
        name="gmlp",
    )(h, gain, w_in, w_out, ln_g, ln_b, w_s, b_s)


_HALO = 8
CB = 256
N_CB = E_MIX // CB
CONV_PAIRS = N_CB // 2
CONV_SLOTS = 4
assert CONV_PAIRS % 2 == 0

_SEQ_TAIL = [divmod((s + 1) * SEQ - _HALO, TM) for s in range(N_PROMPT // SEQ)]


def _conv_kernel(h_ref, g_ref, win_hbm, wout_hbm, wk_ref, st0_ref, st1_ref,
                 o_ref, cs_ref, cp_ref,
                 win_buf, wout_buf, xn_ref, buf_ref, carry_ref, tails_ref, y_ref,
                 sem_in, sem_out, *, jm):
    tile = pl.program_id(0)
    is_last_tile = tile == N_TILE - 1

    def win_copy(block, slot):
        return pltpu.make_async_copy(win_hbm.at[jm, block], win_buf.at[slot], sem_in.at[slot])

    wout_copy = pltpu.make_async_copy(wout_hbm.at[jm], wout_buf, sem_out.at[0])

    @pl.when(tile == 0)
    def _():
        wout_copy.start()
        win_copy(0, 0).start()
        win_copy(1, 1).start()
        carry_ref[...] = jnp.zeros_like(carry_ref)

    xn_ref[...] = _rms_bf16(h_ref[...], g_ref[...])
    xn = xn_ref[...]
    pos = (lax.broadcasted_iota(jnp.int32, (TM, 1), 0) + tile * TM) & (SEQ - 1)

    def conv_block(block, slot):
        cols = slice(block * CB, (block + 1) * CB)
        bg = _dot(xn, win_buf[slot, 0])
        ci = _dot(xn, win_buf[slot, 1]) * _dot(xn, win_buf[slot, 2])

        for s, (_, row0) in enumerate(_SEQ_TAIL):
            tails_ref[s, block] = ci[row0:row0 + _HALO, :]
        ci_s = ci[SAMPLE_ROW0:, :]
        cs_ref[block] = ci_s

        buf = buf_ref.at[block % 2]
        buf[0:_HALO, :] = carry_ref[block]
        buf[_HALO:, :] = ci
        carry_ref[block] = ci[TM - _HALO:, :]
        prev1 = jnp.where(pos >= 1, buf[_HALO - 1:_HALO - 1 + TM, :], 0.0)
        prev2 = jnp.where(pos >= 2, buf[_HALO - 2:_HALO - 2 + TM, :], 0.0)
        w0 = wk_ref[0:1, cols]
        w1 = wk_ref[1:2, cols]
        w2 = wk_ref[2:3, cols]
        co = w0 * prev2 + w1 * prev1 + w2 * ci
        co_s = w0 * st0_ref[:, cols] + w1 * st1_ref[:, cols] + w2 * ci_s
        co_tail = jnp.where(is_last_tile, co_s, co[SAMPLE_ROW0:, :])
        y_ref[0:SAMPLE_ROW0, cols] = (bg[0:SAMPLE_ROW0, :] * co[0:SAMPLE_ROW0, :]).astype(BF16)
        y_ref[SAMPLE_ROW0:, cols] = (bg[SAMPLE_ROW0:, :] * co_tail).astype(BF16)

    for pair in range(CONV_PAIRS):
        blocks = (2 * pair, 2 * pair + 1)
        slots = (0, 1) if pair % 2 == 0 else (2, 3)
        free_slots = (2, 3) if pair % 2 == 0 else (0, 1)
        if pair == 1:
            @pl.when(tile == 0)
            def _():
                wout_copy.wait()
        for b, slot in zip(blocks, free_slots):
            win_copy((b + 2) % N_CB, slot).start()
        for b, slot in zip(blocks, slots):
            win_copy(b, slot).wait()
        for b, slot in zip(blocks, slots):
            conv_block(b, slot)

    o_ref[...] = h_ref[...] + _dot(y_ref[...], wout_buf[...])

    for s, (seq_tile, _) in enumerate(_SEQ_TAIL):
        @pl.when(tile == seq_tile)
        def _():
            cp_ref[s] = tails_ref[s]

    @pl.when(is_last_tile)
    def _():
        for b in (0, 1):
            win_copy(b, b).wait()


def _conv_w_in_blocks_kernel(wb_ref, wc_ref, wx_ref, o_ref):
    for part, w_ref in enumerate((wb_ref, wc_ref, wx_ref)):
        o_ref[part] = w_ref[...].astype(BF16)


def _conv_w_in_blocks(w_in):
    n_b = w_in.shape[0]

    def part_spec(part):
        return pl.BlockSpec((None, D_MODEL, CB), lambda m, b: (m, 0, part * N_CB + b))

    return pl.pallas_call(
        _conv_w_in_blocks_kernel,
        grid=(n_b, N_CB),
        in_specs=[part_spec(0), part_spec(1), part_spec(2)],
        out_specs=pl.BlockSpec((None, None, 3, D_MODEL, CB), lambda m, b: (m, b, 0, 0, 0)),
        out_shape=jax.ShapeDtypeStruct((n_b, N_CB, 3, D_MODEL, CB), BF16),
        compiler_params=pltpu.CompilerParams(dimension_semantics=("arbitrary", "arbitrary")),
        name="conv_w_in_blocks",
    )(w_in, w_in, w_in)


def _conv(h, gain, w_in_blocked, w_conv, st0, st1, w_out, layer, jm):
    n_seq = N_PROMPT // SEQ
    hbm = pl.BlockSpec(memory_space=pl.ANY)
    return pl.pallas_call(
        functools.partial(_conv_kernel, jm=jm),
        grid=(N_TILE,),
        in_specs=[
            pl.BlockSpec((TM, D_MODEL), lambda i: (i, 0)),
            pl.BlockSpec((None, 1, D_MODEL), lambda i: (layer, 0, 0)),
            hbm, hbm,
            pl.BlockSpec((None, 3, E_MIX), lambda i: (jm, 0, 0)),
            pl.BlockSpec((N_SAMPLE, E_MIX), lambda i: (0, 0)),
            pl.BlockSpec((N_SAMPLE, E_MIX), lambda i: (0, 0)),
        ],
        out_specs=[
            pl.BlockSpec((TM, D_MODEL), lambda i: (i, 0)),
            pl.BlockSpec((N_CB, N_SAMPLE, CB), lambda i: (0, 0, 0)),
            pl.BlockSpec((n_seq, N_CB, _HALO, CB), lambda i: (0, 0, 0, 0)),
        ],
        out_shape=[
            jax.ShapeDtypeStruct((N_TOK, D_MODEL), F32),
            jax.ShapeDtypeStruct((N_CB, N_SAMPLE, CB), F32),
            jax.ShapeDtypeStruct((n_seq, N_CB, _HALO, CB), F32),
        ],
        scratch_shapes=[
            pltpu.VMEM((CONV_SLOTS, 3, D_MODEL, CB), BF16),
            pltpu.VMEM((E_MIX, D_MODEL), BF16),
            pltpu.VMEM((TM, D_MODEL), BF16),
            pltpu.VMEM((2, TM + _HALO, CB), F32),
            pltpu.VMEM((N_CB, _HALO, CB), F32),
            pltpu.VMEM((n_seq, N_CB, _HALO, CB), F32),
            pltpu.VMEM((TM, E_MIX), BF16),
            pltpu.SemaphoreType.DMA((CONV_SLOTS,)),
            pltpu.SemaphoreType.DMA((1,)),
        ],
        compiler_params=_ARB1,
        name="sconv",
    )(h, gain, w_in_blocked, w_out, w_conv, st0, st1)


def _ple_kernel(h_ref, g_ref, wg_ref, pp_ref, ps_ref, wp_ref, gf_ref, o_ref, *rest, final_norm):
    if final_norm:
        os_ref, pb_ref = rest
    else:
        (pb_ref,) = rest
    i = pl.program_id(0)

    @pl.when(i < N_TILE - 1)
    def _():
        pb_ref[...] = pp_ref[...].astype(BF16)

    @pl.when(i == N_TILE - 1)
    def _():
        pb_ref[0:SAMPLE_ROW0, :] = pp_ref[0:SAMPLE_ROW0, :].astype(BF16)
        pb_ref[SAMPLE_ROW0:, :] = ps_ref[...].astype(BF16)

    xn = _rms_bf16(h_ref[...], g_ref[...])
    pb = pb_ref[...]
    for c in range(D_MODEL // TN):
        cols = slice(c * TN, (c + 1) * TN)
        gate = jax.nn.sigmoid(_dot(xn, wg_ref[:, cols].astype(BF16)))
        o_ref[:, cols] = h_ref[:, cols] + gate * _dot(pb, wp_ref[:, cols].astype(BF16))
    if final_norm:
        hn = o_ref[...]
        ms = jnp.mean(hn * hn, axis=-1, keepdims=True)
        o_ref[...] = hn * lax.rsqrt(ms + EPS) * gf_ref[...]

        @pl.when(i == N_TILE - 1)
        def _():
            os_ref[...] = o_ref[SAMPLE_ROW0:, :]


def _ple(h, gain, w_gate, p_prompt, p_sample, w_proj, g_final, layer, final_norm):
    row_block = pl.BlockSpec((TM, D_MODEL), lambda i: (i, 0))
    if final_norm:
        out_specs = [row_block, pl.BlockSpec((N_SAMPLE, D_MODEL), lambda i: (0, 0))]
        out_shape = [jax.ShapeDtypeStruct((N_PROMPT, D_MODEL), F32),
                     jax.ShapeDtypeStruct((N_SAMPLE, D_MODEL), F32)]
    else:
        out_specs = row_block
        out_shape = jax.ShapeDtypeStruct((N_TOK, D_MODEL), F32)
    return pl.pallas_call(
        functools.partial(_ple_kernel, final_norm=final_norm),
        grid=(N_TILE,),
        in_specs=[
            row_block,
            pl.BlockSpec((None, 1, D_MODEL), lambda i: (layer, 0, 0)),
            pl.BlockSpec((None, D_MODEL, D_MODEL), lambda i: (layer, 0, 0),
                         pipeline_mode=pl.Buffered(1)),
            pl.BlockSpec((None, TM, D_PLE), lambda i: (layer, i, 0)),
            pl.BlockSpec((None, N_SAMPLE, D_PLE), lambda i: (layer, 0, 0)),
            pl.BlockSpec((None, D_PLE, D_MODEL), lambda i: (layer, 0, 0),
                         pipeline_mode=pl.Buffered(1)),
            pl.BlockSpec((1, D_MODEL), lambda i: (0, 0)),
        ],
        out_specs=out_specs,
        out_shape=out_shape,
        scratch_shapes=[pltpu.VMEM((TM, D_PLE), BF16)],
        compiler_params=_ARB1,
        name="ple",
    )(h, gain, w_gate, p_prompt, p_sample, w_proj, g_final)


def kernel(x_prompt, x_sample, state_conv, p_prompt, p_sample, ffn1_norm, ffn1_w_gate, ffn1_w_up, ffn1_w_down, mix_norm, a_w_in, a_ln_g, a_ln_b, a_w_s, a_b_s, a_w_out, c_w_in, c_w_conv, c_w_out, ffn2_norm, ffn2_w_gate, ffn2_w_up, ffn2_w_down, ple_norm, ple_w_gate, ple_w_proj, final_norm):
    bf = lambda w: w.astype(BF16)
    gain3 = lambda g: g.reshape(g.shape[0], 1, D_MODEL)

    h = (x_prompt.reshape(N_PROMPT, D_MODEL), x_sample.reshape(N_SAMPLE, D_MODEL))
    p_prompt = p_prompt.reshape(DEPTH, N_PROMPT, D_PLE)
    p_sample = p_sample.reshape(DEPTH, N_SAMPLE, D_PLE)

    f1 = (gain3(ffn1_norm), ffn1_w_gate, ffn1_w_up, ffn1_w_down)
    f2 = (gain3(ffn2_norm), ffn2_w_gate, ffn2_w_up, ffn2_w_down)
    mix_gain = gain3(mix_norm)
    ple_gain = gain3(ple_norm)
    a_w_in_b, a_w_out_b = a_w_in, a_w_out
    c_w_in_b = _conv_w_in_blocks(c_w_in)
    c_w_out_b = bf(c_w_out)
    ple_w_gate_b, ple_w_proj_b = ple_w_gate, ple_w_proj
    ln_g = a_ln_g.reshape(-1, N_HEAD, 1, D_HEAD)
    ln_b = a_ln_b.reshape(-1, N_HEAD, 1, D_HEAD)
    b_s = a_b_s.reshape(-1, N_HEAD, CHUNK, 1)
    g_final = final_norm.reshape(1, D_MODEL)

    conv_new_prompt, conv_new_sample, v_new = [], [], []
    for layer in range(DEPTH):
        h = _ffn(h, *f1, layer)
        jm = layer // 2
        if layer % 2 == 0:
            h, v_s = _gmlp(h, mix_gain, a_w_in_b, ln_g, ln_b, a_w_s, b_s, a_w_out_b, layer, jm)
            v_new.append(v_s.transpose(1, 0, 2).reshape(N_SAMPLE, 1, E_MIX))
        else:
            st0 = state_conv[jm, :, 0, :]
            st1 = state_conv[jm, :, 1, :]
            h, ci_s, ci_p = _conv(h, mix_gain, c_w_in_b, c_w_conv, st0, st1, c_w_out_b, layer, jm)
            ci_s = ci_s.transpose(1, 0, 2).reshape(N_SAMPLE, E_MIX)
            ci_p = ci_p.transpose(0, 2, 1, 3).reshape(N_PROMPT // SEQ, _HALO, E_MIX)
            conv_new_prompt.append(ci_p[:, _HALO - 2:, :])
            conv_new_sample.append(jnp.stack([st1, ci_s], axis=1))
        h = _ffn(h, *f2, layer)
        h = _ple(h, ple_gain, ple_w_gate_b, p_prompt, p_sample, ple_w_proj_b, g_final, layer,
                 final_norm=(layer == DEPTH - 1))

    y_prompt, y_sample = h
    return (y_prompt.reshape(N_PROMPT // SEQ, SEQ, D_MODEL), y_sample.reshape(N_SAMPLE, 1, D_MODEL),
            jnp.stack(conv_new_prompt), jnp.stack(conv_new_sample), jnp.stack(v_new))
```

```python
import functools

import jax
import jax.numpy as jnp
from jax import lax
from jax.experimental import pallas as pl
from jax.experimental.pallas import tpu as pltpu

F32 = jnp.float32
BF16 = jnp.bfloat16

D_MODEL = 2048
D_FF = 5632
E_MIX = 2048
N_HEAD = 8
D_HEAD = 256
CHUNK = 128
D_PLE = 256
SEQ = 2048
N_PROMPT = 4 * SEQ
N_SAMPLE = 128
N_TOK = N_PROMPT + N_SAMPLE
DEPTH = 2
EPS = 1e-6

TM = 640
N_TILE = N_TOK // TM
SAMPLE_ROW0 = TM - N_SAMPLE
TM_FFN = 1664
N_TILE_FFN = N_TOK // TM_FFN
SAMPLE_ROW0_FFN = TM_FFN - N_SAMPLE
TF = 256
TN = 512
V7X_VMEM_BYTES = 64 * 1024 * 1024
VMEM_LIMIT = V7X_VMEM_BYTES - 2 * 1024 * 1024

_ARB1 = pltpu.CompilerParams(dimension_semantics=("arbitrary",),
                             vmem_limit_bytes=VMEM_LIMIT)


def _rms_bf16(x, g):
    ms = jnp.mean(x * x, axis=-1, keepdims=True)
    return (x * lax.rsqrt(ms + EPS) * g).astype(BF16)


def _dot(a, b):
    return jnp.dot(a, b, preferred_element_type=F32)


def _gelu_tanh(x):
    c = 0.7978845608028654
    return x * (0.5 * (1.0 + jnp.tanh(c * (x + 0.044715 * (x * x * x)))))


TF2 = 2 * TF
N_FPAIR = D_FF // TF2
N_PAIR_FFN = N_TILE_FFN * N_FPAIR
W_SLOTS = 2


def _ffn_weight_copies(layer, pair, w_hbm, w_buf, sem):
    wg_hbm, wu_hbm, wd_hbm = w_hbm
    wg_buf, wu_buf, wd_buf = w_buf
    slot = pair % W_SLOTS
    col0 = (pair % N_FPAIR) * TF2
    cols = pl.ds(col0 if isinstance(col0, int) else pl.multiple_of(col0, TF2), TF2)
    return (
        pltpu.make_async_copy(wg_hbm.at[layer, :, cols], wg_buf.at[slot], sem.at[0, slot]),
        pltpu.make_async_copy(wu_hbm.at[layer, :, cols], wu_buf.at[slot], sem.at[1, slot]),
        pltpu.make_async_copy(wd_hbm.at[layer, cols, :], wd_buf.at[slot], sem.at[2, slot]),
    )


def _ffn_rows_in(src, tile, buf, sem, action):
    row0 = tile * TM_FFN
    rows = pl.ds(row0 if isinstance(row0, int) else pl.multiple_of(row0, 8), TM_FFN)
    if not isinstance(src, tuple):
        action(pltpu.make_async_copy(src.at[rows], buf, sem))
        return
    xp_hbm, xs_hbm = src

    @pl.when(tile < N_TILE_FFN - 1)
    def _():
        action(pltpu.make_async_copy(xp_hbm.at[rows], buf, sem))

    @pl.when(tile == N_TILE_FFN - 1)
    def _():
        tail = pl.ds((N_TILE_FFN - 1) * TM_FFN, SAMPLE_ROW0_FFN)
        action(pltpu.make_async_copy(xp_hbm.at[tail], buf.at[pl.ds(0, SAMPLE_ROW0_FFN)], sem))
        action(pltpu.make_async_copy(xs_hbm, buf.at[pl.ds(SAMPLE_ROW0_FFN, N_SAMPLE)], sem))


def _start(copy):
    copy.start()


def _wait(copy):
    copy.wait()


def _ffn_kernel(*refs, layer, split):
    if split:
        xp_hbm, xs_hbm, *refs = refs
        src = (xp_hbm, xs_hbm)
    else:
        src, *refs = refs
    (g_ref, wg_hbm, wu_hbm, wd_hbm, o_hbm,
     tbufs, xn_ref, wg_buf, wu_buf, wd_buf, w_sem, in_sem, out_sem) = refs
    tile = pl.program_id(0)
    cur = tile % 2
    nxt = 1 - cur
    tbuf = tbufs.at[cur]
    w_copies = functools.partial(_ffn_weight_copies, layer, w_hbm=(wg_hbm, wu_hbm, wd_hbm),
                                 w_buf=(wg_buf, wu_buf, wd_buf), sem=w_sem)

    def rows_out(t, slot):
        rows = pl.ds(pl.multiple_of(t * TM_FFN, 8), TM_FFN)
        return pltpu.make_async_copy(tbufs.at[slot], o_hbm.at[rows], out_sem.at[slot])

    @pl.when(tile == 0)
    def _():
        for c in w_copies(0):
            c.start()
        _ffn_rows_in(src, 0, tbufs.at[0], in_sem.at[0], _start)

    _ffn_rows_in(src, tile, tbuf, in_sem.at[cur], _wait)
    xn_ref[...] = _rms_bf16(tbuf[...], g_ref[...])

    def block(slot, half):
        cols = slice(half * TF, (half + 1) * TF)
        xn = xn_ref[...]
        gt = _dot(xn, wg_buf[slot, :, cols].astype(BF16))
        up = _dot(xn, wu_buf[slot, :, cols].astype(BF16))
        hd = (gt * jax.nn.sigmoid(gt)) * up * 0.5
        tbuf[...] += _dot(hd.astype(BF16), wd_buf[slot, cols, :].astype(BF16))

    def block_pair(jp, carry):
        @pl.when((jp == 1) & (tile >= 1))
        def _():
            rows_out(tile - 1, nxt).wait()

        @pl.when((jp == 1) & (tile + 1 < N_TILE_FFN))
        def _():
            _ffn_rows_in(src, tile + 1, tbufs.at[nxt], in_sem.at[nxt], _start)

        pair = tile * N_FPAIR + jp
        for c in w_copies(pair + 1):
            c.start()
        for c in w_copies(pair):
            c.wait()
        slot = pair % W_SLOTS
        block(slot, 0)
        block(slot, 1)
        return carry

    lax.fori_loop(0, N_FPAIR, block_pair, 0)
    rows_out(tile, cur).start()

    @pl.when(tile == N_TILE_FFN - 1)
    def _():
        rows_out(tile, cur).wait()
        for c in w_copies(N_PAIR_FFN):
            c.wait()


def _ffn(h, gain, w_gate, w_up, w_down, layer):
    split = isinstance(h, tuple)
    hbm = pl.BlockSpec(memory_space=pl.ANY)
    return pl.pallas_call(
        functools.partial(_ffn_kernel, layer=layer, split=split),
        grid=(N_TILE_FFN,),
        in_specs=[hbm] * (2 if split else 1) + [
            pl.BlockSpec((None, 1, D_MODEL), lambda i: (layer, 0, 0)), hbm, hbm, hbm],
        out_specs=hbm,
        out_shape=jax.ShapeDtypeStruct((N_TOK, D_MODEL), F32),
        scratch_shapes=[
            pltpu.VMEM((2, TM_FFN, D_MODEL), F32),
            pltpu.VMEM((TM_FFN, D_MODEL), BF16),
            pltpu.VMEM((W_SLOTS, D_MODEL, TF2), F32),
            pltpu.VMEM((W_SLOTS, D_MODEL, TF2), F32),
            pltpu.VMEM((W_SLOTS, TF2, D_MODEL), F32),
            pltpu.SemaphoreType.DMA((3, W_SLOTS)),
            pltpu.SemaphoreType.DMA((2,)),
            pltpu.SemaphoreType.DMA((2,)),
        ],
        compiler_params=_ARB1,
        name="ffn",
    )(*(h if split else (h,)), gain, w_gate, w_up, w_down)


W_STAGE_ROWS = 128


def _stage_weight(w_hbm, jm, w_buf, stage, sem):
    n_rows, width = w_buf.shape
    n_chunk = n_rows // W_STAGE_ROWS

    def chunk_rows(c):
        return pl.ds(pl.multiple_of(c * W_STAGE_ROWS, W_STAGE_ROWS), W_STAGE_ROWS)

    def chunk_copy(c, slot):
        return pltpu.make_async_copy(w_hbm.at[jm, chunk_rows(c), :],
                                     stage.at[slot, :, pl.ds(0, width)], sem.at[slot])

    chunk_copy(0, 0).start()

    def chunk(c, carry):
        slot = c % 2

        @pl.when(c + 1 < n_chunk)
        def _():
            chunk_copy(c + 1, 1 - slot).start()

        chunk_copy(c, slot).wait()
        w_buf[chunk_rows(c), :] = stage[slot, :, pl.ds(0, width)].astype(BF16)
        return carry

    lax.fori_loop(0, n_chunk, chunk, 0)


def _gmlp_kernel(h_ref, g_ref, win_hbm, wout_hbm, lng_ref, lnb_ref, ws_ref, bs_ref,
                 o_ref, vo_ref,
                 win_buf, wout_buf, xn_ref, v_ref, y_ref, stage, sem, *, jm):
    tile = pl.program_id(0)
    is_last_tile = tile == N_TILE - 1

    @pl.when(tile == 0)
    def _():
        _stage_weight(win_hbm, jm, win_buf, stage, sem)
        _stage_weight(wout_hbm, jm, wout_buf, stage, sem)

    xn_ref[...] = _rms_bf16(h_ref[...], g_ref[...])
    xn = xn_ref[...]

    def head_cols(first, head):
        return slice(first + head * D_HEAD, first + (head + 1) * D_HEAD)

    v_sum = jnp.zeros((TM, 1), F32)
    for head in range(N_HEAD):
        v = _gelu_tanh(_dot(xn, win_buf[:, head_cols(E_MIX, head)]))
        v_ref[head] = v
        v_sum = v_sum + jnp.sum(v, axis=-1, keepdims=True)
    mu = v_sum * (1.0 / E_MIX)
    var = jnp.zeros((TM, 1), F32)
    for head in range(N_HEAD):
        d = v_ref[head] - mu
        var = var + jnp.sum(d * d, axis=-1, keepdims=True)
    rstd = lax.rsqrt(var * (1.0 / E_MIX) + EPS)

    row = lax.broadcasted_iota(jnp.int32, (CHUNK, CHUNK), 0)
    col = lax.broadcasted_iota(jnp.int32, (CHUNK, CHUNK), 1)
    n_chunk = TM // CHUNK
    for head in range(N_HEAD):
        u = _gelu_tanh(_dot(xn, win_buf[:, head_cols(0, head)]))
        vn = (v_ref[head] - mu) * rstd * lng_ref[head] + lnb_ref[head]

        vo_ref[head] = vn[SAMPLE_ROW0:, :]
        w = ws_ref[head]
        b = bs_ref[head]
        w_causal = jnp.where(col <= row, w, 0.0)
        w_sample = jnp.where(col == row, w[0:1, 0:1], 0.0)
        b_sample = jnp.broadcast_to(b[0:1, :], (CHUNK, 1))
        w_last = jnp.where(is_last_tile, w_sample, w_causal).astype(BF16)
        b_last = jnp.where(is_last_tile, b_sample, b)
        w_causal = w_causal.astype(BF16)
        for c in range(n_chunk):
            rows = slice(c * CHUNK, (c + 1) * CHUNK)
            wm, bm = (w_last, b_last) if c == n_chunk - 1 else (w_causal, b)
            s = _dot(wm, vn[rows, :].astype(BF16)) + bm
            y_ref[rows, head_cols(0, head)] = (u[rows, :] * s).astype(BF16)

    o_ref[...] = h_ref[...] + _dot(y_ref[...], wout_buf[...])


def _gmlp(h, gain, w_in, ln_g, ln_b, w_s, b_s, w_out, layer, jm):
    hbm = pl.BlockSpec(memory_space=pl.ANY)
    return pl.pallas_call(
        functools.partial(_gmlp_kernel, jm=jm),
        grid=(N_TILE,),
        in_specs=[
            pl.BlockSpec((TM, D_MODEL), lambda i: (i, 0)),
            pl.BlockSpec((None, 1, D_MODEL), lambda i: (layer, 0, 0)),
            hbm, hbm,
            pl.BlockSpec((None, N_HEAD, 1, D_HEAD), lambda i: (jm, 0, 0, 0)),
            pl.BlockSpec((None, N_HEAD, 1, D_HEAD), lambda i: (jm, 0, 0, 0)),
            pl.BlockSpec((None, N_HEAD, CHUNK, CHUNK), lambda i: (jm, 0, 0, 0)),
            pl.BlockSpec((None, N_HEAD, CHUNK, 1), lambda i: (jm, 0, 0, 0)),
        ],
        out_specs=[
            pl.BlockSpec((TM, D_MODEL), lambda i: (i, 0)),
            pl.BlockSpec((N_HEAD, N_SAMPLE, D_HEAD), lambda i: (0, 0, 0)),
        ],
        out_shape=[
            jax.ShapeDtypeStruct((N_TOK, D_MODEL), F32),
            jax.ShapeDtypeStruct((N_HEAD, N_SAMPLE, D_HEAD), F32),
        ],
        scratch_shapes=[
            pltpu.VMEM((D_MODEL, 2 * E_MIX), BF16),
            pltpu.VMEM((E_MIX, D_MODEL), BF16),
            pltpu.VMEM((TM, D_MODEL), BF16),
            pltpu.VMEM((N_HEAD, TM, D_HEAD), F32),
            pltpu.VMEM((TM, E_MIX), BF16),
            pltpu.VMEM((2, W_STAGE_ROWS, 2 * E_MIX), F32),
            pltpu.SemaphoreType.DMA((2,)),
        ],
        compiler_params=_ARB1,
        name="gmlp",
    )(h, gain, w_in, w_out, ln_g, ln_b, w_s, b_s)


_HALO = 8
CB = 256
N_CB = E_MIX // CB
CONV_PAIRS = N_CB // 2
CONV_SLOTS = 4
assert CONV_PAIRS % 2 == 0

_SEQ_TAIL = [divmod((s + 1) * SEQ - _HALO, TM) for s in range(N_PROMPT // SEQ)]


def _conv_kernel(h_ref, g_ref, win_hbm, wout_hbm, wk_ref, st0_ref, st1_ref,
                 o_ref, cs_ref, cp_ref,
                 win_buf, wout_buf, xn_ref, buf_ref, carry_ref, tails_ref, y_ref,
                 sem_in, sem_out, *, jm):
    tile = pl.program_id(0)
    is_last_tile = tile == N_TILE - 1

    def win_copy(pair):
        half = pair % 2
        return pltpu.make_async_copy(win_hbm.at[jm, pl.ds(2 * pair, 2)],
                                     win_buf.at[pl.ds(2 * half, 2)], sem_in.at[half])

    wout_copy = pltpu.make_async_copy(wout_hbm.at[jm], wout_buf, sem_out.at[0])

    @pl.when(tile == 0)
    def _():
        wout_copy.start()
        win_copy(0).start()
        carry_ref[...] = jnp.zeros_like(carry_ref)

    xn_ref[...] = _rms_bf16(h_ref[...], g_ref[...])
    xn = xn_ref[...]
    pos = (lax.broadcasted_iota(jnp.int32, (TM, 1), 0) + tile * TM) & (SEQ - 1)

    def conv_block(block, slot):
        cols = slice(block * CB, (block + 1) * CB)
        bg = _dot(xn, win_buf[slot, 0])
        ci = _dot(xn, win_buf[slot, 1]) * _dot(xn, win_buf[slot, 2])

        for s, (_, row0) in enumerate(_SEQ_TAIL):
            tails_ref[s, block] = ci[row0:row0 + _HALO, :]
        ci_s = ci[SAMPLE_ROW0:, :]
        cs_ref[block] = ci_s

        buf = buf_ref.at[block % 2]
        buf[0:_HALO, :] = carry_ref[block]
        buf[_HALO:, :] = ci
        carry_ref[block] = ci[TM - _HALO:, :]
        prev1 = jnp.where(pos >= 1, buf[_HALO - 1:_HALO - 1 + TM, :], 0.0)
        prev2 = jnp.where(pos >= 2, buf[_HALO - 2:_HALO - 2 + TM, :], 0.0)
        w0 = wk_ref[0:1, cols]
        w1 = wk_ref[1:2, cols]
        w2 = wk_ref[2:3, cols]
        co = w0 * prev2 + w1 * prev1 + w2 * ci
        co_s = w0 * st0_ref[:, cols] + w1 * st1_ref[:, cols] + w2 * ci_s
        co_tail = jnp.where(is_last_tile, co_s, co[SAMPLE_ROW0:, :])
        y_ref[0:SAMPLE_ROW0, cols] = (bg[0:SAMPLE_ROW0, :] * co[0:SAMPLE_ROW0, :]).astype(BF16)
        y_ref[SAMPLE_ROW0:, cols] = (bg[SAMPLE_ROW0:, :] * co_tail).astype(BF16)

    for pair in range(CONV_PAIRS):
        blocks = (2 * pair, 2 * pair + 1)
        slots = (0, 1) if pair % 2 == 0 else (2, 3)
        if pair == 1:
            @pl.when(tile == 0)
            def _():
                wout_copy.wait()
        win_copy((pair + 1) % CONV_PAIRS).start()
        win_copy(pair).wait()
        for b, slot in zip(blocks, slots):
            conv_block(b, slot)

    o_ref[...] = h_ref[...] + _dot(y_ref[...], wout_buf[...])

    for s, (seq_tile, _) in enumerate(_SEQ_TAIL):
        @pl.when(tile == seq_tile)
        def _():
            cp_ref[s] = tails_ref[s]

    @pl.when(is_last_tile)
    def _():
        win_copy(0).wait()


def _conv_w_in_blocks_kernel(wb_ref, wc_ref, wx_ref, o_ref):
    for part, w_ref in enumerate((wb_ref, wc_ref, wx_ref)):
        o_ref[part] = w_ref[...].astype(BF16)


def _conv_w_in_blocks(w_in):
    n_b = w_in.shape[0]

    def part_spec(part):
        return pl.BlockSpec((None, D_MODEL, CB), lambda m, b: (m, 0, part * N_CB + b))

    return pl.pallas_call(
        _conv_w_in_blocks_kernel,
        grid=(n_b, N_CB),
        in_specs=[part_spec(0), part_spec(1), part_spec(2)],
        out_specs=pl.BlockSpec((None, None, 3, D_MODEL, CB), lambda m, b: (m, b, 0, 0, 0)),
        out_shape=jax.ShapeDtypeStruct((n_b, N_CB, 3, D_MODEL, CB), BF16),
        compiler_params=pltpu.CompilerParams(dimension_semantics=("arbitrary", "arbitrary")),
        name="conv_w_in_blocks",
    )(w_in, w_in, w_in)


def _conv(h, gain, w_in_blocked, w_conv, st0, st1, w_out, layer, jm):
    n_seq = N_PROMPT // SEQ
    hbm = pl.BlockSpec(memory_space=pl.ANY)
    return pl.pallas_call(
        functools.partial(_conv_kernel, jm=jm),
        grid=(N_TILE,),
        in_specs=[
            pl.BlockSpec((TM, D_MODEL), lambda i: (i, 0)),
            pl.BlockSpec((None, 1, D_MODEL), lambda i: (layer, 0, 0)),
            hbm, hbm,
            pl.BlockSpec((None, 3, E_MIX), lambda i: (jm, 0, 0)),
            pl.BlockSpec((N_SAMPLE, E_MIX), lambda i: (0, 0)),
            pl.BlockSpec((N_SAMPLE, E_MIX), lambda i: (0, 0)),
        ],
        out_specs=[
            pl.BlockSpec((TM, D_MODEL), lambda i: (i, 0)),
            pl.BlockSpec((N_CB, N_SAMPLE, CB), lambda i: (0, 0, 0)),
            pl.BlockSpec((n_seq, N_CB, _HALO, CB), lambda i: (0, 0, 0, 0)),
        ],
        out_shape=[
            jax.ShapeDtypeStruct((N_TOK, D_MODEL), F32),
            jax.ShapeDtypeStruct((N_CB, N_SAMPLE, CB), F32),
            jax.ShapeDtypeStruct((n_seq, N_CB, _HALO, CB), F32),
        ],
        scratch_shapes=[
            pltpu.VMEM((CONV_SLOTS, 3, D_MODEL, CB), BF16),
            pltpu.VMEM((E_MIX, D_MODEL), BF16),
            pltpu.VMEM((TM, D_MODEL), BF16),
            pltpu.VMEM((2, TM + _HALO, CB), F32),
            pltpu.VMEM((N_CB, _HALO, CB), F32),
            pltpu.VMEM((n_seq, N_CB, _HALO, CB), F32),
            pltpu.VMEM((TM, E_MIX), BF16),
            pltpu.SemaphoreType.DMA((CONV_SLOTS,)),
            pltpu.SemaphoreType.DMA((1,)),
        ],
        compiler_params=_ARB1,
        name="sconv",
    )(h, gain, w_in_blocked, w_out, w_conv, st0, st1)


def _ple_kernel(h_ref, g_ref, wg_ref, pp_ref, ps_ref, wp_ref, gf_ref, o_ref, *rest, final_norm):
    if final_norm:
        os_ref, pb_ref = rest
    else:
        (pb_ref,) = rest
    i = pl.program_id(0)

    @pl.when(i < N_TILE - 1)
    def _():
        pb_ref[...] = pp_ref[...].astype(BF16)

    @pl.when(i == N_TILE - 1)
    def _():
        pb_ref[0:SAMPLE_ROW0, :] = pp_ref[0:SAMPLE_ROW0, :].astype(BF16)
        pb_ref[SAMPLE_ROW0:, :] = ps_ref[...].astype(BF16)

    xn = _rms_bf16(h_ref[...], g_ref[...])
    pb = pb_ref[...]
    for c in range(D_MODEL // TN):
        cols = slice(c * TN, (c + 1) * TN)
        gate = jax.nn.sigmoid(_dot(xn, wg_ref[:, cols].astype(BF16)))
        o_ref[:, cols] = h_ref[:, cols] + gate * _dot(pb, wp_ref[:, cols].astype(BF16))
    if final_norm:
        hn = o_ref[...]
        ms = jnp.mean(hn * hn, axis=-1, keepdims=True)
        o_ref[...] = hn * lax.rsqrt(ms + EPS) * gf_ref[...]

        @pl.when(i == N_TILE - 1)
        def _():
            os_ref[...] = o_ref[SAMPLE_ROW0:, :]


def _ple(h, gain, w_gate, p_prompt, p_sample, w_proj, g_final, layer, final_norm):
    row_block = pl.BlockSpec((TM, D_MODEL), lambda i: (i, 0))
    if final_norm:
        out_specs = [row_block, pl.BlockSpec((N_SAMPLE, D_MODEL), lambda i: (0, 0))]
        out_shape = [jax.ShapeDtypeStruct((N_PROMPT, D_MODEL), F32),
                     jax.ShapeDtypeStruct((N_SAMPLE, D_MODEL), F32)]
    else:
        out_specs = row_block
        out_shape = jax.ShapeDtypeStruct((N_TOK, D_MODEL), F32)
    return pl.pallas_call(
        functools.partial(_ple_kernel, final_norm=final_norm),
        grid=(N_TILE,),
        in_specs=[
            row_block,
            pl.BlockSpec((None, 1, D_MODEL), lambda i: (layer, 0, 0)),
            pl.BlockSpec((None, D_MODEL, D_MODEL), lambda i: (layer, 0, 0),
                         pipeline_mode=pl.Buffered(1)),
            pl.BlockSpec((None, TM, D_PLE), lambda i: (layer, i, 0)),
            pl.BlockSpec((None, N_SAMPLE, D_PLE), lambda i: (layer, 0, 0)),
            pl.BlockSpec((None, D_PLE, D_MODEL), lambda i: (layer, 0, 0),
                         pipeline_mode=pl.Buffered(1)),
            pl.BlockSpec((1, D_MODEL), lambda i: (0, 0)),
        ],
        out_specs=out_specs,
        out_shape=out_shape,
        scratch_shapes=[pltpu.VMEM((TM, D_PLE), BF16)],
        compiler_params=_ARB1,
        name="ple",
    )(h, gain, w_gate, p_prompt, p_sample, w_proj, g_final)


def kernel(x_prompt, x_sample, state_conv, p_prompt, p_sample, ffn1_norm, ffn1_w_gate, ffn1_w_up, ffn1_w_down, mix_norm, a_w_in, a_ln_g, a_ln_b, a_w_s, a_b_s, a_w_out, c_w_in, c_w_conv, c_w_out, ffn2_norm, ffn2_w_gate, ffn2_w_up, ffn2_w_down, ple_norm, ple_w_gate, ple_w_proj, final_norm):
    bf = lambda w: w.astype(BF16)
    gain3 = lambda g: g.reshape(g.shape[0], 1, D_MODEL)

    h = (x_prompt.reshape(N_PROMPT, D_MODEL), x_sample.reshape(N_SAMPLE, D_MODEL))
    p_prompt = p_prompt.reshape(DEPTH, N_PROMPT, D_PLE)
    p_sample = p_sample.reshape(DEPTH, N_SAMPLE, D_PLE)

    f1 = (gain3(ffn1_norm), ffn1_w_gate, ffn1_w_up, ffn1_w_down)
    f2 = (gain3(ffn2_norm), ffn2_w_gate, ffn2_w_up, ffn2_w_down)
    mix_gain = gain3(mix_norm)
    ple_gain = gain3(ple_norm)
    a_w_in_b, a_w_out_b = a_w_in, a_w_out
    c_w_in_b = _conv_w_in_blocks(c_w_in)
    c_w_out_b = bf(c_w_out)
    ple_w_gate_b, ple_w_proj_b = ple_w_gate, ple_w_proj
    ln_g = a_ln_g.reshape(-1, N_HEAD, 1, D_HEAD)
    ln_b = a_ln_b.reshape(-1, N_HEAD, 1, D_HEAD)
    b_s = a_b_s.reshape(-1, N_HEAD, CHUNK, 1)
    g_final = final_norm.reshape(1, D_MODEL)

    conv_new_prompt, conv_new_sample, v_new = [], [], []
    for layer in range(DEPTH):
        h = _ffn(h, *f1, layer)
        jm = layer // 2
        if layer % 2 == 0:
            h, v_s = _gmlp(h, mix_gain, a_w_in_b, ln_g, ln_b, a_w_s, b_s, a_w_out_b, layer, jm)
            v_new.append(v_s.transpose(1, 0, 2).reshape(N_SAMPLE, 1, E_MIX))
        else:
            st0 = state_conv[jm, :, 0, :]
            st1 = state_conv[jm, :, 1, :]
            h, ci_s, ci_p = _conv(h, mix_gain, c_w_in_b, c_w_conv, st0, st1, c_w_out_b, layer, jm)
            ci_s = ci_s.transpose(1, 0, 2).reshape(N_SAMPLE, E_MIX)
            ci_p = ci_p.transpose(0, 2, 1, 3).reshape(N_PROMPT // SEQ, _HALO, E_MIX)
            conv_new_prompt.append(ci_p[:, _HALO - 2:, :])
            conv_new_sample.append(jnp.stack([st1, ci_s], axis=1))
        h = _ffn(h, *f2, layer)
        h = _ple(h, ple_gain, ple_w_gate_b, p_prompt, p_sample, ple_w_proj_b, g_final, layer,
                 final_norm=(layer == DEPTH - 1))

    y_prompt, y_sample = h
    return (y_prompt.reshape(N_PROMPT // SEQ, SEQ, D_MODEL), y_sample.reshape(N_SAMPLE, 1, D_MODEL),
            jnp.stack(conv_new_prompt), jnp.stack(conv_new_sample), jnp.stack(v_new))
```

```python
import functools

import jax
import jax.numpy as jnp
from jax import lax
from jax.experimental import pallas as pl
from jax.experimental.pallas import tpu as pltpu

F32 = jnp.float32
BF16 = jnp.bfloat16

D_MODEL = 2048
D_FF = 5632
E_MIX = 2048
N_HEAD = 8
D_HEAD = 256
CHUNK = 128
D_PLE = 256
SEQ = 2048
N_PROMPT = 4 * SEQ
N_SAMPLE = 128
N_TOK = N_PROMPT + N_SAMPLE
DEPTH = 2
EPS = 1e-6

TM = 640
N_TILE = N_TOK // TM
SAMPLE_ROW0 = TM - N_SAMPLE
TM_FFN = 1664
N_TILE_FFN = N_TOK // TM_FFN
SAMPLE_ROW0_FFN = TM_FFN - N_SAMPLE
TF = 256
TN = 512
V7X_VMEM_BYTES = 64 * 1024 * 1024
VMEM_LIMIT = V7X_VMEM_BYTES - 2 * 1024 * 1024

_ARB1 = pltpu.CompilerParams(dimension_semantics=("arbitrary",),
                             vmem_limit_bytes=VMEM_LIMIT)


def _rms_bf16(x, g):
    ms = jnp.mean(x * x, axis=-1, keepdims=True)
    return (x * lax.rsqrt(ms + EPS) * g).astype(BF16)


def _dot(a, b):
    return jnp.dot(a, b, preferred_element_type=F32)


def _gelu_tanh(x):
    c = 0.7978845608028654
    return x * (0.5 * (1.0 + jnp.tanh(c * (x + 0.044715 * (x * x * x)))))


TF2 = 2 * TF
N_FPAIR = D_FF // TF2
N_PAIR_FFN = N_TILE_FFN * N_FPAIR
W_SLOTS = 2


def _ffn_weight_copies(layer, pair, w_hbm, w_buf, sem):
    wg_hbm, wu_hbm, wd_hbm = w_hbm
    wg_buf, wu_buf, wd_buf = w_buf
    slot = pair % W_SLOTS
    col0 = (pair % N_FPAIR) * TF2
    cols = pl.ds(col0 if isinstance(col0, int) else pl.multiple_of(col0, TF2), TF2)
    return (
        pltpu.make_async_copy(wg_hbm.at[layer, :, cols], wg_buf.at[slot], sem.at[0, slot]),
        pltpu.make_async_copy(wu_hbm.at[layer, :, cols], wu_buf.at[slot], sem.at[1, slot]),
        pltpu.make_async_copy(wd_hbm.at[layer, cols, :], wd_buf.at[slot], sem.at[2, slot]),
    )


def _ffn_rows_in(src, tile, buf, sem, action):
    row0 = tile * TM_FFN
    rows = pl.ds(row0 if isinstance(row0, int) else pl.multiple_of(row0, 8), TM_FFN)
    if not isinstance(src, tuple):
        action(pltpu.make_async_copy(src.at[rows], buf, sem))
        return
    xp_hbm, xs_hbm = src

    @pl.when(tile < N_TILE_FFN - 1)
    def _():
        action(pltpu.make_async_copy(xp_hbm.at[rows], buf, sem))

    @pl.when(tile == N_TILE_FFN - 1)
    def _():
        tail = pl.ds((N_TILE_FFN - 1) * TM_FFN, SAMPLE_ROW0_FFN)
        action(pltpu.make_async_copy(xp_hbm.at[tail], buf.at[pl.ds(0, SAMPLE_ROW0_FFN)], sem))
        action(pltpu.make_async_copy(xs_hbm, buf.at[pl.ds(SAMPLE_ROW0_FFN, N_SAMPLE)], sem))


def _start(copy):
    copy.start()


def _wait(copy):
    copy.wait()


def _ffn_kernel(*refs, layer, split):
    if split:
        xp_hbm, xs_hbm, *refs = refs
        src = (xp_hbm, xs_hbm)
    else:
        src, *refs = refs
    (g_ref, wg_hbm, wu_hbm, wd_hbm, o_hbm,
     tbufs, xn_ref, wg_buf, wu_buf, wd_buf, w_sem, in_sem, out_sem) = refs
    tile = pl.program_id(0)
    cur = tile % 2
    nxt = 1 - cur
    tbuf = tbufs.at[cur]
    w_copies = functools.partial(_ffn_weight_copies, layer, w_hbm=(wg_hbm, wu_hbm, wd_hbm),
                                 w_buf=(wg_buf, wu_buf, wd_buf), sem=w_sem)

    def rows_out(t, slot):
        rows = pl.ds(pl.multiple_of(t * TM_FFN, 8), TM_FFN)
        return pltpu.make_async_copy(tbufs.at[slot], o_hbm.at[rows], out_sem.at[slot])

    @pl.when(tile == 0)
    def _():
        for c in w_copies(0):
            c.start()
        _ffn_rows_in(src, 0, tbufs.at[0], in_sem.at[0], _start)

    _ffn_rows_in(src, tile, tbuf, in_sem.at[cur], _wait)
    xn_ref[...] = _rms_bf16(tbuf[...], g_ref[...])

    def block(slot, half):
        cols = slice(half * TF, (half + 1) * TF)
        xn = xn_ref[...]
        gt = _dot(xn, wg_buf[slot, :, cols].astype(BF16))
        up = _dot(xn, wu_buf[slot, :, cols].astype(BF16))
        hd = (gt * jax.nn.sigmoid(gt)) * up * 0.5
        tbuf[...] += _dot(hd.astype(BF16), wd_buf[slot, cols, :].astype(BF16))

    def block_pair(jp, carry):
        @pl.when((jp == 1) & (tile >= 1))
        def _():
            rows_out(tile - 1, nxt).wait()

        @pl.when((jp == 1) & (tile + 1 < N_TILE_FFN))
        def _():
            _ffn_rows_in(src, tile + 1, tbufs.at[nxt], in_sem.at[nxt], _start)

        pair = tile * N_FPAIR + jp
        for c in w_copies(pair + 1):
            c.start()
        for c in w_copies(pair):
            c.wait()
        slot = pair % W_SLOTS
        block(slot, 0)
        block(slot, 1)
        return carry

    lax.fori_loop(0, N_FPAIR, block_pair, 0)
    rows_out(tile, cur).start()

    @pl.when(tile == N_TILE_FFN - 1)
    def _():
        rows_out(tile, cur).wait()
        for c in w_copies(N_PAIR_FFN):
            c.wait()


def _ffn(h, gain, w_gate, w_up, w_down, layer):
    split = isinstance(h, tuple)
    hbm = pl.BlockSpec(memory_space=pl.ANY)
    return pl.pallas_call(
        functools.partial(_ffn_kernel, layer=layer, split=split),
        grid=(N_TILE_FFN,),
        in_specs=[hbm] * (2 if split else 1) + [
            pl.BlockSpec((None, 1, D_MODEL), lambda i: (layer, 0, 0)), hbm, hbm, hbm],
        out_specs=hbm,
        out_shape=jax.ShapeDtypeStruct((N_TOK, D_MODEL), F32),
        scratch_shapes=[
            pltpu.VMEM((2, TM_FFN, D_MODEL), F32),
            pltpu.VMEM((TM_FFN, D_MODEL), BF16),
            pltpu.VMEM((W_SLOTS, D_MODEL, TF2), F32),
            pltpu.VMEM((W_SLOTS, D_MODEL, TF2), F32),
            pltpu.VMEM((W_SLOTS, TF2, D_MODEL), F32),
            pltpu.SemaphoreType.DMA((3, W_SLOTS)),
            pltpu.SemaphoreType.DMA((2,)),
            pltpu.SemaphoreType.DMA((2,)),
        ],
        compiler_params=_ARB1,
        name="ffn",
    )(*(h if split else (h,)), gain, w_gate, w_up, w_down)


W_STAGE_ROWS = 128


def _stage_weight(w_hbm, jm, w_buf, stage, sem):
    n_rows, width = w_buf.shape
    n_chunk = n_rows // W_STAGE_ROWS

    def chunk_rows(c):
        return pl.ds(pl.multiple_of(c * W_STAGE_ROWS, W_STAGE_ROWS), W_STAGE_ROWS)

    def chunk_copy(c, slot):
        return pltpu.make_async_copy(w_hbm.at[jm, chunk_rows(c), :],
                                     stage.at[slot, :, pl.ds(0, width)], sem.at[slot])

    chunk_copy(0, 0).start()

    def chunk(c, carry):
        slot = c % 2

        @pl.when(c + 1 < n_chunk)
        def _():
            chunk_copy(c + 1, 1 - slot).start()

        chunk_copy(c, slot).wait()
        w_buf[chunk_rows(c), :] = stage[slot, :, pl.ds(0, width)].astype(BF16)
        return carry

    lax.fori_loop(0, n_chunk, chunk, 0)


def _gmlp_kernel(h_ref, g_ref, win_hbm, wout_hbm, lng_ref, lnb_ref, ws_ref, bs_ref,
                 o_ref, vo_ref,
                 win_buf, wout_buf, xn_ref, v_ref, y_ref, stage, sem, *, jm):
    tile = pl.program_id(0)
    is_last_tile = tile == N_TILE - 1

    @pl.when(tile == 0)
    def _():
        _stage_weight(win_hbm, jm, win_buf, stage, sem)
        _stage_weight(wout_hbm, jm, wout_buf, stage, sem)

    xn_ref[...] = _rms_bf16(h_ref[...], g_ref[...])
    xn = xn_ref[...]

    def head_cols(first, head):
        return slice(first + head * D_HEAD, first + (head + 1) * D_HEAD)

    v_sum = jnp.zeros((TM, 1), F32)
    for head in range(N_HEAD):
        v = _gelu_tanh(_dot(xn, win_buf[:, head_cols(E_MIX, head)]))
        v_ref[head] = v
        v_sum = v_sum + jnp.sum(v, axis=-1, keepdims=True)
    mu = v_sum * (1.0 / E_MIX)
    var = jnp.zeros((TM, 1), F32)
    for head in range(N_HEAD):
        d = v_ref[head] - mu
        var = var + jnp.sum(d * d, axis=-1, keepdims=True)
    rstd = lax.rsqrt(var * (1.0 / E_MIX) + EPS)

    row = lax.broadcasted_iota(jnp.int32, (CHUNK, CHUNK), 0)
    col = lax.broadcasted_iota(jnp.int32, (CHUNK, CHUNK), 1)
    n_chunk = TM // CHUNK
    for head in range(N_HEAD):
        u = _gelu_tanh(_dot(xn, win_buf[:, head_cols(0, head)]))
        vn = (v_ref[head] - mu) * rstd * lng_ref[head] + lnb_ref[head]

        vo_ref[head] = vn[SAMPLE_ROW0:, :]
        w = ws_ref[head]
        b = bs_ref[head]
        w_causal = jnp.where(col <= row, w, 0.0)
        w_sample = jnp.where(col == row, w[0:1, 0:1], 0.0)
        b_sample = jnp.broadcast_to(b[0:1, :], (CHUNK, 1))
        w_last = jnp.where(is_last_tile, w_sample, w_causal).astype(BF16)
        b_last = jnp.where(is_last_tile, b_sample, b)
        w_causal = w_causal.astype(BF16)
        for c in range(n_chunk):
            rows = slice(c * CHUNK, (c + 1) * CHUNK)
            wm, bm = (w_last, b_last) if c == n_chunk - 1 else (w_causal, b)
            s = _dot(wm, vn[rows, :].astype(BF16)) + bm
            y_ref[rows, head_cols(0, head)] = (u[rows, :] * s).astype(BF16)

    o_ref[...] = h_ref[...] + _dot(y_ref[...], wout_buf[...])


def _gmlp(h, gain, w_in, ln_g, ln_b, w_s, b_s, w_out, layer, jm):
    hbm = pl.BlockSpec(memory_space=pl.ANY)
    return pl.pallas_call(
        functools.partial(_gmlp_kernel, jm=jm),
        grid=(N_TILE,),
        in_specs=[
            pl.BlockSpec((TM, D_MODEL), lambda i: (i, 0)),
            pl.BlockSpec((None, 1, D_MODEL), lambda i: (layer, 0, 0)),
            hbm, hbm,
            pl.BlockSpec((None, N_HEAD, 1, D_HEAD), lambda i: (jm, 0, 0, 0)),
            pl.BlockSpec((None, N_HEAD, 1, D_HEAD), lambda i: (jm, 0, 0, 0)),
            pl.BlockSpec((None, N_HEAD, CHUNK, CHUNK), lambda i: (jm, 0, 0, 0)),
            pl.BlockSpec((None, N_HEAD, CHUNK, 1), lambda i: (jm, 0, 0, 0)),
        ],
        out_specs=[
            pl.BlockSpec((TM, D_MODEL), lambda i: (i, 0)),
            pl.BlockSpec((N_HEAD, N_SAMPLE, D_HEAD), lambda i: (0, 0, 0)),
        ],
        out_shape=[
            jax.ShapeDtypeStruct((N_TOK, D_MODEL), F32),
            jax.ShapeDtypeStruct((N_HEAD, N_SAMPLE, D_HEAD), F32),
        ],
        scratch_shapes=[
            pltpu.VMEM((D_MODEL, 2 * E_MIX), BF16),
            pltpu.VMEM((E_MIX, D_MODEL), BF16),
            pltpu.VMEM((TM, D_MODEL), BF16),
            pltpu.VMEM((N_HEAD, TM, D_HEAD), F32),
            pltpu.VMEM((TM, E_MIX), BF16),
            pltpu.VMEM((2, W_STAGE_ROWS, 2 * E_MIX), F32),
            pltpu.SemaphoreType.DMA((2,)),
        ],
        compiler_params=_ARB1,
        name="gmlp",
    )(h, gain, w_in, w_out, ln_g, ln_b, w_s, b_s)


_HALO = 8
CB = 256
N_CB = E_MIX // CB
CONV_PAIRS = N_CB // 2
CONV_SLOTS = 4
assert CONV_PAIRS % 2 == 0

_SEQ_TAIL = [divmod((s + 1) * SEQ - _HALO, TM) for s in range(N_PROMPT // SEQ)]


def _conv_kernel(h_ref, g_ref, win_hbm, wout_hbm, wk_ref, st0_ref, st1_ref,
                 o_ref, cs_ref, cp_ref,
                 win_buf, wout_buf, xn_ref, buf_ref, carry_ref, tails_ref, y_ref,
                 sem_in, sem_out, *, jm):
    tile = pl.program_id(0)
    is_last_tile = tile == N_TILE - 1

    def win_copy(pair):
        half = pair % 2
        return pltpu.make_async_copy(win_hbm.at[jm, pl.ds(2 * pair, 2)],
                                     win_buf.at[pl.ds(2 * half, 2)], sem_in.at[half])

    wout_copy = pltpu.make_async_copy(wout_hbm.at[jm], wout_buf, sem_out.at[0])

    @pl.when(tile == 0)
    def _():
        wout_copy.start()
        win_copy(0).start()
        carry_ref[...] = jnp.zeros_like(carry_ref)

    xn_ref[...] = _rms_bf16(h_ref[...], g_ref[...])
    xn = xn_ref[...]
    pos = (lax.broadcasted_iota(jnp.int32, (TM, 1), 0) + tile * TM) & (SEQ - 1)

    def conv_block(block, slot):
        cols = slice(block * CB, (block + 1) * CB)
        bg = _dot(xn, win_buf[slot, 0])
        ci = _dot(xn, win_buf[slot, 1]) * _dot(xn, win_buf[slot, 2])

        for s, (_, row0) in enumerate(_SEQ_TAIL):
            tails_ref[s, block] = ci[row0:row0 + _HALO, :]
        ci_s = ci[SAMPLE_ROW0:, :]
        cs_ref[block] = ci_s

        buf = buf_ref.at[block % 2]
        buf[0:_HALO, :] = carry_ref[block]
        buf[_HALO:, :] = ci
        carry_ref[block] = ci[TM - _HALO:, :]
        prev1 = jnp.where(pos >= 1, buf[_HALO - 1:_HALO - 1 + TM, :], 0.0)
        prev2 = jnp.where(pos >= 2, buf[_HALO - 2:_HALO - 2 + TM, :], 0.0)
        w0 = wk_ref[0:1, cols]
        w1 = wk_ref[1:2, cols]
        w2 = wk_ref[2:3, cols]
        co = w0 * prev2 + w1 * prev1 + w2 * ci
        co_s = w0 * st0_ref[:, cols] + w1 * st1_ref[:, cols] + w2 * ci_s
        co_tail = jnp.where(is_last_tile, co_s, co[SAMPLE_ROW0:, :])
        y_ref[0:SAMPLE_ROW0, cols] = (bg[0:SAMPLE_ROW0, :] * co[0:SAMPLE_ROW0, :]).astype(BF16)
        y_ref[SAMPLE_ROW0:, cols] = (bg[SAMPLE_ROW0:, :] * co_tail).astype(BF16)

    for pair in range(CONV_PAIRS):
        blocks = (2 * pair, 2 * pair + 1)
        slots = (0, 1) if pair % 2 == 0 else (2, 3)
        if pair == 1:
            @pl.when(tile == 0)
            def _():
                wout_copy.wait()
        win_copy((pair + 1) % CONV_PAIRS).start()
        win_copy(pair).wait()
        for b, slot in zip(blocks, slots):
            conv_block(b, slot)

    o_ref[...] = h_ref[...] + _dot(y_ref[...], wout_buf[...].astype(BF16))

    for s, (seq_tile, _) in enumerate(_SEQ_TAIL):
        @pl.when(tile == seq_tile)
        def _():
            cp_ref[s] = tails_ref[s]

    @pl.when(is_last_tile)
    def _():
        win_copy(0).wait()


def _conv_w_in_blocks_kernel(wb_ref, wc_ref, wx_ref, o_ref):
    for part, w_ref in enumerate((wb_ref, wc_ref, wx_ref)):
        o_ref[part] = w_ref[...].astype(BF16)


def _conv_w_in_blocks(w_in):
    n_b = w_in.shape[0]

    def part_spec(part):
        return pl.BlockSpec((None, D_MODEL, CB), lambda m, b: (m, 0, part * N_CB + b))

    return pl.pallas_call(
        _conv_w_in_blocks_kernel,
        grid=(n_b, N_CB),
        in_specs=[part_spec(0), part_spec(1), part_spec(2)],
        out_specs=pl.BlockSpec((None, None, 3, D_MODEL, CB), lambda m, b: (m, b, 0, 0, 0)),
        out_shape=jax.ShapeDtypeStruct((n_b, N_CB, 3, D_MODEL, CB), BF16),
        compiler_params=pltpu.CompilerParams(dimension_semantics=("arbitrary", "arbitrary")),
        name="conv_w_in_blocks",
    )(w_in, w_in, w_in)


def _conv(h, gain, w_in_blocked, w_conv, st0, st1, w_out, layer, jm):
    n_seq = N_PROMPT // SEQ
    hbm = pl.BlockSpec(memory_space=pl.ANY)
    return pl.pallas_call(
        functools.partial(_conv_kernel, jm=jm),
        grid=(N_TILE,),
        in_specs=[
            pl.BlockSpec((TM, D_MODEL), lambda i: (i, 0)),
            pl.BlockSpec((None, 1, D_MODEL), lambda i: (layer, 0, 0)),
            hbm, hbm,
            pl.BlockSpec((None, 3, E_MIX), lambda i: (jm, 0, 0)),
            pl.BlockSpec((N_SAMPLE, E_MIX), lambda i: (0, 0)),
            pl.BlockSpec((N_SAMPLE, E_MIX), lambda i: (0, 0)),
        ],
        out_specs=[
            pl.BlockSpec((TM, D_MODEL), lambda i: (i, 0)),
            pl.BlockSpec((N_CB, N_SAMPLE, CB), lambda i: (0, 0, 0)),
            pl.BlockSpec((n_seq, N_CB, _HALO, CB), lambda i: (0, 0, 0, 0)),
        ],
        out_shape=[
            jax.ShapeDtypeStruct((N_TOK, D_MODEL), F32),
            jax.ShapeDtypeStruct((N_CB, N_SAMPLE, CB), F32),
            jax.ShapeDtypeStruct((n_seq, N_CB, _HALO, CB), F32),
        ],
        scratch_shapes=[
            pltpu.VMEM((CONV_SLOTS, 3, D_MODEL, CB), BF16),
            pltpu.VMEM((E_MIX, D_MODEL), F32),
            pltpu.VMEM((TM, D_MODEL), BF16),
            pltpu.VMEM((2, TM + _HALO, CB), F32),
            pltpu.VMEM((N_CB, _HALO, CB), F32),
            pltpu.VMEM((n_seq, N_CB, _HALO, CB), F32),
            pltpu.VMEM((TM, E_MIX), BF16),
            pltpu.SemaphoreType.DMA((CONV_SLOTS,)),
            pltpu.SemaphoreType.DMA((1,)),
        ],
        compiler_params=_ARB1,
        name="sconv",
    )(h, gain, w_in_blocked, w_out, w_conv, st0, st1)


def _ple_kernel(h_ref, g_ref, wg_ref, pp_ref, ps_ref, wp_ref, gf_ref, o_ref, *rest, final_norm):
    if final_norm:
        os_ref, pb_ref = rest
    else:
        (pb_ref,) = rest
    i = pl.program_id(0)

    @pl.when(i < N_TILE - 1)
    def _():
        pb_ref[...] = pp_ref[...].astype(BF16)

    @pl.when(i == N_TILE - 1)
    def _():
        pb_ref[0:SAMPLE_ROW0, :] = pp_ref[0:SAMPLE_ROW0, :].astype(BF16)
        pb_ref[SAMPLE_ROW0:, :] = ps_ref[...].astype(BF16)

    xn = _rms_bf16(h_ref[...], g_ref[...])
    pb = pb_ref[...]
    for c in range(D_MODEL // TN):
        cols = slice(c * TN, (c + 1) * TN)
        gate = jax.nn.sigmoid(_dot(xn, wg_ref[:, cols].astype(BF16)))
        o_ref[:, cols] = h_ref[:, cols] + gate * _dot(pb, wp_ref[:, cols].astype(BF16))
    if final_norm:
        hn = o_ref[...]
        ms = jnp.mean(hn * hn, axis=-1, keepdims=True)
        o_ref[...] = hn * lax.rsqrt(ms + EPS) * gf_ref[...]

        @pl.when(i == N_TILE - 1)
        def _():
            os_ref[...] = o_ref[SAMPLE_ROW0:, :]


def _ple(h, gain, w_gate, p_prompt, p_sample, w_proj, g_final, layer, final_norm):
    row_block = pl.BlockSpec((TM, D_MODEL), lambda i: (i, 0))
    if final_norm:
        out_specs = [row_block, pl.BlockSpec((N_SAMPLE, D_MODEL), lambda i: (0, 0))]
        out_shape = [jax.ShapeDtypeStruct((N_PROMPT, D_MODEL), F32),
                     jax.ShapeDtypeStruct((N_SAMPLE, D_MODEL), F32)]
    else:
        out_specs = row_block
        out_shape = jax.ShapeDtypeStruct((N_TOK, D_MODEL), F32)
    return pl.pallas_call(
        functools.partial(_ple_kernel, final_norm=final_norm),
        grid=(N_TILE,),
        in_specs=[
            row_block,
            pl.BlockSpec((None, 1, D_MODEL), lambda i: (layer, 0, 0)),
            pl.BlockSpec((None, D_MODEL, D_MODEL), lambda i: (layer, 0, 0),
                         pipeline_mode=pl.Buffered(1)),
            pl.BlockSpec((None, TM, D_PLE), lambda i: (layer, i, 0)),
            pl.BlockSpec((None, N_SAMPLE, D_PLE), lambda i: (layer, 0, 0)),
            pl.BlockSpec((None, D_PLE, D_MODEL), lambda i: (layer, 0, 0),
                         pipeline_mode=pl.Buffered(1)),
            pl.BlockSpec((1, D_MODEL), lambda i: (0, 0)),
        ],
        out_specs=out_specs,
        out_shape=out_shape,
        scratch_shapes=[pltpu.VMEM((TM, D_PLE), BF16)],
        compiler_params=_ARB1,
        name="ple",
    )(h, gain, w_gate, p_prompt, p_sample, w_proj, g_final)


def kernel(x_prompt, x_sample, state_conv, p_prompt, p_sample, ffn1_norm, ffn1_w_gate, ffn1_w_up, ffn1_w_down, mix_norm, a_w_in, a_ln_g, a_ln_b, a_w_s, a_b_s, a_w_out, c_w_in, c_w_conv, c_w_out, ffn2_norm, ffn2_w_gate, ffn2_w_up, ffn2_w_down, ple_norm, ple_w_gate, ple_w_proj, final_norm):
    gain3 = lambda g: g.reshape(g.shape[0], 1, D_MODEL)

    h = (x_prompt.reshape(N_PROMPT, D_MODEL), x_sample.reshape(N_SAMPLE, D_MODEL))
    p_prompt = p_prompt.reshape(DEPTH, N_PROMPT, D_PLE)
    p_sample = p_sample.reshape(DEPTH, N_SAMPLE, D_PLE)

    f1 = (gain3(ffn1_norm), ffn1_w_gate, ffn1_w_up, ffn1_w_down)
    f2 = (gain3(ffn2_norm), ffn2_w_gate, ffn2_w_up, ffn2_w_down)
    mix_gain = gain3(mix_norm)
    ple_gain = gain3(ple_norm)
    a_w_in_b, a_w_out_b = a_w_in, a_w_out
    c_w_in_b = _conv_w_in_blocks(c_w_in)
    c_w_out_b = c_w_out
    ple_w_gate_b, ple_w_proj_b = ple_w_gate, ple_w_proj
    ln_g = a_ln_g.reshape(-1, N_HEAD, 1, D_HEAD)
    ln_b = a_ln_b.reshape(-1, N_HEAD, 1, D_HEAD)
    b_s = a_b_s.reshape(-1, N_HEAD, CHUNK, 1)
    g_final = final_norm.reshape(1, D_MODEL)

    conv_new_prompt, conv_new_sample, v_new = [], [], []
    for layer in range(DEPTH):
        h = _ffn(h, *f1, layer)
        jm = layer // 2
        if layer % 2 == 0:
            h, v_s = _gmlp(h, mix_gain, a_w_in_b, ln_g, ln_b, a_w_s, b_s, a_w_out_b, layer, jm)
            v_new.append(v_s.transpose(1, 0, 2).reshape(N_SAMPLE, 1, E_MIX))
        else:
            st0 = state_conv[jm, :, 0, :]
            st1 = state_conv[jm, :, 1, :]
            h, ci_s, ci_p = _conv(h, mix_gain, c_w_in_b, c_w_conv, st0, st1, c_w_out_b, layer, jm)
            ci_s = ci_s.transpose(1, 0, 2).reshape(N_SAMPLE, E_MIX)
            ci_p = ci_p.transpose(0, 2, 1, 3).reshape(N_PROMPT // SEQ, _HALO, E_MIX)
            conv_new_prompt.append(ci_p[:, _HALO - 2:, :])
            conv_new_sample.append(jnp.stack([st1, ci_s], axis=1))
        h = _ffn(h, *f2, layer)
        h = _ple(h, ple_gain, ple_w_gate_b, p_prompt, p_sample, ple_w_proj_b, g_final, layer,
                 final_norm=(layer == DEPTH - 1))

    y_prompt, y_sample = h
    return (y_prompt.reshape(N_PROMPT // SEQ, SEQ, D_MODEL), y_sample.reshape(N_SAMPLE, 1, D_MODEL),
            jnp.stack(conv_new_prompt), jnp.stack(conv_new_sample), jnp.stack(v_new))
```

```python
import functools

import jax
import jax.numpy as jnp
from jax import lax
from jax.experimental import pallas as pl
from jax.experimental.pallas import tpu as pltpu

F32 = jnp.float32
BF16 = jnp.bfloat16

D_MODEL = 2048
D_FF = 5632
E_MIX = 2048
N_HEAD = 8
D_HEAD = 256
CHUNK = 128
D_PLE = 256
SEQ = 2048
N_PROMPT = 4 * SEQ
N_SAMPLE = 128
N_TOK = N_PROMPT + N_SAMPLE
DEPTH = 2
EPS = 1e-6

TM = 640
N_TILE = N_TOK // TM
SAMPLE_ROW0 = TM - N_SAMPLE
TM_FFN = 1664
N_TILE_FFN = N_TOK // TM_FFN
SAMPLE_ROW0_FFN = TM_FFN - N_SAMPLE
TF = 256
TN = 512
V7X_VMEM_BYTES = 64 * 1024 * 1024
VMEM_LIMIT = V7X_VMEM_BYTES - 2 * 1024 * 1024

_ARB1 = pltpu.CompilerParams(dimension_semantics=("arbitrary",),
                             vmem_limit_bytes=VMEM_LIMIT)


def _rms_bf16(x, g):
    ms = jnp.mean(x * x, axis=-1, keepdims=True)
    return (x * lax.rsqrt(ms + EPS) * g).astype(BF16)


def _dot(a, b):
    return jnp.dot(a, b, preferred_element_type=F32)


def _gelu_tanh(x):
    c = 0.7978845608028654
    return x * (0.5 * (1.0 + jnp.tanh(c * (x + 0.044715 * (x * x * x)))))


TF2 = 2 * TF
N_FPAIR = D_FF // TF2
N_PAIR_FFN = N_TILE_FFN * N_FPAIR
W_SLOTS = 2


def _ffn_weight_copies(layer, pair, w_hbm, w_buf, sem):
    wg_hbm, wu_hbm, wd_hbm = w_hbm
    wg_buf, wu_buf, wd_buf = w_buf
    slot = pair % W_SLOTS
    col0 = (pair % N_FPAIR) * TF2
    cols = pl.ds(col0 if isinstance(col0, int) else pl.multiple_of(col0, TF2), TF2)
    return (
        pltpu.make_async_copy(wg_hbm.at[layer, :, cols], wg_buf.at[slot], sem.at[0, slot]),
        pltpu.make_async_copy(wu_hbm.at[layer, :, cols], wu_buf.at[slot], sem.at[1, slot]),
        pltpu.make_async_copy(wd_hbm.at[layer, cols, :], wd_buf.at[slot], sem.at[2, slot]),
    )


def _ffn_rows_in(src, tile, buf, sem, action):
    row0 = tile * TM_FFN
    rows = pl.ds(row0 if isinstance(row0, int) else pl.multiple_of(row0, 8), TM_FFN)
    if not isinstance(src, tuple):
        action(pltpu.make_async_copy(src.at[rows], buf, sem))
        return
    xp_hbm, xs_hbm = src

    @pl.when(tile < N_TILE_FFN - 1)
    def _():
        action(pltpu.make_async_copy(xp_hbm.at[rows], buf, sem))

    @pl.when(tile == N_TILE_FFN - 1)
    def _():
        tail = pl.ds((N_TILE_FFN - 1) * TM_FFN, SAMPLE_ROW0_FFN)
        action(pltpu.make_async_copy(xp_hbm.at[tail], buf.at[pl.ds(0, SAMPLE_ROW0_FFN)], sem))
        action(pltpu.make_async_copy(xs_hbm, buf.at[pl.ds(SAMPLE_ROW0_FFN, N_SAMPLE)], sem))


def _start(copy):
    copy.start()


def _wait(copy):
    copy.wait()


def _ffn_kernel(*refs, layer, split):
    if split:
        xp_hbm, xs_hbm, *refs = refs
        src = (xp_hbm, xs_hbm)
    else:
        src, *refs = refs
    (g_ref, wg_hbm, wu_hbm, wd_hbm, o_hbm,
     tbufs, xn_ref, wg_buf, wu_buf, wd_buf, w_sem, in_sem, out_sem) = refs
    tile = pl.program_id(0)
    cur = tile % 2
    nxt = 1 - cur
    tbuf = tbufs.at[cur]
    w_copies = functools.partial(_ffn_weight_copies, layer, w_hbm=(wg_hbm, wu_hbm, wd_hbm),
                                 w_buf=(wg_buf, wu_buf, wd_buf), sem=w_sem)

    def rows_out(t, slot):
        rows = pl.ds(pl.multiple_of(t * TM_FFN, 8), TM_FFN)
        return pltpu.make_async_copy(tbufs.at[slot], o_hbm.at[rows], out_sem.at[slot])

    @pl.when(tile == 0)
    def _():
        for c in w_copies(0):
            c.start()
        _ffn_rows_in(src, 0, tbufs.at[0], in_sem.at[0], _start)

    _ffn_rows_in(src, tile, tbuf, in_sem.at[cur], _wait)
    xn_ref[...] = _rms_bf16(tbuf[...], g_ref[...])

    def block(slot, half):
        cols = slice(half * TF, (half + 1) * TF)
        xn = xn_ref[...]
        gt = _dot(xn, wg_buf[slot, :, cols].astype(BF16))
        up = _dot(xn, wu_buf[slot, :, cols].astype(BF16))
        hd = (gt * jax.nn.sigmoid(gt)) * up * 0.5
        tbuf[...] += _dot(hd.astype(BF16), wd_buf[slot, cols, :].astype(BF16))

    def block_pair(jp, carry):
        @pl.when((jp == 1) & (tile >= 1))
        def _():
            rows_out(tile - 1, nxt).wait()

        @pl.when((jp == 1) & (tile + 1 < N_TILE_FFN))
        def _():
            _ffn_rows_in(src, tile + 1, tbufs.at[nxt], in_sem.at[nxt], _start)

        pair = tile * N_FPAIR + jp
        for c in w_copies(pair + 1):
            c.start()
        for c in w_copies(pair):
            c.wait()
        slot = pair % W_SLOTS
        block(slot, 0)
        block(slot, 1)
        return carry

    lax.fori_loop(0, N_FPAIR, block_pair, 0)
    rows_out(tile, cur).start()

    @pl.when(tile == N_TILE_FFN - 1)
    def _():
        rows_out(tile, cur).wait()
        for c in w_copies(N_PAIR_FFN):
            c.wait()


def _ffn(h, gain, w_gate, w_up, w_down, layer):
    split = isinstance(h, tuple)
    hbm = pl.BlockSpec(memory_space=pl.ANY)
    return pl.pallas_call(
        functools.partial(_ffn_kernel, layer=layer, split=split),
        grid=(N_TILE_FFN,),
        in_specs=[hbm] * (2 if split else 1) + [
            pl.BlockSpec((None, 1, D_MODEL), lambda i: (layer, 0, 0)), hbm, hbm, hbm],
        out_specs=hbm,
        out_shape=jax.ShapeDtypeStruct((N_TOK, D_MODEL), F32),
        scratch_shapes=[
            pltpu.VMEM((2, TM_FFN, D_MODEL), F32),
            pltpu.VMEM((TM_FFN, D_MODEL), BF16),
            pltpu.VMEM((W_SLOTS, D_MODEL, TF2), F32),
            pltpu.VMEM((W_SLOTS, D_MODEL, TF2), F32),
            pltpu.VMEM((W_SLOTS, TF2, D_MODEL), F32),
            pltpu.SemaphoreType.DMA((3, W_SLOTS)),
            pltpu.SemaphoreType.DMA((2,)),
            pltpu.SemaphoreType.DMA((2,)),
        ],
        compiler_params=_ARB1,
        name="ffn",
    )(*(h if split else (h,)), gain, w_gate, w_up, w_down)


W_STAGE_ROWS = 128


def _stage_weight(w_hbm, jm, w_buf, stage, sem):
    n_rows, width = w_buf.shape
    n_chunk = n_rows // W_STAGE_ROWS

    def chunk_rows(c):
        return pl.ds(pl.multiple_of(c * W_STAGE_ROWS, W_STAGE_ROWS), W_STAGE_ROWS)

    def chunk_copy(c, slot):
        return pltpu.make_async_copy(w_hbm.at[jm, chunk_rows(c), :],
                                     stage.at[slot, :, pl.ds(0, width)], sem.at[slot])

    chunk_copy(0, 0).start()

    def chunk(c, carry):
        slot = c % 2

        @pl.when(c + 1 < n_chunk)
        def _():
            chunk_copy(c + 1, 1 - slot).start()

        chunk_copy(c, slot).wait()
        w_buf[chunk_rows(c), :] = stage[slot, :, pl.ds(0, width)].astype(BF16)
        return carry

    lax.fori_loop(0, n_chunk, chunk, 0)


def _gmlp_kernel(h_ref, g_ref, win_hbm, wout_hbm, lng_ref, lnb_ref, ws_ref, bs_ref,
                 o_ref, vo_ref,
                 win_buf, wout_buf, xn_ref, v_ref, y_ref, stage, sem, *, jm):
    tile = pl.program_id(0)
    is_last_tile = tile == N_TILE - 1

    @pl.when(tile == 0)
    def _():
        _stage_weight(win_hbm, jm, win_buf, stage, sem)
        _stage_weight(wout_hbm, jm, wout_buf, stage, sem)

    xn_ref[...] = _rms_bf16(h_ref[...], g_ref[...])
    xn = xn_ref[...]

    def head_cols(first, head):
        return slice(first + head * D_HEAD, first + (head + 1) * D_HEAD)

    v_sum = jnp.zeros((TM, 1), F32)
    for head in range(N_HEAD):
        v = _gelu_tanh(_dot(xn, win_buf[:, head_cols(E_MIX, head)]))
        v_ref[head] = v
        v_sum = v_sum + jnp.sum(v, axis=-1, keepdims=True)
    mu = v_sum * (1.0 / E_MIX)
    var = jnp.zeros((TM, 1), F32)
    for head in range(N_HEAD):
        d = v_ref[head] - mu
        var = var + jnp.sum(d * d, axis=-1, keepdims=True)
    rstd = lax.rsqrt(var * (1.0 / E_MIX) + EPS)

    row = lax.broadcasted_iota(jnp.int32, (CHUNK, CHUNK), 0)
    col = lax.broadcasted_iota(jnp.int32, (CHUNK, CHUNK), 1)
    n_chunk = TM // CHUNK
    for head in range(N_HEAD):
        u = _gelu_tanh(_dot(xn, win_buf[:, head_cols(0, head)]))
        vn = (v_ref[head] - mu) * rstd * lng_ref[head] + lnb_ref[head]

        vo_ref[:, head_cols(0, head)] = vn[SAMPLE_ROW0:, :]
        w = ws_ref[head]
        b = bs_ref[head]
        w_causal = jnp.where(col <= row, w, 0.0)
        w_sample = jnp.where(col == row, w[0:1, 0:1], 0.0)
        b_sample = jnp.broadcast_to(b[0:1, :], (CHUNK, 1))
        w_last = jnp.where(is_last_tile, w_sample, w_causal).astype(BF16)
        b_last = jnp.where(is_last_tile, b_sample, b)
        w_causal = w_causal.astype(BF16)
        for c in range(n_chunk):
            rows = slice(c * CHUNK, (c + 1) * CHUNK)
            wm, bm = (w_last, b_last) if c == n_chunk - 1 else (w_causal, b)
            s = _dot(wm, vn[rows, :].astype(BF16)) + bm
            y_ref[rows, head_cols(0, head)] = (u[rows, :] * s).astype(BF16)

    o_ref[...] = h_ref[...] + _dot(y_ref[...], wout_buf[...])


def _gmlp(h, gain, w_in, ln_g, ln_b, w_s, b_s, w_out, layer, jm):
    hbm = pl.BlockSpec(memory_space=pl.ANY)
    return pl.pallas_call(
        functools.partial(_gmlp_kernel, jm=jm),
        grid=(N_TILE,),
        in_specs=[
            pl.BlockSpec((TM, D_MODEL), lambda i: (i, 0)),
            pl.BlockSpec((None, 1, D_MODEL), lambda i: (layer, 0, 0)),
            hbm, hbm,
            pl.BlockSpec((None, N_HEAD, 1, D_HEAD), lambda i: (jm, 0, 0, 0)),
            pl.BlockSpec((None, N_HEAD, 1, D_HEAD), lambda i: (jm, 0, 0, 0)),
            pl.BlockSpec((None, N_HEAD, CHUNK, CHUNK), lambda i: (jm, 0, 0, 0)),
            pl.BlockSpec((None, N_HEAD, CHUNK, 1), lambda i: (jm, 0, 0, 0)),
        ],
        out_specs=[
            pl.BlockSpec((TM, D_MODEL), lambda i: (i, 0)),
            pl.BlockSpec((N_SAMPLE, E_MIX), lambda i: (0, 0)),
        ],
        out_shape=[
            jax.ShapeDtypeStruct((N_TOK, D_MODEL), F32),
            jax.ShapeDtypeStruct((N_SAMPLE, E_MIX), F32),
        ],
        scratch_shapes=[
            pltpu.VMEM((D_MODEL, 2 * E_MIX), BF16),
            pltpu.VMEM((E_MIX, D_MODEL), BF16),
            pltpu.VMEM((TM, D_MODEL), BF16),
            pltpu.VMEM((N_HEAD, TM, D_HEAD), F32),
            pltpu.VMEM((TM, E_MIX), BF16),
            pltpu.VMEM((2, W_STAGE_ROWS, 2 * E_MIX), F32),
            pltpu.SemaphoreType.DMA((2,)),
        ],
        compiler_params=_ARB1,
        name="gmlp",
    )(h, gain, w_in, w_out, ln_g, ln_b, w_s, b_s)


_HALO = 8
CB = 256
N_CB = E_MIX // CB
CONV_PAIRS = N_CB // 2
CONV_SLOTS = 4
assert CONV_PAIRS % 2 == 0

_SEQ_TAIL = [divmod((s + 1) * SEQ - _HALO, TM) for s in range(N_PROMPT // SEQ)]


def _conv_kernel(h_ref, g_ref, win_hbm, wout_hbm, wk_ref, st_ref,
                 o_ref, cs_ref, cp_ref,
                 win_buf, wout_buf, xn_ref, buf_ref, carry_ref, tails_ref, y_ref,
                 sem_in, sem_out, *, jm):
    tile = pl.program_id(0)
    is_last_tile = tile == N_TILE - 1

    def win_copy(pair):
        half = pair % 2
        return pltpu.make_async_copy(win_hbm.at[jm, pl.ds(2 * pair, 2)],
                                     win_buf.at[pl.ds(2 * half, 2)], sem_in.at[half])

    wout_copy = pltpu.make_async_copy(wout_hbm.at[jm], wout_buf, sem_out.at[0])

    @pl.when(tile == 0)
    def _():
        wout_copy.start()
        win_copy(0).start()
        carry_ref[...] = jnp.zeros_like(carry_ref)

    xn_ref[...] = _rms_bf16(h_ref[...], g_ref[...])
    xn = xn_ref[...]
    pos = (lax.broadcasted_iota(jnp.int32, (TM, 1), 0) + tile * TM) & (SEQ - 1)

    def conv_block(block, slot):
        cols = slice(block * CB, (block + 1) * CB)
        bg = _dot(xn, win_buf[slot, 0])
        ci = _dot(xn, win_buf[slot, 1]) * _dot(xn, win_buf[slot, 2])

        for s, (_, row0) in enumerate(_SEQ_TAIL):
            tails_ref[s, :, cols] = ci[row0:row0 + _HALO, :]
        ci_s = ci[SAMPLE_ROW0:, :]
        cs_ref[:, cols] = ci_s

        buf = buf_ref.at[block % 2]
        buf[0:_HALO, :] = carry_ref[block]
        buf[_HALO:, :] = ci
        carry_ref[block] = ci[TM - _HALO:, :]
        prev1 = jnp.where(pos >= 1, buf[_HALO - 1:_HALO - 1 + TM, :], 0.0)
        prev2 = jnp.where(pos >= 2, buf[_HALO - 2:_HALO - 2 + TM, :], 0.0)
        w0 = wk_ref[0:1, cols]
        w1 = wk_ref[1:2, cols]
        w2 = wk_ref[2:3, cols]
        co = w0 * prev2 + w1 * prev1 + w2 * ci
        st1_cols = slice(E_MIX + block * CB, E_MIX + (block + 1) * CB)
        co_s = w0 * st_ref[:, cols] + w1 * st_ref[:, st1_cols] + w2 * ci_s
        co_tail = jnp.where(is_last_tile, co_s, co[SAMPLE_ROW0:, :])
        y_ref[0:SAMPLE_ROW0, cols] = (bg[0:SAMPLE_ROW0, :] * co[0:SAMPLE_ROW0, :]).astype(BF16)
        y_ref[SAMPLE_ROW0:, cols] = (bg[SAMPLE_ROW0:, :] * co_tail).astype(BF16)

    for pair in range(CONV_PAIRS):
        blocks = (2 * pair, 2 * pair + 1)
        slots = (0, 1) if pair % 2 == 0 else (2, 3)
        if pair == 1:
            @pl.when(tile == 0)
            def _():
                wout_copy.wait()
        win_copy((pair + 1) % CONV_PAIRS).start()
        win_copy(pair).wait()
        for b, slot in zip(blocks, slots):
            conv_block(b, slot)

    o_ref[...] = h_ref[...] + _dot(y_ref[...], wout_buf[...].astype(BF16))

    for s, (seq_tile, _) in enumerate(_SEQ_TAIL):
        @pl.when(tile == seq_tile)
        def _():
            cp_ref[s] = tails_ref[s]

    @pl.when(is_last_tile)
    def _():
        win_copy(0).wait()


def _conv_w_in_blocks_kernel(wb_ref, wc_ref, wx_ref, o_ref):
    for part, w_ref in enumerate((wb_ref, wc_ref, wx_ref)):
        o_ref[part] = w_ref[...].astype(BF16)


def _conv_w_in_blocks(w_in):
    n_b = w_in.shape[0]

    def part_spec(part):
        return pl.BlockSpec((None, D_MODEL, CB), lambda m, b: (m, 0, part * N_CB + b))

    return pl.pallas_call(
        _conv_w_in_blocks_kernel,
        grid=(n_b, N_CB),
        in_specs=[part_spec(0), part_spec(1), part_spec(2)],
        out_specs=pl.BlockSpec((None, None, 3, D_MODEL, CB), lambda m, b: (m, b, 0, 0, 0)),
        out_shape=jax.ShapeDtypeStruct((n_b, N_CB, 3, D_MODEL, CB), BF16),
        compiler_params=pltpu.CompilerParams(dimension_semantics=("arbitrary", "arbitrary")),
        name="conv_w_in_blocks",
    )(w_in, w_in, w_in)


def _conv(h, gain, w_in_blocked, w_conv, state, w_out, layer, jm):
    n_seq = N_PROMPT // SEQ
    hbm = pl.BlockSpec(memory_space=pl.ANY)
    return pl.pallas_call(
        functools.partial(_conv_kernel, jm=jm),
        grid=(N_TILE,),
        in_specs=[
            pl.BlockSpec((TM, D_MODEL), lambda i: (i, 0)),
            pl.BlockSpec((None, 1, D_MODEL), lambda i: (layer, 0, 0)),
            hbm, hbm,
            pl.BlockSpec((None, 3, E_MIX), lambda i: (jm, 0, 0)),
            pl.BlockSpec((None, N_SAMPLE, 2 * E_MIX), lambda i: (jm, 0, 0)),
        ],
        out_specs=[
            pl.BlockSpec((TM, D_MODEL), lambda i: (i, 0)),
            pl.BlockSpec((N_SAMPLE, E_MIX), lambda i: (0, 0)),
            pl.BlockSpec((n_seq, _HALO, E_MIX), lambda i: (0, 0, 0)),
        ],
        out_shape=[
            jax.ShapeDtypeStruct((N_TOK, D_MODEL), F32),
            jax.ShapeDtypeStruct((N_SAMPLE, E_MIX), F32),
            jax.ShapeDtypeStruct((n_seq, _HALO, E_MIX), F32),
        ],
        scratch_shapes=[
            pltpu.VMEM((CONV_SLOTS, 3, D_MODEL, CB), BF16),
            pltpu.VMEM((E_MIX, D_MODEL), F32),
            pltpu.VMEM((TM, D_MODEL), BF16),
            pltpu.VMEM((2, TM + _HALO, CB), F32),
            pltpu.VMEM((N_CB, _HALO, CB), F32),
            pltpu.VMEM((n_seq, _HALO, E_MIX), F32),
            pltpu.VMEM((TM, E_MIX), BF16),
            pltpu.SemaphoreType.DMA((CONV_SLOTS,)),
            pltpu.SemaphoreType.DMA((1,)),
        ],
        compiler_params=_ARB1,
        name="sconv",
    )(h, gain, w_in_blocked, w_out, w_conv, state)


def _ple_kernel(h_ref, g_ref, wg_ref, pp_ref, ps_ref, wp_ref, gf_ref, o_ref, *rest, final_norm):
    if final_norm:
        os_ref, pb_ref = rest
    else:
        (pb_ref,) = rest
    i = pl.program_id(0)

    @pl.when(i < N_TILE - 1)
    def _():
        pb_ref[...] = pp_ref[...].astype(BF16)

    @pl.when(i == N_TILE - 1)
    def _():
        pb_ref[0:SAMPLE_ROW0, :] = pp_ref[0:SAMPLE_ROW0, :].astype(BF16)
        pb_ref[SAMPLE_ROW0:, :] = ps_ref[...].astype(BF16)

    xn = _rms_bf16(h_ref[...], g_ref[...])
    pb = pb_ref[...]
    for c in range(D_MODEL // TN):
        cols = slice(c * TN, (c + 1) * TN)
        gate = jax.nn.sigmoid(_dot(xn, wg_ref[:, cols].astype(BF16)))
        o_ref[:, cols] = h_ref[:, cols] + gate * _dot(pb, wp_ref[:, cols].astype(BF16))
    if final_norm:
        hn = o_ref[...]
        ms = jnp.mean(hn * hn, axis=-1, keepdims=True)
        o_ref[...] = hn * lax.rsqrt(ms + EPS) * gf_ref[...]

        @pl.when(i == N_TILE - 1)
        def _():
            os_ref[...] = o_ref[SAMPLE_ROW0:, :]


def _ple(h, gain, w_gate, p_prompt, p_sample, w_proj, g_final, layer, final_norm):
    row_block = pl.BlockSpec((TM, D_MODEL), lambda i: (i, 0))
    if final_norm:
        out_specs = [row_block, pl.BlockSpec((N_SAMPLE, D_MODEL), lambda i: (0, 0))]
        out_shape = [jax.ShapeDtypeStruct((N_PROMPT, D_MODEL), F32),
                     jax.ShapeDtypeStruct((N_SAMPLE, D_MODEL), F32)]
    else:
        out_specs = row_block
        out_shape = jax.ShapeDtypeStruct((N_TOK, D_MODEL), F32)
    return pl.pallas_call(
        functools.partial(_ple_kernel, final_norm=final_norm),
        grid=(N_TILE,),
        in_specs=[
            row_block,
            pl.BlockSpec((None, 1, D_MODEL), lambda i: (layer, 0, 0)),
            pl.BlockSpec((None, D_MODEL, D_MODEL), lambda i: (layer, 0, 0),
                         pipeline_mode=pl.Buffered(1)),
            pl.BlockSpec((None, TM, D_PLE), lambda i: (layer, i, 0)),
            pl.BlockSpec((None, N_SAMPLE, D_PLE), lambda i: (layer, 0, 0)),
            pl.BlockSpec((None, D_PLE, D_MODEL), lambda i: (layer, 0, 0),
                         pipeline_mode=pl.Buffered(1)),
            pl.BlockSpec((1, D_MODEL), lambda i: (0, 0)),
        ],
        out_specs=out_specs,
        out_shape=out_shape,
        scratch_shapes=[pltpu.VMEM((TM, D_PLE), BF16)],
        compiler_params=_ARB1,
        name="ple",
    )(h, gain, w_gate, p_prompt, p_sample, w_proj, g_final)


def kernel(x_prompt, x_sample, state_conv, p_prompt, p_sample, ffn1_norm, ffn1_w_gate, ffn1_w_up, ffn1_w_down, mix_norm, a_w_in, a_ln_g, a_ln_b, a_w_s, a_b_s, a_w_out, c_w_in, c_w_conv, c_w_out, ffn2_norm, ffn2_w_gate, ffn2_w_up, ffn2_w_down, ple_norm, ple_w_gate, ple_w_proj, final_norm):
    gain3 = lambda g: g.reshape(g.shape[0], 1, D_MODEL)

    h = (x_prompt.reshape(N_PROMPT, D_MODEL), x_sample.reshape(N_SAMPLE, D_MODEL))
    p_prompt = p_prompt.reshape(DEPTH, N_PROMPT, D_PLE)
    p_sample = p_sample.reshape(DEPTH, N_SAMPLE, D_PLE)

    f1 = (gain3(ffn1_norm), ffn1_w_gate, ffn1_w_up, ffn1_w_down)
    f2 = (gain3(ffn2_norm), ffn2_w_gate, ffn2_w_up, ffn2_w_down)
    mix_gain = gain3(mix_norm)
    ple_gain = gain3(ple_norm)
    a_w_in_b, a_w_out_b = a_w_in, a_w_out
    c_w_in_b = _conv_w_in_blocks(c_w_in)
    c_w_out_b = c_w_out
    ple_w_gate_b, ple_w_proj_b = ple_w_gate, ple_w_proj
    ln_g = a_ln_g.reshape(-1, N_HEAD, 1, D_HEAD)
    ln_b = a_ln_b.reshape(-1, N_HEAD, 1, D_HEAD)
    b_s = a_b_s.reshape(-1, N_HEAD, CHUNK, 1)
    g_final = final_norm.reshape(1, D_MODEL)

    conv_new_prompt, conv_new_sample, v_new = [], [], []
    for layer in range(DEPTH):
        h = _ffn(h, *f1, layer)
        jm = layer // 2
        if layer % 2 == 0:
            h, v_s = _gmlp(h, mix_gain, a_w_in_b, ln_g, ln_b, a_w_s, b_s, a_w_out_b, layer, jm)
            v_new.append(v_s.reshape(N_SAMPLE, 1, E_MIX))
        else:
            state = state_conv.reshape(-1, N_SAMPLE, 2 * E_MIX)
            h, ci_s, ci_p = _conv(h, mix_gain, c_w_in_b, c_w_conv, state, c_w_out_b, layer, jm)
            conv_new_prompt.append(ci_p[:, _HALO - 2:, :])
            conv_new_sample.append(jnp.stack([state_conv[jm, :, 1, :], ci_s], axis=1))
        h = _ffn(h, *f2, layer)
        h = _ple(h, ple_gain, ple_w_gate_b, p_prompt, p_sample, ple_w_proj_b, g_final, layer,
                 final_norm=(layer == DEPTH - 1))

    y_prompt, y_sample = h
    return (y_prompt.reshape(N_PROMPT // SEQ, SEQ, D_MODEL), y_sample.reshape(N_SAMPLE, 1, D_MODEL),
            jnp.stack(conv_new_prompt), jnp.stack(conv_new_sample), jnp.stack(v_new))
```
